```python
import math
import jax, jax.numpy as jnp
from jax import lax
import numpy as np

D_MODEL = 2048
BATCH = 32
SEQ = 256
DEPTH = 1
DEC_BATCH = 4
DEC_SEQ = 4096
PAST_LEN = 512

GRID_W = 64
NA_HEADS = 16
NA_HEAD_DIM = 64
NA_WIDTH = NA_HEADS * NA_HEAD_DIM
WIN_R = 8
WIN_C = 16
S5_GROUP_CH = 16
S5_WIDTH = 1024
S5_GROUPS = S5_WIDTH // S5_GROUP_CH
S5_STATE = 64
D_FF = 5632
N_MOD = 9
Q_BLOCK = 128
EPS = 1e-6
IN_COLS = 3 * NA_WIDTH + S5_WIDTH + 2 * D_MODEL

kernel_name = 'hybrid_natten_s5_macaron_prefix_step'


def rms_norm(x, g):
    x32 = x.astype(jnp.float32)
    y = x32 * lax.rsqrt(jnp.mean(x32 * x32, axis=-1, keepdims=True) + EPS)
    return (y * g.astype(jnp.float32)).astype(x.dtype)


def modulate(h, shift, scale):
    return h * (1 + scale) + shift


def swiglu(h, w_in, w_out):
    g, u = jnp.split(h @ w_in, 2, axis=-1)
    return (jax.nn.silu(g) * u) @ w_out


def split_heads(t):
    b, l, _ = t.shape
    return t.reshape(b, l, NA_HEADS, NA_HEAD_DIM).transpose(0, 2, 1, 3)


def merge_heads(t):
    b, h, l, d = t.shape
    return t.transpose(0, 2, 1, 3).reshape(b, l, h * d)


def context_attention(q, k, v):
    b, h, l, d = q.shape
    nb = l // Q_BLOCK
    scale = NA_HEAD_DIM ** -0.5
    qb = jnp.moveaxis(q.reshape(b, h, nb, Q_BLOCK, d), 2, 0)

    def block(q_blk):
        s = jnp.einsum('bhqd,bhkd->bhqk', q_blk, k).astype(jnp.float32) * scale
        p = jax.nn.softmax(s, axis=-1).astype(v.dtype)
        return jnp.einsum('bhqk,bhkd->bhqd', p, v)

    o = lax.map(block, qb)
    return jnp.moveaxis(o, 0, 2).reshape(b, h, l, d)


def neighbourhood_attention(q, k, v, ck, cv, rpb):
    b, h, t, d = q.shape
    rows = t // GRID_W
    kr = min(WIN_R, rows)
    n_keys = kr * WIN_C
    scale = NA_HEAD_DIM ** -0.5
    r = jnp.arange(rows)
    col = jnp.arange(GRID_W)
    key_rows = jnp.clip(r - kr // 2, 0, rows - kr)[:, None] + jnp.arange(kr)[None, :]
    key_cols = jnp.clip(col - WIN_C // 2, 0, GRID_W - WIN_C)[:, None] + jnp.arange(WIN_C)[None, :]
    idx = (key_rows[:, None, :, None] * GRID_W + key_cols[None, :, None, :]).reshape(rows, GRID_W, n_keys)
    dr = key_rows - r[:, None] + (WIN_R - 1)
    dc = key_cols - col[:, None] + (WIN_C - 1)
    bias = rpb[:, dr[:, None, :, None], dc[None, :, None, :]]
    bias = jnp.moveaxis(bias.reshape(h, rows, GRID_W, n_keys), 1, 0)
    q_rows = jnp.moveaxis(q.reshape(b, h, rows, GRID_W, d), 2, 0)

    def row_block(args):
        q_r, idx_r, bias_r = args
        k_w = jnp.take(k, idx_r, axis=2)
        v_w = jnp.take(v, idx_r, axis=2)
        s_loc = jnp.einsum('bhwd,bhwnd->bhwn', q_r, k_w).astype(jnp.float32) * scale + bias_r.astype(jnp.float32)
        s_ctx = jnp.einsum('bhwd,bhcd->bhwc', q_r, ck).astype(jnp.float32) * scale
        p = jax.nn.softmax(jnp.concatenate([s_loc, s_ctx], axis=-1), axis=-1).astype(v.dtype)
        return (jnp.einsum('bhwn,bhwnd->bhwd', p[..., :n_keys], v_w)
                + jnp.einsum('bhwc,bhcd->bhwd', p[..., n_keys:], cv))

    o = lax.map(row_block, (q_rows, idx, bias))
    return jnp.moveaxis(o, 0, 2).reshape(b, h, t, d)


def s5_discretize(lam_re, lam_im, log_dt, b_re, b_im):
    f32 = jnp.float32
    lam_re, lam_im = lam_re.astype(f32), lam_im.astype(f32)
    dt = jnp.exp(log_dt.astype(f32))[..., None]
    ldr, ldi = lam_re * dt, lam_im * dt
    ea = jnp.exp(ldr)
    a_re, a_im = ea * jnp.cos(ldi), ea * jnp.sin(ldi)
    mag2 = lam_re * lam_re + lam_im * lam_im
    f_re = ((a_re - 1) * lam_re + a_im * lam_im) / mag2
    f_im = (a_im * lam_re - (a_re - 1) * lam_im) / mag2
    b_re, b_im = b_re.astype(f32), b_im.astype(f32)
    bb_re = f_re[..., None] * b_re - f_im[..., None] * b_im
    bb_im = f_re[..., None] * b_im + f_im[..., None] * b_re
    return ldr, ldi, bb_re, bb_im


def ssm_scan(ug, ldr, ldi, bb_re, bb_im, h0, reverse):
    l = ug.shape[1]
    bu_re = jnp.einsum('blgh,gph->blgp', ug, bb_re)
    bu_im = jnp.einsum('blgh,gph->blgp', ug, bb_im)
    ea = jnp.exp(ldr)
    a_re = jnp.broadcast_to(ea * jnp.cos(ldi), bu_re.shape)
    a_im = jnp.broadcast_to(ea * jnp.sin(ldi), bu_re.shape)

    def combine(e1, e2):
        a1r, a1i, b1r, b1i = e1
        a2r, a2i, b2r, b2i = e2
        return (a2r * a1r - a2i * a1i, a2r * a1i + a2i * a1r,
                a2r * b1r - a2i * b1i + b2r, a2r * b1i + a2i * b1r + b2i)

    _, _, s_re, s_im = lax.associative_scan(combine, (a_re, a_im, bu_re, bu_im), reverse=reverse, axis=1)
    if h0 is not None:
        n = jnp.arange(l, 0, -1, dtype=jnp.float32) if reverse else jnp.arange(1, l + 1, dtype=jnp.float32)
        mag = jnp.exp(ldr[None] * n[:, None, None])
        ang = ldi[None] * n[:, None, None]
        p_re, p_im = mag * jnp.cos(ang), mag * jnp.sin(ang)
        h0r, h0i = h0[0][:, None], h0[1][:, None]
        s_re = s_re + p_re * h0r - p_im * h0i
        s_im = s_im + p_re * h0i + p_im * h0r
    last = 0 if reverse else -1
    return s_re, s_im, s_re[:, last], s_im[:, last]


def s5_branch(u, lam_re, lam_im, log_dt, b_re, b_im, c_re, c_im, d_skip, w_glu, h0):
    f32 = jnp.float32
    b, l, _ = u.shape
    ug = u.astype(f32).reshape(b, l, S5_GROUPS, S5_GROUP_CH)
    ldr, ldi, bb_re, bb_im = s5_discretize(lam_re, lam_im, log_dt, b_re, b_im)
    y = d_skip.astype(f32).reshape(S5_GROUPS, S5_GROUP_CH) * ug
    finals = []
    for dirn in range(2):
        h0_d = None if h0 is None else (h0[:, dirn, 0].astype(f32), h0[:, dirn, 1].astype(f32))
        s_re, s_im, f_re, f_im = ssm_scan(ug, ldr[dirn], ldi[dirn], bb_re[dirn], bb_im[dirn], h0_d, dirn == 1)
        y = (y + jnp.einsum('blgp,ghp->blgh', s_re, c_re[dirn].astype(f32))
             - jnp.einsum('blgp,ghp->blgh', s_im, c_im[dirn].astype(f32)))
        if h0 is None:
            finals.append(jnp.stack([f_re, f_im], axis=1))
    y = jax.nn.gelu(y.reshape(b, l, S5_WIDTH)).astype(u.dtype)
    y = y * jax.nn.sigmoid(y @ w_glu)
    state = jnp.stack(finals, axis=1) if h0 is None else None
    return y, state


def trunk_layer(x, cond, ctx, p):
    mods = jax.nn.silu(cond) @ p['w_ada'] + p['b_ada']
    sh1, sc1, g1, sh2, sc2, g2, sh3, sc3, g3 = [m[:, None, :] for m in jnp.split(mods, N_MOD, axis=-1)]
    h = modulate(rms_norm(x, p['norm_g'][0]), sh1, sc1)
    x = x + 0.5 * g1 * swiglu(h, p['ffn_in'][0], p['ffn_out'][0])

    h = modulate(rms_norm(x, p['norm_g'][1]), sh2, sc2)
    proj = h @ p['w_in']
    q, k, v, u, ga, gb = jnp.split(
        proj, [NA_WIDTH, 2 * NA_WIDTH, 3 * NA_WIDTH, 3 * NA_WIDTH + S5_WIDTH,
               3 * NA_WIDTH + S5_WIDTH + D_MODEL], axis=-1)
    q, k, v = split_heads(q), split_heads(k), split_heads(v)
    if ctx is None:
        ya = context_attention(q, k, v)
        h0 = None
    else:
        ck, cv, h0 = ctx
        ya = neighbourhood_attention(q, k, v, ck, cv, p['rpb'])
    yb, s_final = s5_branch(u, p['lam_re'], p['lam_im'], p['log_dt'], p['b_re'], p['b_im'],
                            p['c_re'], p['c_im'], p['d'], p['w_glu'], h0)
    merged = (jax.nn.sigmoid(ga) * (merge_heads(ya) @ p['w_up_a'])
              + jax.nn.sigmoid(gb) * (yb @ p['w_up_b']))
    x = x + g2 * (merged @ p['w_out'])

    h = modulate(rms_norm(x, p['norm_g'][2]), sh3, sc3)
    x = x + 0.5 * g3 * swiglu(h, p['ffn_in'][1], p['ffn_out'][1])
    ctx_out = (k, v, s_final) if ctx is None else None
    return x, ctx_out


def setup_inputs(seed: int = 0) -> dict:
    key = jax.random.key(seed)
    ks = jax.random.split(key, 32)
    f32 = jnp.float32

    def nrm(k, shape, scale=1.0):
        return jax.random.normal(k, shape, f32) * scale

    lam_im = (jnp.pi * jnp.arange(S5_STATE, dtype=f32))[None, None, None, :] + nrm(ks[14], (DEPTH, 2, S5_GROUPS, S5_STATE), 0.01)
    return {
        'x_prompt': nrm(ks[0], (BATCH, SEQ, D_MODEL)),
        'x_sample': nrm(ks[1], (DEC_BATCH, DEC_SEQ, D_MODEL)),
        'cache_k': nrm(ks[2], (DEC_BATCH, DEPTH, NA_HEADS, PAST_LEN, NA_HEAD_DIM)),
        'cache_v': nrm(ks[3], (DEC_BATCH, DEPTH, NA_HEADS, PAST_LEN, NA_HEAD_DIM)),
        'state_ssm': nrm(ks[4], (DEC_BATCH, DEPTH, 2, 2, S5_GROUPS, S5_STATE), 0.1),
        'c': nrm(ks[5], (DEC_BATCH, D_MODEL)),
        'c_ctx': nrm(ks[6], (D_MODEL,)),
        'w_ada': nrm(ks[7], (DEPTH, D_MODEL, N_MOD * D_MODEL), 0.5 * D_MODEL ** -0.5),
        'b_ada': nrm(ks[8], (DEPTH, N_MOD * D_MODEL), 0.01),
        'norm_g': 1.0 + nrm(ks[9], (DEPTH, 3, D_MODEL), 0.02),
        'ffn_in': nrm(ks[10], (DEPTH, 2, D_MODEL, 2 * D_FF), D_MODEL ** -0.5),
        'ffn_out': nrm(ks[11], (DEPTH, 2, D_FF, D_MODEL), D_FF ** -0.5),
        'w_in': nrm(ks[12], (DEPTH, D_MODEL, IN_COLS), D_MODEL ** -0.5),
        'rpb': nrm(ks[13], (DEPTH, NA_HEADS, 2 * WIN_R - 1, 2 * WIN_C - 1), 0.1),
        's5_lam_re': -0.5 + nrm(ks[15], (DEPTH, 2, S5_GROUPS, S5_STATE), 0.01),
        's5_lam_im': lam_im,
        's5_log_dt': jax.random.uniform(ks[16], (DEPTH, 2, S5_GROUPS), f32, math.log(1e-3), math.log(1e-1)),
        's5_b_re': nrm(ks[17], (DEPTH, 2, S5_GROUPS, S5_STATE, S5_GROUP_CH), (2 * S5_GROUP_CH) ** -0.5),
        's5_b_im': nrm(ks[18], (DEPTH, 2, S5_GROUPS, S5_STATE, S5_GROUP_CH), (2 * S5_GROUP_CH) ** -0.5),
        's5_c_re': nrm(ks[19], (DEPTH, 2, S5_GROUPS, S5_GROUP_CH, S5_STATE), S5_STATE ** -0.5),
        's5_c_im': nrm(ks[20], (DEPTH, 2, S5_GROUPS, S5_GROUP_CH, S5_STATE), S5_STATE ** -0.5),
        's5_d': nrm(ks[21], (DEPTH, S5_WIDTH)),
        'w_glu': nrm(ks[22], (DEPTH, S5_WIDTH, S5_WIDTH), S5_WIDTH ** -0.5),
        'w_up_a': nrm(ks[23], (DEPTH, NA_WIDTH, D_MODEL), NA_WIDTH ** -0.5),
        'w_up_b': nrm(ks[24], (DEPTH, S5_WIDTH, D_MODEL), S5_WIDTH ** -0.5),
        'w_out': nrm(ks[25], (DEPTH, D_MODEL, D_MODEL), D_MODEL ** -0.5),
        'final_g': 1.0 + nrm(ks[26], (D_MODEL,), 0.02),
    }


def reference(x_prompt, x_sample, cache_k, cache_v, state_ssm, c, c_ctx, w_ada, b_ada, norm_g,
              ffn_in, ffn_out, w_in, rpb, s5_lam_re, s5_lam_im, s5_log_dt, s5_b_re, s5_b_im,
              s5_c_re, s5_c_im, s5_d, w_glu, w_up_a, w_up_b, w_out, final_g):
    xp, xs = x_prompt, x_sample
    new_k, new_v, new_s = [], [], []
    for l in range(DEPTH):
        p = {
            'w_ada': w_ada[l], 'b_ada': b_ada[l], 'norm_g': norm_g[l],
            'ffn_in': ffn_in[l], 'ffn_out': ffn_out[l], 'w_in': w_in[l], 'rpb': rpb[l],
            'lam_re': s5_lam_re[l], 'lam_im': s5_lam_im[l], 'log_dt': s5_log_dt[l],
            'b_re': s5_b_re[l], 'b_im': s5_b_im[l], 'c_re': s5_c_re[l], 'c_im': s5_c_im[l],
            'd': s5_d[l], 'w_glu': w_glu[l], 'w_up_a': w_up_a[l], 'w_up_b': w_up_b[l], 'w_out': w_out[l],
        }
        xp, (k_l, v_l, s_l) = trunk_layer(xp, c_ctx[None, :], None, p)
        new_k.append(k_l)
        new_v.append(v_l)
        new_s.append(s_l)
        xs, _ = trunk_layer(xs, c, (cache_k[:, l], cache_v[:, l], state_ssm[:, l]), p)
    y_prompt = rms_norm(xp, final_g)
    y_sample = rms_norm(xs, final_g)
    new_cache_k = jnp.stack(new_k, axis=1)
    new_cache_v = jnp.stack(new_v, axis=1)
    new_state_ssm = jnp.stack(new_s, axis=1)
    return (y_prompt, y_sample, new_cache_k, new_cache_v, new_state_ssm)
```

```python
import functools
import math

import numpy as np
import jax
import jax.numpy as jnp
from jax import lax
from jax.experimental import pallas as pl
from jax.experimental.pallas import tpu as pltpu

F32 = jnp.float32
BF16 = jnp.bfloat16

GRID_W = 64
NA_HEADS = 16
NA_HEAD_DIM = 64
NA_WIDTH = NA_HEADS * NA_HEAD_DIM
WIN_R = 8
WIN_C = 16
S5_GROUP_CH = 16
S5_WIDTH = 1024
S5_GROUPS = S5_WIDTH // S5_GROUP_CH
S5_STATE = 64
N_MOD = 9
EPS = 1e-6

LANES = 128
HEADS_PER_SLAB = LANES // NA_HEAD_DIM
N_HEAD_SLABS = NA_WIDTH // LANES
GROUPS_PER_SLAB = LANES // S5_GROUP_CH
N_GROUP_SLABS = S5_WIDTH // LANES
SLAB_STATE = GROUPS_PER_SLAB * S5_STATE
SLAB_COLS = 4 * SLAB_STATE
S5_CHUNK = 8
NA_QROWS = 4
NA_KROWS = NA_QROWS + WIN_R
NEG_BIAS = -1e30
VMEM_LIMIT = 56 * 1024 * 1024


def _cparams(sem, limit=VMEM_LIMIT):
    return pltpu.CompilerParams(dimension_semantics=sem, vmem_limit_bytes=limit)


def _silu(x):
    return x * jax.nn.sigmoid(x)


def _norm_mod(x, g, shift, scale):
    y = x * lax.rsqrt(jnp.mean(x * x, axis=-1, keepdims=True) + EPS) * g
    return y * (1.0 + scale) + shift


def _mods_kernel(c_ref, w_ref, b_ref, o_ref):
    a = _silu(c_ref[...])
    o_ref[...] = jnp.dot(a, w_ref[...], preferred_element_type=F32,
                         precision=lax.Precision.HIGHEST) + b_ref[...]


def _mods(cond, w_ada, b_ada):
    rows, d = cond.shape
    n = w_ada.shape[1]
    tn = math.gcd(1024, n)
    return pl.pallas_call(
        _mods_kernel,
        grid=(n // tn,),
        in_specs=[pl.BlockSpec((rows, d), lambda j: (0, 0)),
                  pl.BlockSpec((d, tn), lambda j: (0, j)),
                  pl.BlockSpec((1, tn), lambda j: (0, j))],
        out_specs=pl.BlockSpec((rows, tn), lambda j: (0, j)),
        out_shape=jax.ShapeDtypeStruct((rows, n), F32),
        compiler_params=_cparams(("arbitrary",)),
        name="mods",
    )(cond, w_ada, b_ada.reshape(1, n))


def _ffn_kernel(x_ref, mods_ref, ng_ref, fg_ref, wg_ref, wu_ref, wo_ref, o_ref, h_ref, acc_ref,
                *, mod_base, final_norm):
    j = pl.program_id(1)

    @pl.when(j == 0)
    def _():
        m = mods_ref[0]
        h = _norm_mod(x_ref[...], ng_ref[...], m[mod_base:mod_base + 1], m[mod_base + 1:mod_base + 2])
        h_ref[...] = h.astype(BF16)

    h = h_ref[...]
    g = jnp.dot(h, wg_ref[...], preferred_element_type=F32)
    u = jnp.dot(h, wu_ref[...], preferred_element_type=F32)
    a = (_silu(g) * u).astype(BF16)
    part = jnp.dot(a, wo_ref[...], preferred_element_type=F32)

    @pl.when(j == 0)
    def _():
        acc_ref[...] = part

    @pl.when(j > 0)
    def _():
        acc_ref[...] += part

    @pl.when(j == pl.num_programs(1) - 1)
    def _():
        gate = mods_ref[0][mod_base + 2:mod_base + 3]
        y = x_ref[...] + (0.5 * gate) * acc_ref[...]
        if final_norm:
            y = y * lax.rsqrt(jnp.mean(y * y, axis=-1, keepdims=True) + EPS) * fg_ref[...]
        o_ref[...] = y


def _ffn(x, mods3, row0, tiles_per_row, norm_g, final_g, w_in_bf, w_out_bf, mod_base, final_norm):
    n_tok, d = x.shape
    d_ff = w_out_bf.shape[0]
    tm = min(512, n_tok)
    tf = min(512, d_ff)
    nj = d_ff // tf
    kern = functools.partial(_ffn_kernel, mod_base=mod_base, final_norm=final_norm)
    return pl.pallas_call(
        kern,
        grid=(n_tok // tm, nj),
        in_specs=[pl.BlockSpec((tm, d), lambda i, j: (i, 0)),
                  pl.BlockSpec((1, N_MOD, d), lambda i, j: (row0 + i // tiles_per_row, 0, 0)),
                  pl.BlockSpec((1, d), lambda i, j: (0, 0)),
                  pl.BlockSpec((1, d), lambda i, j: (0, 0)),
                  pl.BlockSpec((d, tf), lambda i, j: (0, j)),
                  pl.BlockSpec((d, tf), lambda i, j: (0, nj + j)),
                  pl.BlockSpec((tf, d), lambda i, j: (j, 0))],
        out_specs=pl.BlockSpec((tm, d), lambda i, j: (i, 0)),
        out_shape=jax.ShapeDtypeStruct((n_tok, d), F32),
        scratch_shapes=[pltpu.VMEM((tm, d), BF16), pltpu.VMEM((tm, d), F32)],
        compiler_params=_cparams(("parallel", "arbitrary")),
        name="ffn",
    )(x, mods3, norm_g.reshape(1, d), final_g.reshape(1, d), w_in_bf, w_in_bf, w_out_bf)


def _proj_kernel(x_ref, mods_ref, ng_ref, w_ref, o_ref, h_ref):
    @pl.when(pl.program_id(1) == 0)
    def _():
        m = mods_ref[0]
        h_ref[...] = _norm_mod(x_ref[...], ng_ref[...], m[3:4], m[4:5]).astype(BF16)

    o_ref[...] = jnp.dot(h_ref[...], w_ref[...], preferred_element_type=F32)


def _proj(x, mods3, row0, tiles_per_row, norm_g, w_bf):
    n_tok, d = x.shape
    n = w_bf.shape[1]
    tm = min(512, n_tok)
    tn = math.gcd(1024, d)
    return pl.pallas_call(
        _proj_kernel,
        grid=(n_tok // tm, n // tn),
        in_specs=[pl.BlockSpec((tm, d), lambda i, j: (i, 0)),
                  pl.BlockSpec((1, N_MOD, d), lambda i, j: (row0 + i // tiles_per_row, 0, 0)),
                  pl.BlockSpec((1, d), lambda i, j: (0, 0)),
                  pl.BlockSpec((d, tn), lambda i, j: (0, j))],
        out_specs=pl.BlockSpec((tm, tn), lambda i, j: (i, j)),
        out_shape=jax.ShapeDtypeStruct((n_tok, n), F32),
        scratch_shapes=[pltpu.VMEM((tm, d), BF16)],
        compiler_params=_cparams(("parallel", "arbitrary")),
        name="proj",
    )(x, mods3, norm_g.reshape(1, d), w_bf)


def _head_mask(shape, hh):
    lane = lax.broadcasted_iota(jnp.int32, shape, 1)
    return (lane >= hh * NA_HEAD_DIM) & (lane < (hh + 1) * NA_HEAD_DIM)


def _nt_dot(a, b):
    return lax.dot_general(a, b, (((1,), (1,)), ((), ())), preferred_element_type=F32)


def _attn_ctx_kernel(q_ref, k_ref, v_ref, o_ref):
    q = q_ref[...] * (NA_HEAD_DIM ** -0.5)
    k = k_ref[...].astype(BF16)
    v = v_ref[...].astype(BF16)
    outs = []
    for hh in range(HEADS_PER_SLAB):
        qm = jnp.where(_head_mask(q.shape, hh), q, 0.0).astype(BF16)
        s = _nt_dot(qm, k)
        p = jnp.exp(s - jnp.max(s, axis=-1, keepdims=True))
        l = jnp.sum(p, axis=-1, keepdims=True)
        outs.append(jnp.dot(p.astype(BF16), v, preferred_element_type=F32) / l)
    o_ref[...] = jnp.where(_head_mask(q.shape, 0), outs[0], outs[1])


def _attn_ctx(proj, batch, seq):
    return pl.pallas_call(
        _attn_ctx_kernel,
        grid=(batch, N_HEAD_SLABS),
        in_specs=[pl.BlockSpec((seq, LANES), lambda b, s: (b, s)),
                  pl.BlockSpec((seq, LANES), lambda b, s: (b, N_HEAD_SLABS + s)),
                  pl.BlockSpec((seq, LANES), lambda b, s: (b, 2 * N_HEAD_SLABS + s))],
        out_specs=pl.BlockSpec((seq, LANES), lambda b, s: (b, s)),
        out_shape=jax.ShapeDtypeStruct((batch * seq, NA_WIDTH), F32),
        compiler_params=_cparams(("parallel", "parallel")),
        name="attn_ctx",
    )(proj, proj, proj)


def _attn_na_kernel(q_ref, k_ref, v_ref, ck_ref, cv_ref, bias_ref, o_ref,
                    kb_ref, vb_ref, ckb_ref, cvb_ref, *, rows):
    rb = pl.program_id(2)

    @pl.when(rb == 0)
    def _():
        kb_ref[...] = k_ref[...].astype(BF16)
        vb_ref[...] = v_ref[...].astype(BF16)
        ckb_ref[...] = ck_ref[...].astype(BF16)
        cvb_ref[...] = cv_ref[...].astype(BF16)

    start = jnp.clip(rb * NA_QROWS - WIN_R // 2, 0, rows - NA_KROWS)
    off = pl.multiple_of(start * GRID_W, NA_QROWS * GRID_W)
    kw = kb_ref[pl.ds(off, NA_KROWS * GRID_W), :]
    vw = vb_ref[pl.ds(off, NA_KROWS * GRID_W), :]
    ck = ckb_ref[...]
    cv = cvb_ref[...]
    q = q_ref[...] * (NA_HEAD_DIM ** -0.5)
    outs = []
    for hh in range(HEADS_PER_SLAB):
        qm = jnp.where(_head_mask(q.shape, hh), q, 0.0).astype(BF16)
        s_loc = _nt_dot(qm, kw) + bias_ref[0, hh]
        s_ctx = _nt_dot(qm, ck)
        m = jnp.maximum(jnp.max(s_loc, axis=-1, keepdims=True), jnp.max(s_ctx, axis=-1, keepdims=True))
        p_loc = jnp.exp(s_loc - m)
        p_ctx = jnp.exp(s_ctx - m)
        l = jnp.sum(p_loc, axis=-1, keepdims=True) + jnp.sum(p_ctx, axis=-1, keepdims=True)
        o = (jnp.dot(p_loc.astype(BF16), vw, preferred_element_type=F32)
             + jnp.dot(p_ctx.astype(BF16), cv, preferred_element_type=F32))
        outs.append(o / l)
    o_ref[...] = jnp.where(_head_mask(q.shape, 0), outs[0], outs[1])


def _na_bias_tables(rpb, rows):
    tables = []
    qc = np.arange(GRID_W)
    kc = np.arange(GRID_W)
    cs = np.clip(qc - WIN_C // 2, 0, GRID_W - WIN_C)
    valid_c = (kc[None, :] >= cs[:, None]) & (kc[None, :] < cs[:, None] + WIN_C)
    dc = np.clip(kc[None, :] - qc[:, None] + WIN_C - 1, 0, 2 * WIN_C - 2)
    for r0 in (0, NA_QROWS, rows - NA_QROWS):
        q_row = r0 + np.arange(NA_QROWS)
        ws = np.clip(q_row - WIN_R // 2, 0, rows - WIN_R)
        start = int(np.clip(r0 - WIN_R // 2, 0, rows - NA_KROWS))
        k_row = start + np.arange(NA_KROWS)
        valid_r = (k_row[None, :] >= ws[:, None]) & (k_row[None, :] < ws[:, None] + WIN_R)
        dr = np.clip(k_row[None, :] - q_row[:, None] + WIN_R - 1, 0, 2 * WIN_R - 2)
        b = rpb[:, dr[:, None, :, None], dc[None, :, None, :]]
        valid = valid_r[:, None, :, None] & valid_c[None, :, None, :]
        b = jnp.where(valid[None], b, NEG_BIAS)
        tables.append(b.reshape(NA_HEADS, NA_QROWS * GRID_W, NA_KROWS * GRID_W))
    return jnp.stack(tables).astype(F32)


def _attn_na(proj, ck_tok, cv_tok, bias, batch, seq, past):
    rows = seq // GRID_W
    assert rows >= NA_KROWS and rows % NA_QROWS == 0
    nrb = rows // NA_QROWS
    tq = NA_QROWS * GRID_W

    def bias_map(b, s, r):
        return ((r > 0).astype(jnp.int32) + (r == nrb - 1).astype(jnp.int32), s, 0, 0)

    return pl.pallas_call(
        functools.partial(_attn_na_kernel, rows=rows),
        grid=(batch, N_HEAD_SLABS, nrb),
        in_specs=[pl.BlockSpec((tq, LANES), lambda b, s, r: (b * nrb + r, s)),
                  pl.BlockSpec((seq, LANES), lambda b, s, r: (b, N_HEAD_SLABS + s)),
                  pl.BlockSpec((seq, LANES), lambda b, s, r: (b, 2 * N_HEAD_SLABS + s)),
                  pl.BlockSpec((past, LANES), lambda b, s, r: (b, s)),
                  pl.BlockSpec((past, LANES), lambda b, s, r: (b, s)),
                  pl.BlockSpec((1, HEADS_PER_SLAB, tq, NA_KROWS * GRID_W), bias_map)],
        out_specs=pl.BlockSpec((tq, LANES), lambda b, s, r: (b * nrb + r, s)),
        out_shape=jax.ShapeDtypeStruct((batch * seq, NA_WIDTH), F32),
        scratch_shapes=[pltpu.VMEM((seq, LANES), BF16), pltpu.VMEM((seq, LANES), BF16),
                        pltpu.VMEM((past, LANES), BF16), pltpu.VMEM((past, LANES), BF16)],
        compiler_params=_cparams(("parallel", "parallel", "arbitrary")),
        name="attn_na",
    )(proj, proj, proj, ck_tok, cv_tok, bias)


def _s5_prep_kernel(lr_ref, li_ref, ld_ref, lrc_ref, lic_ref, ldc_ref, btr_ref, bti_ref, ctr_ref, cti_ref,
                    e_ref, m_ref, f_ref, at_ref):
    T = S5_CHUNK
    ns = SLAB_STATE
    rb = lax.broadcasted_iota(jnp.int32, (LANES, ns), 0) // S5_GROUP_CH
    cb = lax.broadcasted_iota(jnp.int32, (LANES, ns), 1) // S5_STATE
    mask_b = rb == cb
    rc = lax.broadcasted_iota(jnp.int32, (ns, LANES), 0) // S5_STATE
    cc = lax.broadcasted_iota(jnp.int32, (ns, LANES), 1) // S5_GROUP_CH
    mask_c = rc == cc

    def powers(ldr, ldi, n):
        mag = jnp.exp(ldr * float(n))
        return mag * jnp.cos(ldi * float(n)), mag * jnp.sin(ldi * float(n))

    kt = [[None] * T, [None] * T]
    for d in range(2):
        lam_re, lam_im = lr_ref[0, d], li_ref[0, d]
        dt = jnp.exp(ld_ref[0, d])
        ldr, ldi = lam_re * dt, lam_im * dt
        a_re, a_im = powers(ldr, ldi, 1)
        mag2 = lam_re * lam_re + lam_im * lam_im
        f_re = ((a_re - 1.0) * lam_re + a_im * lam_im) / mag2
        f_im = (a_im * lam_re - (a_re - 1.0) * lam_im) / mag2
        btr = jnp.where(mask_b, btr_ref[0, d], 0.0)
        bti = jnp.where(mask_b, bti_ref[0, d], 0.0)
        ctr = jnp.where(mask_c, ctr_ref[0, d], 0.0)
        cti = jnp.where(mask_c, cti_ref[0, d], 0.0)
        c0 = d * 2 * ns
        for n in range(T):
            an_re, an_im = powers(ldr, ldi, n)
            w_re = f_re * an_re - f_im * an_im
            w_im = f_re * an_im + f_im * an_re
            e_re = btr * w_re - bti * w_im
            e_im = btr * w_im + bti * w_re
            t = T - 1 - n if d == 0 else n
            e_ref[0, t * LANES:(t + 1) * LANES, c0:c0 + ns] = e_re.astype(BF16)
            e_ref[0, t * LANES:(t + 1) * LANES, c0 + ns:c0 + 2 * ns] = e_im.astype(BF16)
            kt[d][n] = (jnp.dot(e_re, ctr, preferred_element_type=F32, precision=lax.Precision.HIGHEST)
                        - jnp.dot(e_im, cti, preferred_element_type=F32, precision=lax.Precision.HIGHEST))
        lam_re_c, lam_im_c = lrc_ref[0, d], lic_ref[0, d]
        dt_c = jnp.exp(ldc_ref[0, d])
        ldr_c, ldi_c = lam_re_c * dt_c, lam_im_c * dt_c
        for tp in range(T):
            n = tp + 1 if d == 0 else T - tp
            an_re, an_im = powers(ldr_c, ldi_c, n)
            f_ref[0, c0:c0 + ns, tp * LANES:(tp + 1) * LANES] = (ctr * an_re - cti * an_im).astype(BF16)
            f_ref[0, c0 + ns:c0 + 2 * ns, tp * LANES:(tp + 1) * LANES] = (-(ctr * an_im + cti * an_re)).astype(BF16)
        at_re, at_im = powers(ldr, ldi, T)
        at_ref[0, 2 * d:2 * d + 1, :] = at_re
        at_ref[0, 2 * d + 1:2 * d + 2, :] = at_im
    for t in range(T):
        for tp in range(T):
            if tp > t:
                blk = kt[0][tp - t]
            elif tp < t:
                blk = kt[1][t - tp]
            else:
                blk = kt[0][0] + kt[1][0]
            m_ref[0, t * LANES:(t + 1) * LANES, tp * LANES:(tp + 1) * LANES] = blk.astype(BF16)


def _s5_prep(lam_re, lam_im, log_dt, b_re, b_im, c_re, c_im):
    T, S, G8, ns = S5_CHUNK, N_GROUP_SLABS, GROUPS_PER_SLAB, SLAB_STATE

    def state_vec(v):
        return v.reshape(2, S, ns).transpose(1, 0, 2)

    lr, li = state_vec(lam_re), state_vec(lam_im)
    ld = state_vec(jnp.broadcast_to(log_dt[:, :, None], lam_re.shape))

    def bt(v):
        v = v.reshape(2, S, G8, S5_STATE, S5_GROUP_CH).transpose(1, 0, 2, 4, 3).reshape(S, 2, LANES, S5_STATE)
        return jnp.tile(v, (1, 1, 1, G8))

    def ct(v):
        v = v.reshape(2, S, G8, S5_GROUP_CH, S5_STATE).transpose(1, 0, 2, 4, 3).reshape(S, 2, ns, S5_GROUP_CH)
        return jnp.tile(v, (1, 1, 1, G8))

    row = lambda v: v[:, :, None, :]
    col = lambda v: v[:, :, :, None]
    spec_row = pl.BlockSpec((1, 2, 1, ns), lambda j: (j, 0, 0, 0))
    spec_col = pl.BlockSpec((1, 2, ns, 1), lambda j: (j, 0, 0, 0))
    spec_bt = pl.BlockSpec((1, 2, LANES, ns), lambda j: (j, 0, 0, 0))
    spec_ct = pl.BlockSpec((1, 2, ns, LANES), lambda j: (j, 0, 0, 0))
    return pl.pallas_call(
        _s5_prep_kernel,
        grid=(S,),
        in_specs=[spec_row, spec_row, spec_row, spec_col, spec_col, spec_col, spec_bt, spec_bt, spec_ct, spec_ct],
        out_specs=[pl.BlockSpec((1, T * LANES, SLAB_COLS), lambda j: (j, 0, 0)),
                   pl.BlockSpec((1, T * LANES, T * LANES), lambda j: (j, 0, 0)),
                   pl.BlockSpec((1, SLAB_COLS, T * LANES), lambda j: (j, 0, 0)),
                   pl.BlockSpec((1, 4, ns), lambda j: (j, 0, 0))],
        out_shape=[jax.ShapeDtypeStruct((S, T * LANES, SLAB_COLS), BF16),
                   jax.ShapeDtypeStruct((S, T * LANES, T * LANES), BF16),
                   jax.ShapeDtypeStruct((S, SLAB_COLS, T * LANES), BF16),
                   jax.ShapeDtypeStruct((S, 4, ns), F32)],
        compiler_params=_cparams(("parallel",)),
        name="s5_prep",
    )(row(lr), row(li), row(ld), col(lr), col(li), col(ld), bt(b_re), bt(b_im), ct(c_re), ct(c_im))


def _s5_kernel(u_ref, h0_ref, e_ref, m_ref, f_ref, at_ref, d_ref, y_ref, fin_ref,
               us_ref, ubf_ref, st_ref, ys_ref, *, bt, seq):
    T = S5_CHUNK
    ns = SLAB_STATE
    nc = seq // T
    for b in range(bt):
        for t in range(T):
            us_ref[t, pl.ds(b, nc, stride=bt), :] = u_ref[pl.ds(b * seq + t, nc, stride=T), :]
    for t in range(T):
        ubf_ref[:, t * LANES:(t + 1) * LANES] = us_ref[t].astype(BF16)
    ubf = ubf_ref[...]
    st_ref[...] = jnp.dot(ubf, e_ref[0], preferred_element_type=F32)

    at = at_ref[0]
    a_fr, a_fi, a_br, a_bi = at[0:1], at[1:2], at[2:3], at[3:4]
    h0 = h0_ref[0, 0]

    def step(i, carry):
        c_fr, c_fi, c_br, c_bi = carry
        rf = pl.multiple_of(i * bt, bt)
        rbk = pl.multiple_of((nc - 1 - i) * bt, bt)
        l_fr = st_ref[pl.ds(rf, bt), 0:ns]
        l_fi = st_ref[pl.ds(rf, bt), ns:2 * ns]
        l_br = st_ref[pl.ds(rbk, bt), 2 * ns:3 * ns]
        l_bi = st_ref[pl.ds(rbk, bt), 3 * ns:4 * ns]
        st_ref[pl.ds(rf, bt), 0:ns] = c_fr
        st_ref[pl.ds(rf, bt), ns:2 * ns] = c_fi
        st_ref[pl.ds(rbk, bt), 2 * ns:3 * ns] = c_br
        st_ref[pl.ds(rbk, bt), 3 * ns:4 * ns] = c_bi
        return (a_fr * c_fr - a_fi * c_fi + l_fr, a_fr * c_fi + a_fi * c_fr + l_fi,
                a_br * c_br - a_bi * c_bi + l_br, a_br * c_bi + a_bi * c_br + l_bi)

    fin = lax.fori_loop(0, nc, step, (h0[:, 0:ns], h0[:, ns:2 * ns], h0[:, 2 * ns:3 * ns], h0[:, 3 * ns:4 * ns]))
    for k in range(4):
        fin_ref[0, 0, :, k * ns:(k + 1) * ns] = fin[k]

    y = (jnp.dot(ubf, m_ref[0], preferred_element_type=F32)
         + jnp.dot(st_ref[...].astype(BF16), f_ref[0], preferred_element_type=F32))
    for t in range(T):
        ys_ref[t] = y[:, t * LANES:(t + 1) * LANES] + d_ref[0] * us_ref[t]
    for b in range(bt):
        for t in range(T):
            y_ref[pl.ds(b * seq + t, nc, stride=T), :] = ys_ref[t, pl.ds(b, nc, stride=bt), :]


def _s5(proj, u_col, h0, e, m, f, at, d_skip, batch, seq, bt):
    T, S = S5_CHUNK, N_GROUP_SLABS
    rows = bt * seq // T
    const = lambda j, g: (j, 0, 0)
    y, fin = pl.pallas_call(
        functools.partial(_s5_kernel, bt=bt, seq=seq),
        grid=(S, batch // bt),
        in_specs=[pl.BlockSpec((bt * seq, LANES), lambda j, g: (g, u_col + j)),
                  pl.BlockSpec((1, 1, bt, SLAB_COLS), lambda j, g: (j, g, 0, 0)),
                  pl.BlockSpec((1, T * LANES, SLAB_COLS), const),
                  pl.BlockSpec((1, T * LANES, T * LANES), const),
                  pl.BlockSpec((1, SLAB_COLS, T * LANES), const),
                  pl.BlockSpec((1, 4, SLAB_STATE), const),
                  pl.BlockSpec((1, 1, LANES), const)],
        out_specs=[pl.BlockSpec((bt * seq, LANES), lambda j, g: (g, j)),
                   pl.BlockSpec((1, 1, bt, SLAB_COLS), lambda j, g: (j, g, 0, 0))],
        out_shape=[jax.ShapeDtypeStruct((batch * seq, S5_WIDTH), F32),
                   jax.ShapeDtypeStruct((S, batch // bt, bt, SLAB_COLS), F32)],
        scratch_shapes=[pltpu.VMEM((T, rows, LANES), F32), pltpu.VMEM((rows, T * LANES), BF16),
                        pltpu.VMEM((rows, SLAB_COLS), F32), pltpu.VMEM((T, rows, LANES), F32)],
        compiler_params=_cparams(("parallel", "arbitrary")),
        name="s5",
    )(proj, h0.reshape(S, batch // bt, bt, SLAB_COLS), e, m, f, at, d_skip.reshape(S, 1, LANES))
    return y, fin.reshape(S, batch, SLAB_COLS)


def _state_to_slabs(state):
    b = state.shape[0]
    s = state.reshape(b, 2, 2, N_GROUP_SLABS, SLAB_STATE).transpose(3, 0, 1, 2, 4)
    return s.reshape(N_GROUP_SLABS, b, SLAB_COLS)


def _slabs_to_state(slabs):
    b = slabs.shape[1]
    s = slabs.reshape(N_GROUP_SLABS, b, 2, 2, SLAB_STATE).transpose(1, 2, 3, 0, 4)
    return s.reshape(b, 2, 2, S5_GROUPS, S5_STATE)


def _merge_kernel(x_ref, mods_ref, ya_ref, yb_ref, ga_ref, gb_ref, wglu_ref, wua_ref, wub_ref, wo_ref, o_ref):
    yb = jax.nn.gelu(yb_ref[...])
    glu = jnp.dot(yb.astype(BF16), wglu_ref[...], preferred_element_type=F32)
    yb = yb * jax.nn.sigmoid(glu)
    a = jnp.dot(ya_ref[...].astype(BF16), wua_ref[...], preferred_element_type=F32)
    b = jnp.dot(yb.astype(BF16), wub_ref[...], preferred_element_type=F32)
    merged = jax.nn.sigmoid(ga_ref[...]) * a + jax.nn.sigmoid(gb_ref[...]) * b
    o = jnp.dot(merged.astype(BF16), wo_ref[...], preferred_element_type=F32)
    o_ref[...] = x_ref[...] + mods_ref[0][5:6] * o


def _merge(x, mods3, row0, tiles_per_row, ya, yb, proj, w_glu, w_up_a, w_up_b, w_out):
    n_tok, d = x.shape
    tm = min(256, n_tok)
    gate_col = (3 * NA_WIDTH + S5_WIDTH) // d
    assert gate_col * d == 3 * NA_WIDTH + S5_WIDTH
    resident = functools.partial(pl.BlockSpec, pipeline_mode=pl.Buffered(1))
    return pl.pallas_call(
        _merge_kernel,
        grid=(n_tok // tm,),
        in_specs=[pl.BlockSpec((tm, d), lambda i: (i, 0)),
                  pl.BlockSpec((1, N_MOD, d), lambda i: (row0 + i // tiles_per_row, 0, 0)),
                  pl.BlockSpec((tm, NA_WIDTH), lambda i: (i, 0)),
                  pl.BlockSpec((tm, S5_WIDTH), lambda i: (i, 0)),
                  pl.BlockSpec((tm, d), lambda i: (i, gate_col)),
                  pl.BlockSpec((tm, d), lambda i: (i, gate_col + 1)),
                  resident((S5_WIDTH, S5_WIDTH), lambda i: (0, 0)),
                  resident((NA_WIDTH, d), lambda i: (0, 0)),
                  resident((S5_WIDTH, d), lambda i: (0, 0)),
                  resident((d, d), lambda i: (0, 0))],
        out_specs=pl.BlockSpec((tm, d), lambda i: (i, 0)),
        out_shape=jax.ShapeDtypeStruct((n_tok, d), F32),
        compiler_params=_cparams(("parallel",)),
        name="merge",
    )(x, mods3, ya, yb, proj, proj, w_glu, w_up_a, w_up_b, w_out)


def _heads_first(t, batch, seq):
    return t.reshape(batch, seq, NA_HEADS, NA_HEAD_DIM).transpose(0, 2, 1, 3)


def _tokens_first(t):
    b, h, l, dh = t.shape
    return t.transpose(0, 2, 1, 3).reshape(b * l, h * dh)


def kernel(x_prompt, x_sample, cache_k, cache_v, state_ssm, c, c_ctx, w_ada, b_ada, norm_g, ffn_in, ffn_out, w_in, rpb, s5_lam_re, s5_lam_im, s5_log_dt, s5_b_re, s5_b_im, s5_c_re, s5_c_im, s5_d, w_glu, w_up_a, w_up_b, w_out, final_g):
    depth = w_ada.shape[0]
    pb, pl_, d = x_prompt.shape
    sb, sl, _ = x_sample.shape
    past = cache_k.shape[3]
    u_col = 3 * NA_WIDTH // LANES
    mod_rows = 8 * ((1 + sb + 7) // 8)

    xp = x_prompt.reshape(pb * pl_, d)
    xs = x_sample.reshape(sb * sl, d)
    cond = jnp.concatenate([c_ctx[None, :], c, jnp.zeros((mod_rows - 1 - sb, d), F32)], axis=0)
    new_k, new_v, new_s = [], [], []
    for l in range(depth):
        last = l == depth - 1
        mods3 = _mods(cond, w_ada[l], b_ada[l]).reshape(mod_rows, N_MOD, d)
        ffn_in_bf = ffn_in[l].astype(BF16)
        ffn_out_bf = ffn_out[l].astype(BF16)
        w_in_bf = w_in[l].astype(BF16)
        w_glu_bf, w_up_a_bf, w_up_b_bf, w_out_bf = (w[l].astype(BF16) for w in (w_glu, w_up_a, w_up_b, w_out))
        e, m, f, at = _s5_prep(s5_lam_re[l], s5_lam_im[l], s5_log_dt[l], s5_b_re[l], s5_b_im[l],
                               s5_c_re[l], s5_c_im[l])
        bias = _na_bias_tables(rpb[l], sl // GRID_W)
        tiles_s = sl // min(512, sl)

        never = pb * pl_
        xp = _ffn(xp, mods3, 0, never, norm_g[l, 0], final_g, ffn_in_bf[0], ffn_out_bf[0], 0, False)
        proj_p = _proj(xp, mods3, 0, never, norm_g[l, 1], w_in_bf)
        ya_p = _attn_ctx(proj_p, pb, pl_)
        bt_p = math.gcd(pb, 16)
        yb_p, fin_p = _s5(proj_p, u_col, jnp.zeros((N_GROUP_SLABS, pb, SLAB_COLS), F32), e, m, f, at,
                          s5_d[l], pb, pl_, bt_p)
        xp = _merge(xp, mods3, 0, never, ya_p, yb_p, proj_p, w_glu_bf, w_up_a_bf, w_up_b_bf, w_out_bf)
        xp = _ffn(xp, mods3, 0, never, norm_g[l, 2], final_g, ffn_in_bf[1], ffn_out_bf[1], 6, last)
        new_k.append(_heads_first(proj_p[:, NA_WIDTH:2 * NA_WIDTH], pb, pl_))
        new_v.append(_heads_first(proj_p[:, 2 * NA_WIDTH:3 * NA_WIDTH], pb, pl_))
        new_s.append(_slabs_to_state(fin_p))

        xs = _ffn(xs, mods3, 1, tiles_s, norm_g[l, 0], final_g, ffn_in_bf[0], ffn_out_bf[0], 0, False)
        proj_s = _proj(xs, mods3, 1, tiles_s, norm_g[l, 1], w_in_bf)
        ya_s = _attn_na(proj_s, _tokens_first(cache_k[:, l]), _tokens_first(cache_v[:, l]), bias, sb, sl, past)
        yb_s, _ = _s5(proj_s, u_col, _state_to_slabs(state_ssm[:, l]), e, m, f, at, s5_d[l], sb, sl, 1)
        xs = _merge(xs, mods3, 1, sl // min(256, sl), ya_s, yb_s, proj_s, w_glu_bf, w_up_a_bf, w_up_b_bf, w_out_bf)
        xs = _ffn(xs, mods3, 1, tiles_s, norm_g[l, 2], final_g, ffn_in_bf[1], ffn_out_bf[1], 6, last)

    y_prompt = xp.reshape(pb, pl_, d)
    y_sample = xs.reshape(sb, sl, d)
    return (y_prompt, y_sample, jnp.stack(new_k, axis=1), jnp.stack(new_v, axis=1), jnp.stack(new_s, axis=1))
```

```python
import functools
import math

import numpy as np
import jax
import jax.numpy as jnp
from jax import lax
from jax.experimental import pallas as pl
from jax.experimental.pallas import tpu as pltpu

F32 = jnp.float32
BF16 = jnp.bfloat16

GRID_W = 64
NA_HEADS = 16
NA_HEAD_DIM = 64
NA_WIDTH = NA_HEADS * NA_HEAD_DIM
WIN_R = 8
WIN_C = 16
S5_GROUP_CH = 16
S5_WIDTH = 1024
S5_GROUPS = S5_WIDTH // S5_GROUP_CH
S5_STATE = 64
N_MOD = 9
EPS = 1e-6

LANES = 128
HEADS_PER_SLAB = LANES // NA_HEAD_DIM
N_HEAD_SLABS = NA_WIDTH // LANES
GROUPS_PER_SLAB = LANES // S5_GROUP_CH
N_GROUP_SLABS = S5_WIDTH // LANES
SLAB_STATE = GROUPS_PER_SLAB * S5_STATE
SLAB_COLS = 4 * SLAB_STATE
S5_CHUNK = 8
S5_SEGMENTS = 8
NA_QROWS = 4
NA_KROWS = NA_QROWS + WIN_R
NEG_BIAS = -1e30
LOG2E = 1.4426950408889634
VMEM_LIMIT = 56 * 1024 * 1024


def _cparams(sem, limit=VMEM_LIMIT):
    return pltpu.CompilerParams(dimension_semantics=sem, vmem_limit_bytes=limit)


def _silu(x):
    return x * jax.nn.sigmoid(x)


def _norm_mod(x, g, shift, scale):
    y = x * lax.rsqrt(jnp.mean(x * x, axis=-1, keepdims=True) + EPS) * g
    return y * (1.0 + scale) + shift


def _mods_kernel(c_ref, w_ref, b_ref, o_ref):
    a = _silu(c_ref[...])
    o_ref[...] = jnp.dot(a, w_ref[...], preferred_element_type=F32,
                         precision=lax.Precision.HIGHEST) + b_ref[...]


def _mods(cond, w_ada, b_ada):
    rows, d = cond.shape
    n = w_ada.shape[1]
    tn = math.gcd(1024, n)
    return pl.pallas_call(
        _mods_kernel,
        grid=(n // tn,),
        in_specs=[pl.BlockSpec((rows, d), lambda j: (0, 0)),
                  pl.BlockSpec((d, tn), lambda j: (0, j)),
                  pl.BlockSpec((1, tn), lambda j: (0, j))],
        out_specs=pl.BlockSpec((rows, tn), lambda j: (0, j)),
        out_shape=jax.ShapeDtypeStruct((rows, n), F32),
        compiler_params=_cparams(("arbitrary",)),
        name="mods",
    )(cond, w_ada, b_ada.reshape(1, n))


def _ffn_kernel(x_ref, mods_ref, ng_ref, fg_ref, w_ref, wo_ref, o_ref, h_ref, acc_ref,
                *, mod_base, final_norm):
    j = pl.program_id(1)
    tf = wo_ref.shape[0]

    @pl.when(j == 0)
    def _():
        m = mods_ref[0]
        h = _norm_mod(x_ref[...], ng_ref[...], m[mod_base:mod_base + 1], m[mod_base + 1:mod_base + 2])
        h_ref[...] = h.astype(BF16)
        acc_ref[...] = jnp.zeros_like(acc_ref)

    gu = jnp.dot(h_ref[...], w_ref[...], preferred_element_type=F32)
    a = (_silu(gu[:, :tf]) * gu[:, tf:]).astype(BF16)
    acc_ref[...] += jnp.dot(a, wo_ref[...], preferred_element_type=F32)

    @pl.when(j == pl.num_programs(1) - 1)
    def _():
        gate = mods_ref[0][mod_base + 2:mod_base + 3]
        y = x_ref[...] + (0.5 * gate) * acc_ref[...]
        if final_norm:
            y = y * lax.rsqrt(jnp.mean(y * y, axis=-1, keepdims=True) + EPS) * fg_ref[...]
        o_ref[...] = y


FFN_TM = 512
FFN_TF = 512


def _ffn_weights(w_in, w_out):
    d, two_ff = w_in.shape
    d_ff = two_ff // 2
    tf = min(FFN_TF, d_ff)
    w = w_in.reshape(d, 2, d_ff // tf, tf).transpose(0, 2, 1, 3).reshape(d, two_ff)
    return w.astype(BF16), w_out.astype(BF16)


def _ffn(x, mods3, row0, cond_tokens, norm_g, final_g, w_in_bf, w_out_bf, mod_base, final_norm):
    n_tok, d = x.shape
    d_ff = w_out_bf.shape[0]
    tm = min(FFN_TM, n_tok)
    tf = min(FFN_TF, d_ff)
    assert cond_tokens % tm == 0
    tiles_per_row = cond_tokens // tm
    kern = functools.partial(_ffn_kernel, mod_base=mod_base, final_norm=final_norm)
    return pl.pallas_call(
        kern,
        grid=(n_tok // tm, d_ff // tf),
        in_specs=[pl.BlockSpec((tm, d), lambda i, j: (i, 0)),
                  pl.BlockSpec((1, N_MOD, d), lambda i, j: (row0 + i // tiles_per_row, 0, 0)),
                  pl.BlockSpec((1, d), lambda i, j: (0, 0)),
                  pl.BlockSpec((1, d), lambda i, j: (0, 0)),
                  pl.BlockSpec((d, 2 * tf), lambda i, j: (0, j)),
                  pl.BlockSpec((tf, d), lambda i, j: (j, 0))],
        out_specs=pl.BlockSpec((tm, d), lambda i, j: (i, 0)),
        out_shape=jax.ShapeDtypeStruct((n_tok, d), F32),
        scratch_shapes=[pltpu.VMEM((tm, d), BF16), pltpu.VMEM((tm, d), F32)],
        compiler_params=_cparams(("parallel", "arbitrary")),
        name="ffn",
    )(x, mods3, norm_g.reshape(1, d), final_g.reshape(1, d), w_in_bf, w_out_bf)


def _proj_kernel(x_ref, mods_ref, ng_ref, w_ref, o_ref, h_ref):
    @pl.when(pl.program_id(1) == 0)
    def _():
        m = mods_ref[0]
        h_ref[...] = _norm_mod(x_ref[...], ng_ref[...], m[3:4], m[4:5]).astype(BF16)

    o_ref[...] = jnp.dot(h_ref[...], w_ref[...], preferred_element_type=F32)


def _proj(x, mods3, row0, cond_tokens, norm_g, w_bf):
    n_tok, d = x.shape
    n = w_bf.shape[1]
    tm = min(1024, n_tok)
    tn = math.gcd(1024, d)
    assert cond_tokens % tm == 0
    tiles_per_row = cond_tokens // tm
    return pl.pallas_call(
        _proj_kernel,
        grid=(n_tok // tm, n // tn),
        in_specs=[pl.BlockSpec((tm, d), lambda i, j: (i, 0)),
                  pl.BlockSpec((1, N_MOD, d), lambda i, j: (row0 + i // tiles_per_row, 0, 0)),
                  pl.BlockSpec((1, d), lambda i, j: (0, 0)),
                  pl.BlockSpec((d, tn), lambda i, j: (0, j))],
        out_specs=pl.BlockSpec((tm, tn), lambda i, j: (i, j)),
        out_shape=jax.ShapeDtypeStruct((n_tok, n), F32),
        scratch_shapes=[pltpu.VMEM((tm, d), BF16)],
        compiler_params=_cparams(("parallel", "arbitrary")),
        name="proj",
    )(x, mods3, norm_g.reshape(1, d), w_bf)


def _head_mask(shape, hh):
    lane = lax.broadcasted_iota(jnp.int32, shape, 1)
    return (lane >= hh * NA_HEAD_DIM) & (lane < (hh + 1) * NA_HEAD_DIM)


def _nt_dot(a, b):
    return lax.dot_general(a, b, (((1,), (1,)), ((), ())), preferred_element_type=F32)


def _attn_ctx_kernel(q_ref, k_ref, v_ref, o_ref):
    q = q_ref[...] * (NA_HEAD_DIM ** -0.5 * LOG2E)
    k = k_ref[...].astype(BF16)
    v = v_ref[...].astype(BF16)
    outs = []
    for hh in range(HEADS_PER_SLAB):
        qm = jnp.where(_head_mask(q.shape, hh), q, 0.0).astype(BF16)
        s = _nt_dot(qm, k)
        p = jnp.exp2(s - jnp.max(s, axis=-1, keepdims=True))
        l = jnp.sum(p, axis=-1, keepdims=True)
        outs.append(jnp.dot(p.astype(BF16), v, preferred_element_type=F32) / l)
    o_ref[...] = jnp.where(_head_mask(q.shape, 0), outs[0], outs[1])


def _attn_ctx(proj, batch, seq):
    return pl.pallas_call(
        _attn_ctx_kernel,
        grid=(batch, N_HEAD_SLABS),
        in_specs=[pl.BlockSpec((seq, LANES), lambda b, s: (b, s)),
                  pl.BlockSpec((seq, LANES), lambda b, s: (b, N_HEAD_SLABS + s)),
                  pl.BlockSpec((seq, LANES), lambda b, s: (b, 2 * N_HEAD_SLABS + s))],
        out_specs=pl.BlockSpec((seq, LANES), lambda b, s: (b, s)),
        out_shape=jax.ShapeDtypeStruct((batch * seq, NA_WIDTH), F32),
        compiler_params=_cparams(("parallel", "parallel")),
        name="attn_ctx",
    )(proj, proj, proj)


def _attn_na_kernel(q_ref, k_ref, v_ref, ck_ref, cv_ref, bias_ref, o_ref,
                    kb_ref, vb_ref, ckb_ref, cvb_ref, *, rows):
    rb = pl.program_id(2)

    @pl.when(rb == 0)
    def _():
        kb_ref[...] = k_ref[...].astype(BF16)
        vb_ref[...] = v_ref[...].astype(BF16)
        ckb_ref[...] = ck_ref[...].astype(BF16)
        cvb_ref[...] = cv_ref[...].astype(BF16)

    start = jnp.clip(rb * NA_QROWS - WIN_R // 2, 0, rows - NA_KROWS)
    off = pl.multiple_of(start * GRID_W, NA_QROWS * GRID_W)
    kw = kb_ref[pl.ds(off, NA_KROWS * GRID_W), :]
    vw = vb_ref[pl.ds(off, NA_KROWS * GRID_W), :]
    ck = ckb_ref[...]
    cv = cvb_ref[...]
    q = q_ref[...] * (NA_HEAD_DIM ** -0.5 * LOG2E)
    outs = []
    for hh in range(HEADS_PER_SLAB):
        qm = jnp.where(_head_mask(q.shape, hh), q, 0.0).astype(BF16)
        s_loc = _nt_dot(qm, kw) + bias_ref[0, hh]
        s_ctx = _nt_dot(qm, ck)
        m = jnp.maximum(jnp.max(s_loc, axis=-1, keepdims=True), jnp.max(s_ctx, axis=-1, keepdims=True))
        p_loc = jnp.exp2(s_loc - m)
        p_ctx = jnp.exp2(s_ctx - m)
        l = jnp.sum(p_loc, axis=-1, keepdims=True) + jnp.sum(p_ctx, axis=-1, keepdims=True)
        o = (jnp.dot(p_loc.astype(BF16), vw, preferred_element_type=F32)
             + jnp.dot(p_ctx.astype(BF16), cv, preferred_element_type=F32))
        outs.append(o / l)
    o_ref[...] = jnp.where(_head_mask(q.shape, 0), outs[0], outs[1])


def _na_bias_kernel(rpb_ref, o_ref, *, rows):
    shape = (GRID_W, LANES)
    qc = lax.broadcasted_iota(jnp.int32, shape, 0)
    lane = lax.broadcasted_iota(jnp.int32, shape, 1)
    kc = lane % GRID_W
    cs = jnp.clip(qc - WIN_C // 2, 0, GRID_W - WIN_C)
    valid_c = (kc >= cs) & (kc < cs + WIN_C)
    first_row = lane < GRID_W
    r = rpb_ref[0] * LOG2E
    for p, r0 in enumerate((0, NA_QROWS, rows - NA_QROWS)):
        start = min(max(r0 - WIN_R // 2, 0), rows - NA_KROWS)
        for qr in range(NA_QROWS):
            q_row = r0 + qr
            ws = min(max(q_row - WIN_R // 2, 0), rows - WIN_R)
            for k2 in range(NA_KROWS // 2):
                halves = []
                for kk in range(2):
                    k_row = start + 2 * k2 + kk
                    if ws <= k_row < ws + WIN_R:
                        dr = k_row - q_row + WIN_R - 1
                        row = jnp.broadcast_to(r[dr:dr + 1, :], shape)
                        shift = (LANES - (WIN_C - 1) + GRID_W * kk) % LANES
                        halves.append(pltpu.roll(row, shift, 1, stride=1, stride_axis=0))
                    else:
                        halves.append(jnp.full(shape, NEG_BIAS, F32))
                tile = jnp.where(valid_c, jnp.where(first_row, halves[0], halves[1]), NEG_BIAS)
                o_ref[p, 0, qr * GRID_W:(qr + 1) * GRID_W, k2 * LANES:(k2 + 1) * LANES] = tile


def _na_bias_tables(rpb, rows):
    h, nr, nc = rpb.shape
    rpb_pad = jnp.zeros((h, 16, LANES), F32).at[:, :nr, :nc].set(rpb)
    tq, tk = NA_QROWS * GRID_W, NA_KROWS * GRID_W
    return pl.pallas_call(
        functools.partial(_na_bias_kernel, rows=rows),
        grid=(h,),
        in_specs=[pl.BlockSpec((1, 16, LANES), lambda i: (i, 0, 0))],
        out_specs=pl.BlockSpec((3, 1, tq, tk), lambda i: (0, i, 0, 0)),
        out_shape=jax.ShapeDtypeStruct((3, h, tq, tk), F32),
        compiler_params=_cparams(("parallel",)),
        name="na_bias",
    )(rpb_pad)


def _attn_na(proj, ck_tok, cv_tok, bias, batch, seq, past):
    rows = seq // GRID_W
    assert rows >= NA_KROWS and rows % NA_QROWS == 0
    nrb = rows // NA_QROWS
    tq = NA_QROWS * GRID_W

    def bias_map(b, s, r):
        return ((r > 0).astype(jnp.int32) + (r == nrb - 1).astype(jnp.int32), s, 0, 0)

    return pl.pallas_call(
        functools.partial(_attn_na_kernel, rows=rows),
        grid=(batch, N_HEAD_SLABS, nrb),
        in_specs=[pl.BlockSpec((tq, LANES), lambda b, s, r: (b * nrb + r, s)),
                  pl.BlockSpec((seq, LANES), lambda b, s, r: (b, N_HEAD_SLABS + s)),
                  pl.BlockSpec((seq, LANES), lambda b, s, r: (b, 2 * N_HEAD_SLABS + s)),
                  pl.BlockSpec((past, LANES), lambda b, s, r: (b, s)),
                  pl.BlockSpec((past, LANES), lambda b, s, r: (b, s)),
                  pl.BlockSpec((1, HEADS_PER_SLAB, tq, NA_KROWS * GRID_W), bias_map)],
        out_specs=pl.BlockSpec((tq, LANES), lambda b, s, r: (b * nrb + r, s)),
        out_shape=jax.ShapeDtypeStruct((batch * seq, NA_WIDTH), F32),
        scratch_shapes=[pltpu.VMEM((seq, LANES), BF16), pltpu.VMEM((seq, LANES), BF16),
                        pltpu.VMEM((past, LANES), BF16), pltpu.VMEM((past, LANES), BF16)],
        compiler_params=_cparams(("parallel", "parallel", "arbitrary")),
        name="attn_na",
    )(proj, proj, proj, ck_tok, cv_tok, bias)


def _s5_prep_kernel(lr_ref, li_ref, ld_ref, lrc_ref, lic_ref, ldc_ref, btr_ref, bti_ref, ctr_ref, cti_ref,
                    e_ref, m_ref, f_ref, at_ref, *, seg_tokens):
    T = S5_CHUNK
    ns = SLAB_STATE
    rb = lax.broadcasted_iota(jnp.int32, (LANES, ns), 0) // S5_GROUP_CH
    cb = lax.broadcasted_iota(jnp.int32, (LANES, ns), 1) // S5_STATE
    mask_b = rb == cb
    rc = lax.broadcasted_iota(jnp.int32, (ns, LANES), 0) // S5_STATE
    cc = lax.broadcasted_iota(jnp.int32, (ns, LANES), 1) // S5_GROUP_CH
    mask_c = rc == cc

    def powers(ldr, ldi, n):
        mag = jnp.exp(ldr * float(n))
        return mag * jnp.cos(ldi * float(n)), mag * jnp.sin(ldi * float(n))

    kt = [[None] * T, [None] * T]
    for d in range(2):
        lam_re, lam_im = lr_ref[0, d], li_ref[0, d]
        dt = jnp.exp(ld_ref[0, d])
        ldr, ldi = lam_re * dt, lam_im * dt
        a_re, a_im = powers(ldr, ldi, 1)
        mag2 = lam_re * lam_re + lam_im * lam_im
        f_re = ((a_re - 1.0) * lam_re + a_im * lam_im) / mag2
        f_im = (a_im * lam_re - (a_re - 1.0) * lam_im) / mag2
        btr = jnp.where(mask_b, btr_ref[0, d], 0.0)
        bti = jnp.where(mask_b, bti_ref[0, d], 0.0)
        ctr = jnp.where(mask_c, ctr_ref[0, d], 0.0)
        cti = jnp.where(mask_c, cti_ref[0, d], 0.0)
        c0 = d * 2 * ns
        for n in range(T):
            an_re, an_im = powers(ldr, ldi, n)
            w_re = f_re * an_re - f_im * an_im
            w_im = f_re * an_im + f_im * an_re
            e_re = btr * w_re - bti * w_im
            e_im = btr * w_im + bti * w_re
            t = T - 1 - n if d == 0 else n
            e_ref[0, t * LANES:(t + 1) * LANES, c0:c0 + ns] = e_re.astype(BF16)
            e_ref[0, t * LANES:(t + 1) * LANES, c0 + ns:c0 + 2 * ns] = e_im.astype(BF16)
            kt[d][n] = (jnp.dot(e_re, ctr, preferred_element_type=F32, precision=lax.Precision.HIGHEST)
                        - jnp.dot(e_im, cti, preferred_element_type=F32, precision=lax.Precision.HIGHEST))
        lam_re_c, lam_im_c = lrc_ref[0, d], lic_ref[0, d]
        dt_c = jnp.exp(ldc_ref[0, d])
        ldr_c, ldi_c = lam_re_c * dt_c, lam_im_c * dt_c
        for tp in range(T):
            n = tp + 1 if d == 0 else T - tp
            an_re, an_im = powers(ldr_c, ldi_c, n)
            f_ref[0, c0:c0 + ns, tp * LANES:(tp + 1) * LANES] = (ctr * an_re - cti * an_im).astype(BF16)
            f_ref[0, c0 + ns:c0 + 2 * ns, tp * LANES:(tp + 1) * LANES] = (-(ctr * an_im + cti * an_re)).astype(BF16)
        at_re, at_im = powers(ldr, ldi, T)
        at_ref[0, 2 * d:2 * d + 1, :] = at_re
        at_ref[0, 2 * d + 1:2 * d + 2, :] = at_im
        as_re, as_im = powers(ldr, ldi, seg_tokens)
        at_ref[0, 4 + 2 * d:5 + 2 * d, :] = as_re
        at_ref[0, 5 + 2 * d:6 + 2 * d, :] = as_im
    for t in range(T):
        for tp in range(T):
            if tp > t:
                blk = kt[0][tp - t]
            elif tp < t:
                blk = kt[1][t - tp]
            else:
                blk = kt[0][0] + kt[1][0]
            m_ref[0, t * LANES:(t + 1) * LANES, tp * LANES:(tp + 1) * LANES] = blk.astype(BF16)


def _s5_prep(lam_re, lam_im, log_dt, b_re, b_im, c_re, c_im, seg_tokens):
    T, S, G8, ns = S5_CHUNK, N_GROUP_SLABS, GROUPS_PER_SLAB, SLAB_STATE

    def state_vec(v):
        return v.reshape(2, S, ns).transpose(1, 0, 2)

    lr, li = state_vec(lam_re), state_vec(lam_im)
    ld = state_vec(jnp.broadcast_to(log_dt[:, :, None], lam_re.shape))

    def bt(v):
        v = v.reshape(2, S, G8, S5_STATE, S5_GROUP_CH).transpose(1, 0, 2, 4, 3).reshape(S, 2, LANES, S5_STATE)
        return jnp.tile(v, (1, 1, 1, G8))

    def ct(v):
        v = v.reshape(2, S, G8, S5_GROUP_CH, S5_STATE).transpose(1, 0, 2, 4, 3).reshape(S, 2, ns, S5_GROUP_CH)
        return jnp.tile(v, (1, 1, 1, G8))

    row = lambda v: v[:, :, None, :]
    col = lambda v: v[:, :, :, None]
    spec_row = pl.BlockSpec((1, 2, 1, ns), lambda j: (j, 0, 0, 0))
    spec_col = pl.BlockSpec((1, 2, ns, 1), lambda j: (j, 0, 0, 0))
    spec_bt = pl.BlockSpec((1, 2, LANES, ns), lambda j: (j, 0, 0, 0))
    spec_ct = pl.BlockSpec((1, 2, ns, LANES), lambda j: (j, 0, 0, 0))
    return pl.pallas_call(
        functools.partial(_s5_prep_kernel, seg_tokens=seg_tokens),
        grid=(S,),
        in_specs=[spec_row, spec_row, spec_row, spec_col, spec_col, spec_col, spec_bt, spec_bt, spec_ct, spec_ct],
        out_specs=[pl.BlockSpec((1, T * LANES, SLAB_COLS), lambda j: (j, 0, 0)),
                   pl.BlockSpec((1, T * LANES, T * LANES), lambda j: (j, 0, 0)),
                   pl.BlockSpec((1, SLAB_COLS, T * LANES), lambda j: (j, 0, 0)),
                   pl.BlockSpec((1, 8, ns), lambda j: (j, 0, 0))],
        out_shape=[jax.ShapeDtypeStruct((S, T * LANES, SLAB_COLS), BF16),
                   jax.ShapeDtypeStruct((S, T * LANES, T * LANES), BF16),
                   jax.ShapeDtypeStruct((S, SLAB_COLS, T * LANES), BF16),
                   jax.ShapeDtypeStruct((S, 8, ns), F32)],
        compiler_params=_cparams(("parallel",)),
        name="s5_prep",
    )(row(lr), row(li), row(ld), col(lr), col(li), col(ld), bt(b_re), bt(b_im), ct(c_re), ct(c_im))


def _s5_kernel(u_ref, h0_ref, e_ref, m_ref, f_ref, at_ref, d_ref, y_ref, fin_ref,
               us_ref, ubf_ref, st_ref, ys_ref, *, bt, nseg, seq):
    T = S5_CHUNK
    ns = SLAB_STATE
    npar = bt * nseg
    seg = seq // nseg
    nc = seg // T
    for p in range(npar):
        for t in range(T):
            us_ref[t, pl.ds(p, nc, stride=npar), :] = u_ref[pl.ds(p * seg + t, nc, stride=T), :]
    for t in range(T):
        ubf_ref[:, t * LANES:(t + 1) * LANES] = us_ref[t].astype(BF16)
    ubf = ubf_ref[...]
    st_ref[...] = jnp.dot(ubf, e_ref[0], preferred_element_type=F32)

    at = at_ref[0]
    a_fr, a_fi, a_br, a_bi = at[0:1], at[1:2], at[2:3], at[3:4]

    def scan(init, keep_entering):
        def step(i, carry):
            c_fr, c_fi, c_br, c_bi = carry
            rf = pl.multiple_of(i * npar, npar)
            rbk = pl.multiple_of((nc - 1 - i) * npar, npar)
            l_fr = st_ref[pl.ds(rf, npar), 0:ns]
            l_fi = st_ref[pl.ds(rf, npar), ns:2 * ns]
            l_br = st_ref[pl.ds(rbk, npar), 2 * ns:3 * ns]
            l_bi = st_ref[pl.ds(rbk, npar), 3 * ns:4 * ns]
            if keep_entering:
                st_ref[pl.ds(rf, npar), 0:ns] = c_fr
                st_ref[pl.ds(rf, npar), ns:2 * ns] = c_fi
                st_ref[pl.ds(rbk, npar), 2 * ns:3 * ns] = c_br
                st_ref[pl.ds(rbk, npar), 3 * ns:4 * ns] = c_bi
            return (a_fr * c_fr - a_fi * c_fi + l_fr, a_fr * c_fi + a_fi * c_fr + l_fi,
                    a_br * c_br - a_bi * c_bi + l_br, a_br * c_bi + a_bi * c_br + l_bi)

        return lax.fori_loop(0, nc, step, init)

    h0 = h0_ref[0, 0]
    h0 = [h0[:, k * ns:(k + 1) * ns] for k in range(4)]
    if nseg == 1:
        enter = tuple(h0)
    else:
        zero = jnp.zeros((npar, ns), F32)
        z_fr, z_fi, z_br, z_bi = scan((zero, zero, zero, zero), False)
        s_fr, s_fi, s_br, s_bi = at[4:5], at[5:6], at[6:7], at[7:8]
        rows_f, rows_b = [None] * npar, [None] * npar
        for b in range(bt):
            c_re, c_im = h0[0][b:b + 1], h0[1][b:b + 1]
            for s in range(nseg):
                p = b * nseg + s
                rows_f[p] = (c_re, c_im)
                c_re, c_im = (s_fr * c_re - s_fi * c_im + z_fr[p:p + 1], s_fr * c_im + s_fi * c_re + z_fi[p:p + 1])
            c_re, c_im = h0[2][b:b + 1], h0[3][b:b + 1]
            for s in reversed(range(nseg)):
                p = b * nseg + s
                rows_b[p] = (c_re, c_im)
                c_re, c_im = (s_br * c_re - s_bi * c_im + z_br[p:p + 1], s_br * c_im + s_bi * c_re + z_bi[p:p + 1])
        enter = (jnp.concatenate([r[0] for r in rows_f], axis=0), jnp.concatenate([r[1] for r in rows_f], axis=0),
                 jnp.concatenate([r[0] for r in rows_b], axis=0), jnp.concatenate([r[1] for r in rows_b], axis=0))
    fin = scan(enter, True)
    for b in range(bt):
        last_f, last_b = b * nseg + nseg - 1, b * nseg
        fin_ref[0, 0, b:b + 1, 0:ns] = fin[0][last_f:last_f + 1]
        fin_ref[0, 0, b:b + 1, ns:2 * ns] = fin[1][last_f:last_f + 1]
        fin_ref[0, 0, b:b + 1, 2 * ns:3 * ns] = fin[2][last_b:last_b + 1]
        fin_ref[0, 0, b:b + 1, 3 * ns:4 * ns] = fin[3][last_b:last_b + 1]

    y = (jnp.dot(ubf, m_ref[0], preferred_element_type=F32)
         + jnp.dot(st_ref[...].astype(BF16), f_ref[0], preferred_element_type=F32))
    for t in range(T):
        ys_ref[t] = y[:, t * LANES:(t + 1) * LANES] + d_ref[0] * us_ref[t]
    for p in range(npar):
        for t in range(T):
            y_ref[pl.ds(p * seg + t, nc, stride=T), :] = ys_ref[t, pl.ds(p, nc, stride=npar), :]


def _s5(proj, u_col, h0, e, m, f, at, d_skip, batch, seq, bt, nseg):
    T, S = S5_CHUNK, N_GROUP_SLABS
    rows = bt * seq // T
    const = lambda j, g: (j, 0, 0)
    y, fin = pl.pallas_call(
        functools.partial(_s5_kernel, bt=bt, nseg=nseg, seq=seq),
        grid=(S, batch // bt),
        in_specs=[pl.BlockSpec((bt * seq, LANES), lambda j, g: (g, u_col + j)),
                  pl.BlockSpec((1, 1, bt, SLAB_COLS), lambda j, g: (j, g, 0, 0)),
                  pl.BlockSpec((1, T * LANES, SLAB_COLS), const),
                  pl.BlockSpec((1, T * LANES, T * LANES), const),
                  pl.BlockSpec((1, SLAB_COLS, T * LANES), const),
                  pl.BlockSpec((1, 8, SLAB_STATE), const),
                  pl.BlockSpec((1, 1, LANES), const)],
        out_specs=[pl.BlockSpec((bt * seq, LANES), lambda j, g: (g, j)),
                   pl.BlockSpec((1, 1, bt, SLAB_COLS), lambda j, g: (j, g, 0, 0))],
        out_shape=[jax.ShapeDtypeStruct((batch * seq, S5_WIDTH), F32),
                   jax.ShapeDtypeStruct((S, batch // bt, bt, SLAB_COLS), F32)],
        scratch_shapes=[pltpu.VMEM((T, rows, LANES), F32), pltpu.VMEM((rows, T * LANES), BF16),
                        pltpu.VMEM((rows, SLAB_COLS), F32), pltpu.VMEM((T, rows, LANES), F32)],
        compiler_params=_cparams(("parallel", "arbitrary")),
        name="s5",
    )(proj, h0.reshape(S, batch // bt, bt, SLAB_COLS), e, m, f, at, d_skip.reshape(S, 1, LANES))
    return y, fin.reshape(S, batch, SLAB_COLS)


def _state_to_slabs(state):
    b = state.shape[0]
    s = state.reshape(b, 2, 2, N_GROUP_SLABS, SLAB_STATE).transpose(3, 0, 1, 2, 4)
    return s.reshape(N_GROUP_SLABS, b, SLAB_COLS)


def _slabs_to_state(slabs):
    b = slabs.shape[1]
    s = slabs.reshape(N_GROUP_SLABS, b, 2, 2, SLAB_STATE).transpose(1, 2, 3, 0, 4)
    return s.reshape(b, 2, 2, S5_GROUPS, S5_STATE)


def _merge_kernel(x_ref, mods_ref, ya_ref, yb_ref, ga_ref, gb_ref, wglu_ref, wua_ref, wub_ref, wo_ref, o_ref):
    yb = jax.nn.gelu(yb_ref[...])
    glu = jnp.dot(yb.astype(BF16), wglu_ref[...], preferred_element_type=F32)
    yb = yb * jax.nn.sigmoid(glu)
    a = jnp.dot(ya_ref[...].astype(BF16), wua_ref[...], preferred_element_type=F32)
    b = jnp.dot(yb.astype(BF16), wub_ref[...], preferred_element_type=F32)
    merged = jax.nn.sigmoid(ga_ref[...]) * a + jax.nn.sigmoid(gb_ref[...]) * b
    o = jnp.dot(merged.astype(BF16), wo_ref[...], preferred_element_type=F32)
    o_ref[...] = x_ref[...] + mods_ref[0][5:6] * o


def _merge(x, mods3, row0, cond_tokens, ya, yb, proj, w_glu, w_up_a, w_up_b, w_out):
    n_tok, d = x.shape
    tm = min(256, n_tok)
    assert cond_tokens % tm == 0
    tiles_per_row = cond_tokens // tm
    gate_col = (3 * NA_WIDTH + S5_WIDTH) // d
    assert gate_col * d == 3 * NA_WIDTH + S5_WIDTH
    resident = functools.partial(pl.BlockSpec, pipeline_mode=pl.Buffered(1))
    return pl.pallas_call(
        _merge_kernel,
        grid=(n_tok // tm,),
        in_specs=[pl.BlockSpec((tm, d), lambda i: (i, 0)),
                  pl.BlockSpec((1, N_MOD, d), lambda i: (row0 + i // tiles_per_row, 0, 0)),
                  pl.BlockSpec((tm, NA_WIDTH), lambda i: (i, 0)),
                  pl.BlockSpec((tm, S5_WIDTH), lambda i: (i, 0)),
                  pl.BlockSpec((tm, d), lambda i: (i, gate_col)),
                  pl.BlockSpec((tm, d), lambda i: (i, gate_col + 1)),
                  resident((S5_WIDTH, S5_WIDTH), lambda i: (0, 0)),
                  resident((NA_WIDTH, d), lambda i: (0, 0)),
                  resident((S5_WIDTH, d), lambda i: (0, 0)),
                  resident((d, d), lambda i: (0, 0))],
        out_specs=pl.BlockSpec((tm, d), lambda i: (i, 0)),
        out_shape=jax.ShapeDtypeStruct((n_tok, d), F32),
        compiler_params=_cparams(("parallel",)),
        name="merge",
    )(x, mods3, ya, yb, proj, proj, w_glu, w_up_a, w_up_b, w_out)


def _heads_first(t, batch, seq):
    return t.reshape(batch, seq, NA_HEADS, NA_HEAD_DIM).transpose(0, 2, 1, 3)


def _tokens_first(t):
    b, h, l, dh = t.shape
    return t.transpose(0, 2, 1, 3).reshape(b * l, h * dh)


def kernel(x_prompt, x_sample, cache_k, cache_v, state_ssm, c, c_ctx, w_ada, b_ada, norm_g, ffn_in, ffn_out, w_in, rpb, s5_lam_re, s5_lam_im, s5_log_dt, s5_b_re, s5_b_im, s5_c_re, s5_c_im, s5_d, w_glu, w_up_a, w_up_b, w_out, final_g):
    depth = w_ada.shape[0]
    pb, pl_, d = x_prompt.shape
    sb, sl, _ = x_sample.shape
    past = cache_k.shape[3]
    u_col = 3 * NA_WIDTH // LANES
    mod_rows = 8 * ((1 + sb + 7) // 8)

    xp = x_prompt.reshape(pb * pl_, d)
    xs = x_sample.reshape(sb * sl, d)
    cond = jnp.concatenate([c_ctx[None, :], c, jnp.zeros((mod_rows - 1 - sb, d), F32)], axis=0)
    new_k, new_v, new_s = [], [], []
    for l in range(depth):
        last = l == depth - 1
        mods3 = _mods(cond, w_ada[l], b_ada[l]).reshape(mod_rows, N_MOD, d)
        ffn1_in, ffn1_out = _ffn_weights(ffn_in[l, 0], ffn_out[l, 0])
        ffn2_in, ffn2_out = _ffn_weights(ffn_in[l, 1], ffn_out[l, 1])
        w_in_bf = w_in[l].astype(BF16)
        w_glu_bf, w_up_a_bf, w_up_b_bf, w_out_bf = (w[l].astype(BF16) for w in (w_glu, w_up_a, w_up_b, w_out))
        e, m, f, at = _s5_prep(s5_lam_re[l], s5_lam_im[l], s5_log_dt[l], s5_b_re[l], s5_b_im[l],
                               s5_c_re[l], s5_c_im[l], sl // S5_SEGMENTS)
        bias = _na_bias_tables(rpb[l], sl // GRID_W)

        ptok = pb * pl_
        xp = _ffn(xp, mods3, 0, ptok, norm_g[l, 0], final_g, ffn1_in, ffn1_out, 0, False)
        proj_p = _proj(xp, mods3, 0, ptok, norm_g[l, 1], w_in_bf)
        ya_p = _attn_ctx(proj_p, pb, pl_)
        yb_p, fin_p = _s5(proj_p, u_col, jnp.zeros((N_GROUP_SLABS, pb, SLAB_COLS), F32), e, m, f, at,
                          s5_d[l], pb, pl_, math.gcd(pb, 16), 1)
        xp = _merge(xp, mods3, 0, ptok, ya_p, yb_p, proj_p, w_glu_bf, w_up_a_bf, w_up_b_bf, w_out_bf)
        xp = _ffn(xp, mods3, 0, ptok, norm_g[l, 2], final_g, ffn2_in, ffn2_out, 6, last)
        new_k.append(_heads_first(proj_p[:, NA_WIDTH:2 * NA_WIDTH], pb, pl_))
        new_v.append(_heads_first(proj_p[:, 2 * NA_WIDTH:3 * NA_WIDTH], pb, pl_))
        new_s.append(_slabs_to_state(fin_p))

        xs = _ffn(xs, mods3, 1, sl, norm_g[l, 0], final_g, ffn1_in, ffn1_out, 0, False)
        proj_s = _proj(xs, mods3, 1, sl, norm_g[l, 1], w_in_bf)
        ya_s = _attn_na(proj_s, _tokens_first(cache_k[:, l]), _tokens_first(cache_v[:, l]), bias, sb, sl, past)
        yb_s, _ = _s5(proj_s, u_col, _state_to_slabs(state_ssm[:, l]), e, m, f, at, s5_d[l], sb, sl, 1, S5_SEGMENTS)
        xs = _merge(xs, mods3, 1, sl, ya_s, yb_s, proj_s, w_glu_bf, w_up_a_bf, w_up_b_bf, w_out_bf)
        xs = _ffn(xs, mods3, 1, sl, norm_g[l, 2], final_g, ffn2_in, ffn2_out, 6, last)

    y_prompt = xp.reshape(pb, pl_, d)
    y_sample = xs.reshape(sb, sl, d)
    return (y_prompt, y_sample, jnp.stack(new_k, axis=1), jnp.stack(new_v, axis=1), jnp.stack(new_s, axis=1))
```

```python
import functools
import math

import numpy as np
import jax
import jax.numpy as jnp
from jax import lax
from jax.experimental import pallas as pl
from jax.experimental.pallas import tpu as pltpu

F32 = jnp.float32
BF16 = jnp.bfloat16

GRID_W = 64
NA_HEADS = 16
NA_HEAD_DIM = 64
NA_WIDTH = NA_HEADS * NA_HEAD_DIM
WIN_R = 8
WIN_C = 16
S5_GROUP_CH = 16
S5_WIDTH = 1024
S5_GROUPS = S5_WIDTH // S5_GROUP_CH
S5_STATE = 64
N_MOD = 9
EPS = 1e-6

LANES = 128
HEADS_PER_SLAB = LANES // NA_HEAD_DIM
N_HEAD_SLABS = NA_WIDTH // LANES
GROUPS_PER_SLAB = LANES // S5_GROUP_CH
N_GROUP_SLABS = S5_WIDTH // LANES
SLAB_STATE = GROUPS_PER_SLAB * S5_STATE
SLAB_COLS = 4 * SLAB_STATE
S5_CHUNK = 8
S5_SEGMENTS = 8
NA_QROWS = 4
NA_KROWS = NA_QROWS + WIN_R
NEG_BIAS = -1e30
LOG2E = 1.4426950408889634
VMEM_LIMIT = 56 * 1024 * 1024


def _cparams(sem, limit=VMEM_LIMIT):
    return pltpu.CompilerParams(dimension_semantics=sem, vmem_limit_bytes=limit)


def _silu(x):
    return x * jax.nn.sigmoid(x)


NORM_ROWS = 16
NORM_UNROLL = 8


def _norm_mod_store(x_ref, h_ref, g, shift, scale):
    gs = g * (1.0 + scale)

    def body(i, carry):
        r0 = pl.multiple_of(i * NORM_ROWS, NORM_ROWS)
        x = x_ref[pl.ds(r0, NORM_ROWS), :]
        r = lax.rsqrt(jnp.mean(x * x, axis=-1, keepdims=True) + EPS)
        h_ref[pl.ds(r0, NORM_ROWS), :] = ((x * r) * gs + shift).astype(BF16)
        return carry

    lax.fori_loop(0, x_ref.shape[0] // NORM_ROWS, body, 0, unroll=NORM_UNROLL)


def _mods_kernel(c_ref, w_ref, b_ref, o_ref):
    a = _silu(c_ref[...])
    o_ref[...] = jnp.dot(a, w_ref[...], preferred_element_type=F32,
                         precision=lax.Precision.HIGHEST) + b_ref[...]


def _mods(cond, w_ada, b_ada):
    rows, d = cond.shape
    n = w_ada.shape[1]
    tn = math.gcd(1024, n)
    return pl.pallas_call(
        _mods_kernel,
        grid=(n // tn,),
        in_specs=[pl.BlockSpec((rows, d), lambda j: (0, 0)),
                  pl.BlockSpec((d, tn), lambda j: (0, j)),
                  pl.BlockSpec((1, tn), lambda j: (0, j))],
        out_specs=pl.BlockSpec((rows, tn), lambda j: (0, j)),
        out_shape=jax.ShapeDtypeStruct((rows, n), F32),
        compiler_params=_cparams(("arbitrary",)),
        name="mods",
    )(cond, w_ada, b_ada.reshape(1, n))


def _ffn_kernel(x_ref, mods_ref, ng_ref, fg_ref, wg_ref, wu_ref, wo_ref, o_ref, h_ref, acc_ref,
                *, mod_base, final_norm):
    j = pl.program_id(1)

    @pl.when(j == 0)
    def _():
        m = mods_ref[0]
        _norm_mod_store(x_ref, h_ref, ng_ref[...], m[mod_base:mod_base + 1], m[mod_base + 1:mod_base + 2])
        acc_ref[...] = jnp.zeros_like(acc_ref)

    h = h_ref[...]
    g = jnp.dot(h, wg_ref[...], preferred_element_type=F32)
    u = jnp.dot(h, wu_ref[...], preferred_element_type=F32)
    a = (_silu(g) * u).astype(BF16)
    acc_ref[...] += jnp.dot(a, wo_ref[...], preferred_element_type=F32)

    @pl.when(j == pl.num_programs(1) - 1)
    def _():
        half_gate = 0.5 * mods_ref[0][mod_base + 2:mod_base + 3]
        if not final_norm:
            o_ref[...] = x_ref[...] + half_gate * acc_ref[...]
            return
        fg = fg_ref[...]

        def body(i, carry):
            r0 = pl.multiple_of(i * NORM_ROWS, NORM_ROWS)
            y = x_ref[pl.ds(r0, NORM_ROWS), :] + half_gate * acc_ref[pl.ds(r0, NORM_ROWS), :]
            o_ref[pl.ds(r0, NORM_ROWS), :] = (y * lax.rsqrt(jnp.mean(y * y, axis=-1, keepdims=True) + EPS)) * fg
            return carry

        lax.fori_loop(0, x_ref.shape[0] // NORM_ROWS, body, 0, unroll=NORM_UNROLL)


FFN_TM = 512
FFN_TF = 512


def _ffn(x, mods3, row0, cond_tokens, norm_g, final_g, w_in_bf, w_out_bf, mod_base, final_norm):
    n_tok, d = x.shape
    d_ff = w_out_bf.shape[0]
    tm = min(FFN_TM, n_tok)
    tf = min(FFN_TF, d_ff)
    assert cond_tokens % tm == 0
    tiles_per_row = cond_tokens // tm
    kern = functools.partial(_ffn_kernel, mod_base=mod_base, final_norm=final_norm)
    return pl.pallas_call(
        kern,
        grid=(n_tok // tm, d_ff // tf),
        in_specs=[pl.BlockSpec((tm, d), lambda i, j: (i, 0)),
                  pl.BlockSpec((1, N_MOD, d), lambda i, j: (row0 + i // tiles_per_row, 0, 0)),
                  pl.BlockSpec((1, d), lambda i, j: (0, 0)),
                  pl.BlockSpec((1, d), lambda i, j: (0, 0)),
                  pl.BlockSpec((d, tf), lambda i, j: (0, j)),
                  pl.BlockSpec((d, tf), lambda i, j: (0, d_ff // tf + j)),
                  pl.BlockSpec((tf, d), lambda i, j: (j, 0))],
        out_specs=pl.BlockSpec((tm, d), lambda i, j: (i, 0)),
        out_shape=jax.ShapeDtypeStruct((n_tok, d), F32),
        scratch_shapes=[pltpu.VMEM((tm, d), BF16), pltpu.VMEM((tm, d), F32)],
        compiler_params=_cparams(("parallel", "arbitrary")),
        name="ffn",
    )(x, mods3, norm_g.reshape(1, d), final_g.reshape(1, d), w_in_bf, w_in_bf, w_out_bf)


def _proj_kernel(x_ref, mods_ref, ng_ref, w_ref, o_ref, h_ref):
    @pl.when(pl.program_id(1) == 0)
    def _():
        m = mods_ref[0]
        _norm_mod_store(x_ref, h_ref, ng_ref[...], m[3:4], m[4:5])

    o_ref[...] = jnp.dot(h_ref[...], w_ref[...], preferred_element_type=F32)


def _proj(x, mods3, row0, cond_tokens, norm_g, w_bf):
    n_tok, d = x.shape
    n = w_bf.shape[1]
    tm = min(1024, n_tok)
    tn = math.gcd(1024, d)
    assert cond_tokens % tm == 0
    tiles_per_row = cond_tokens // tm
    return pl.pallas_call(
        _proj_kernel,
        grid=(n_tok // tm, n // tn),
        in_specs=[pl.BlockSpec((tm, d), lambda i, j: (i, 0)),
                  pl.BlockSpec((1, N_MOD, d), lambda i, j: (row0 + i // tiles_per_row, 0, 0)),
                  pl.BlockSpec((1, d), lambda i, j: (0, 0)),
                  pl.BlockSpec((d, tn), lambda i, j: (0, j))],
        out_specs=pl.BlockSpec((tm, tn), lambda i, j: (i, j)),
        out_shape=jax.ShapeDtypeStruct((n_tok, n), F32),
        scratch_shapes=[pltpu.VMEM((tm, d), BF16)],
        compiler_params=_cparams(("parallel", "arbitrary")),
        name="proj",
    )(x, mods3, norm_g.reshape(1, d), w_bf)


def _head_mask(shape, hh):
    lane = lax.broadcasted_iota(jnp.int32, shape, 1)
    return (lane >= hh * NA_HEAD_DIM) & (lane < (hh + 1) * NA_HEAD_DIM)


def _nt_dot(a, b):
    return lax.dot_general(a, b, (((1,), (1,)), ((), ())), preferred_element_type=F32)


Q_SCALE = NA_HEAD_DIM ** -0.5 * LOG2E
CTX_SLABS = 4
NA_SLABS = 2


def _attn_ctx_kernel(q_ref, k_ref, v_ref, o_ref):
    for sl in range(q_ref.shape[1] // LANES):
        cols = slice(sl * LANES, (sl + 1) * LANES)
        q = q_ref[:, cols] * Q_SCALE
        k = k_ref[:, cols].astype(BF16)
        v = v_ref[:, cols].astype(BF16)
        outs = []
        for hh in range(HEADS_PER_SLAB):
            qm = jnp.where(_head_mask(q.shape, hh), q, 0.0).astype(BF16)
            s = _nt_dot(qm, k)
            p = jnp.exp2(s - jnp.max(s, axis=-1, keepdims=True))
            l = jnp.sum(p, axis=-1, keepdims=True)
            outs.append(jnp.dot(p.astype(BF16), v, preferred_element_type=F32) / l)
        o_ref[:, cols] = jnp.where(_head_mask(q.shape, 0), outs[0], outs[1])


def _attn_ctx(proj, batch, seq):
    w = CTX_SLABS * LANES
    nsb = NA_WIDTH // w
    return pl.pallas_call(
        _attn_ctx_kernel,
        grid=(batch, nsb),
        in_specs=[pl.BlockSpec((seq, w), lambda b, s: (b, s)),
                  pl.BlockSpec((seq, w), lambda b, s: (b, nsb + s)),
                  pl.BlockSpec((seq, w), lambda b, s: (b, 2 * nsb + s))],
        out_specs=pl.BlockSpec((seq, w), lambda b, s: (b, s)),
        out_shape=jax.ShapeDtypeStruct((batch * seq, NA_WIDTH), F32),
        compiler_params=_cparams(("parallel", "parallel")),
        name="attn_ctx",
    )(proj, proj, proj)


def _attn_na_kernel(q_ref, k_ref, v_ref, ck_ref, cv_ref, bias_ref, o_ref,
                    kb_ref, vb_ref, ckb_ref, cvb_ref, *, rows):
    rb = pl.program_id(2)

    @pl.when(rb == 0)
    def _():
        kb_ref[...] = k_ref[...].astype(BF16)
        vb_ref[...] = v_ref[...].astype(BF16)
        ckb_ref[...] = ck_ref[...].astype(BF16)
        cvb_ref[...] = cv_ref[...].astype(BF16)

    start = jnp.clip(rb * NA_QROWS - WIN_R // 2, 0, rows - NA_KROWS)
    off = pl.multiple_of(start * GRID_W, NA_QROWS * GRID_W)
    for sl in range(q_ref.shape[1] // LANES):
        cols = slice(sl * LANES, (sl + 1) * LANES)
        kw = kb_ref[pl.ds(off, NA_KROWS * GRID_W), cols]
        vw = vb_ref[pl.ds(off, NA_KROWS * GRID_W), cols]
        ck = ckb_ref[:, cols]
        cv = cvb_ref[:, cols]
        q = q_ref[:, cols] * Q_SCALE
        outs = []
        for hh in range(HEADS_PER_SLAB):
            qm = jnp.where(_head_mask(q.shape, hh), q, 0.0).astype(BF16)
            s_loc = _nt_dot(qm, kw) + bias_ref[0, sl * HEADS_PER_SLAB + hh]
            s_ctx = _nt_dot(qm, ck)
            m = jnp.maximum(jnp.max(s_loc, axis=-1, keepdims=True), jnp.max(s_ctx, axis=-1, keepdims=True))
            p_loc = jnp.exp2(s_loc - m)
            p_ctx = jnp.exp2(s_ctx - m)
            l = jnp.sum(p_loc, axis=-1, keepdims=True) + jnp.sum(p_ctx, axis=-1, keepdims=True)
            o = (jnp.dot(p_loc.astype(BF16), vw, preferred_element_type=F32)
                 + jnp.dot(p_ctx.astype(BF16), cv, preferred_element_type=F32))
            outs.append(o / l)
        o_ref[:, cols] = jnp.where(_head_mask(q.shape, 0), outs[0], outs[1])


def _na_bias_kernel(rpb_ref, o_ref, *, rows):
    shape = (GRID_W, LANES)
    qc = lax.broadcasted_iota(jnp.int32, shape, 0)
    lane = lax.broadcasted_iota(jnp.int32, shape, 1)
    kc = lane % GRID_W
    cs = jnp.clip(qc - WIN_C // 2, 0, GRID_W - WIN_C)
    valid_c = (kc >= cs) & (kc < cs + WIN_C)
    first_row = lane < GRID_W
    r = rpb_ref[0] * LOG2E
    for p, r0 in enumerate((0, NA_QROWS, rows - NA_QROWS)):
        start = min(max(r0 - WIN_R // 2, 0), rows - NA_KROWS)
        for qr in range(NA_QROWS):
            q_row = r0 + qr
            ws = min(max(q_row - WIN_R // 2, 0), rows - WIN_R)
            for k2 in range(NA_KROWS // 2):
                halves = []
                for kk in range(2):
                    k_row = start + 2 * k2 + kk
                    if ws <= k_row < ws + WIN_R:
                        dr = k_row - q_row + WIN_R - 1
                        row = jnp.broadcast_to(r[dr:dr + 1, :], shape)
                        shift = (LANES - (WIN_C - 1) + GRID_W * kk) % LANES
                        halves.append(pltpu.roll(row, shift, 1, stride=1, stride_axis=0))
                    else:
                        halves.append(jnp.full(shape, NEG_BIAS, F32))
                tile = jnp.where(valid_c, jnp.where(first_row, halves[0], halves[1]), NEG_BIAS)
                o_ref[p, 0, qr * GRID_W:(qr + 1) * GRID_W, k2 * LANES:(k2 + 1) * LANES] = tile


def _na_bias_tables(rpb, rows):
    h, nr, nc = rpb.shape
    rpb_pad = jnp.zeros((h, 16, LANES), F32).at[:, :nr, :nc].set(rpb)
    tq, tk = NA_QROWS * GRID_W, NA_KROWS * GRID_W
    return pl.pallas_call(
        functools.partial(_na_bias_kernel, rows=rows),
        grid=(h,),
        in_specs=[pl.BlockSpec((1, 16, LANES), lambda i: (i, 0, 0))],
        out_specs=pl.BlockSpec((3, 1, tq, tk), lambda i: (0, i, 0, 0)),
        out_shape=jax.ShapeDtypeStruct((3, h, tq, tk), F32),
        compiler_params=_cparams(("parallel",)),
        name="na_bias",
    )(rpb_pad)


def _attn_na(proj, ck_tok, cv_tok, bias, batch, seq, past):
    rows = seq // GRID_W
    assert rows >= NA_KROWS and rows % NA_QROWS == 0
    nrb = rows // NA_QROWS
    tq = NA_QROWS * GRID_W

    def bias_map(b, s, r):
        return ((r > 0).astype(jnp.int32) + (r == nrb - 1).astype(jnp.int32), s, 0, 0)

    w = NA_SLABS * LANES
    nsb = NA_WIDTH // w
    return pl.pallas_call(
        functools.partial(_attn_na_kernel, rows=rows),
        grid=(batch, nsb, nrb),
        in_specs=[pl.BlockSpec((tq, w), lambda b, s, r: (b * nrb + r, s)),
                  pl.BlockSpec((seq, w), lambda b, s, r: (b, nsb + s)),
                  pl.BlockSpec((seq, w), lambda b, s, r: (b, 2 * nsb + s)),
                  pl.BlockSpec((past, w), lambda b, s, r: (b, s)),
                  pl.BlockSpec((past, w), lambda b, s, r: (b, s)),
                  pl.BlockSpec((1, NA_SLABS * HEADS_PER_SLAB, tq, NA_KROWS * GRID_W), bias_map)],
        out_specs=pl.BlockSpec((tq, w), lambda b, s, r: (b * nrb + r, s)),
        out_shape=jax.ShapeDtypeStruct((batch * seq, NA_WIDTH), F32),
        scratch_shapes=[pltpu.VMEM((seq, w), BF16), pltpu.VMEM((seq, w), BF16),
                        pltpu.VMEM((past, w), BF16), pltpu.VMEM((past, w), BF16)],
        compiler_params=_cparams(("parallel", "parallel", "arbitrary")),
        name="attn_na",
    )(proj, proj, proj, ck_tok, cv_tok, bias)


def _s5_prep_kernel(lr_ref, li_ref, ld_ref, btr_ref, bti_ref, cr_ref, ci_ref,
                    e_ref, m_ref, f_ref, at_ref, *, seg_tokens):
    T = S5_CHUNK
    ns = SLAB_STATE
    rb = lax.broadcasted_iota(jnp.int32, (LANES, ns), 0) // S5_GROUP_CH
    cb = lax.broadcasted_iota(jnp.int32, (LANES, ns), 1) // S5_STATE
    mask_b = rb == cb

    def powers(ldr, ldi, n):
        mag = jnp.exp(ldr * float(n))
        return mag * jnp.cos(ldi * float(n)), mag * jnp.sin(ldi * float(n))

    kt = [[None] * T, [None] * T]
    for d in range(2):
        lam_re, lam_im = lr_ref[0, d], li_ref[0, d]
        dt = jnp.exp(ld_ref[0, d])
        ldr, ldi = lam_re * dt, lam_im * dt
        a_re, a_im = powers(ldr, ldi, 1)
        mag2 = lam_re * lam_re + lam_im * lam_im
        f_re = ((a_re - 1.0) * lam_re + a_im * lam_im) / mag2
        f_im = (a_im * lam_re - (a_re - 1.0) * lam_im) / mag2
        btr = jnp.where(mask_b, btr_ref[0, d], 0.0)
        bti = jnp.where(mask_b, bti_ref[0, d], 0.0)
        cr = jnp.where(mask_b, cr_ref[0, d], 0.0)
        ci = jnp.where(mask_b, ci_ref[0, d], 0.0)
        ctr, cti = cr.T, ci.T
        c0 = d * 2 * ns
        e_res, e_ims = [], []
        for n in range(T):
            an_re, an_im = powers(ldr, ldi, n)
            w_re = f_re * an_re - f_im * an_im
            w_im = f_re * an_im + f_im * an_re
            e_re = btr * w_re - bti * w_im
            e_im = btr * w_im + bti * w_re
            t = T - 1 - n if d == 0 else n
            e_ref[0, t * LANES:(t + 1) * LANES, c0:c0 + ns] = e_re.astype(BF16)
            e_ref[0, t * LANES:(t + 1) * LANES, c0 + ns:c0 + 2 * ns] = e_im.astype(BF16)
            e_res.append(e_re)
            e_ims.append(e_im)
        k_all = (jnp.dot(jnp.concatenate(e_res, axis=0), ctr, preferred_element_type=F32,
                         precision=lax.Precision.HIGHEST)
                 - jnp.dot(jnp.concatenate(e_ims, axis=0), cti, preferred_element_type=F32,
                           precision=lax.Precision.HIGHEST))
        for n in range(T):
            kt[d][n] = k_all[n * LANES:(n + 1) * LANES]
        for tp in range(T):
            n = tp + 1 if d == 0 else T - tp
            an_re, an_im = powers(ldr, ldi, n)
            f_ref[0, c0:c0 + ns, tp * LANES:(tp + 1) * LANES] = (cr * an_re - ci * an_im).T.astype(BF16)
            f_ref[0, c0 + ns:c0 + 2 * ns, tp * LANES:(tp + 1) * LANES] = (-(cr * an_im + ci * an_re)).T.astype(BF16)
        at_re, at_im = powers(ldr, ldi, T)
        at_ref[0, 2 * d:2 * d + 1, :] = at_re
        at_ref[0, 2 * d + 1:2 * d + 2, :] = at_im
        as_re, as_im = powers(ldr, ldi, seg_tokens)
        at_ref[0, 4 + 2 * d:5 + 2 * d, :] = as_re
        at_ref[0, 5 + 2 * d:6 + 2 * d, :] = as_im
    for t in range(T):
        for tp in range(T):
            if tp > t:
                blk = kt[0][tp - t]
            elif tp < t:
                blk = kt[1][t - tp]
            else:
                blk = kt[0][0] + kt[1][0]
            m_ref[0, t * LANES:(t + 1) * LANES, tp * LANES:(tp + 1) * LANES] = blk.astype(BF16)


def _s5_prep(lam_re, lam_im, log_dt, b_re, b_im, c_re, c_im, seg_tokens):
    T, S, G8, ns = S5_CHUNK, N_GROUP_SLABS, GROUPS_PER_SLAB, SLAB_STATE

    def state_vec(v):
        return v.reshape(2, S, ns).transpose(1, 0, 2)

    lr, li = state_vec(lam_re), state_vec(lam_im)
    ld = state_vec(jnp.broadcast_to(log_dt[:, :, None], lam_re.shape))

    def bt(v):
        v = v.reshape(2, S, G8, S5_STATE, S5_GROUP_CH).transpose(1, 0, 2, 4, 3).reshape(S, 2, LANES, S5_STATE)
        return jnp.tile(v, (1, 1, 1, G8))

    def crow(v):
        v = v.reshape(2, S, G8, S5_GROUP_CH, S5_STATE).transpose(1, 0, 2, 3, 4).reshape(S, 2, LANES, S5_STATE)
        return jnp.tile(v, (1, 1, 1, G8))

    row = lambda v: v[:, :, None, :]
    spec_row = pl.BlockSpec((1, 2, 1, ns), lambda j: (j, 0, 0, 0))
    spec_bt = pl.BlockSpec((1, 2, LANES, ns), lambda j: (j, 0, 0, 0))
    return pl.pallas_call(
        functools.partial(_s5_prep_kernel, seg_tokens=seg_tokens),
        grid=(S,),
        in_specs=[spec_row, spec_row, spec_row, spec_bt, spec_bt, spec_bt, spec_bt],
        out_specs=[pl.BlockSpec((1, T * LANES, SLAB_COLS), lambda j: (j, 0, 0)),
                   pl.BlockSpec((1, T * LANES, T * LANES), lambda j: (j, 0, 0)),
                   pl.BlockSpec((1, SLAB_COLS, T * LANES), lambda j: (j, 0, 0)),
                   pl.BlockSpec((1, 8, ns), lambda j: (j, 0, 0))],
        out_shape=[jax.ShapeDtypeStruct((S, T * LANES, SLAB_COLS), BF16),
                   jax.ShapeDtypeStruct((S, T * LANES, T * LANES), BF16),
                   jax.ShapeDtypeStruct((S, SLAB_COLS, T * LANES), BF16),
                   jax.ShapeDtypeStruct((S, 8, ns), F32)],
        compiler_params=_cparams(("parallel",)),
        name="s5_prep",
    )(row(lr), row(li), row(ld), bt(b_re), bt(b_im), crow(c_re), crow(c_im))


def _s5_kernel(u_ref, h0_ref, e_ref, m_ref, f_ref, at_ref, d_ref, y_ref, fin_ref,
               us_ref, ubf_ref, st_ref, ys_ref, *, bt, nseg, seq):
    T = S5_CHUNK
    ns = SLAB_STATE
    npar = bt * nseg
    seg = seq // nseg
    nc = seg // T
    for p in range(npar):
        for t in range(T):
            us_ref[t, pl.ds(p, nc, stride=npar), :] = u_ref[pl.ds(p * seg + t, nc, stride=T), :]
    for t in range(T):
        ubf_ref[:, t * LANES:(t + 1) * LANES] = us_ref[t].astype(BF16)
    ubf = ubf_ref[...]
    st_ref[...] = jnp.dot(ubf, e_ref[0], preferred_element_type=F32)

    at = at_ref[0]
    a_fr, a_fi, a_br, a_bi = at[0:1], at[1:2], at[2:3], at[3:4]

    def scan(init, keep_entering):
        def step(i, carry):
            c_fr, c_fi, c_br, c_bi = carry
            rf = pl.multiple_of(i * npar, npar)
            rbk = pl.multiple_of((nc - 1 - i) * npar, npar)
            l_fr = st_ref[pl.ds(rf, npar), 0:ns]
            l_fi = st_ref[pl.ds(rf, npar), ns:2 * ns]
            l_br = st_ref[pl.ds(rbk, npar), 2 * ns:3 * ns]
            l_bi = st_ref[pl.ds(rbk, npar), 3 * ns:4 * ns]
            if keep_entering:
                st_ref[pl.ds(rf, npar), 0:ns] = c_fr
                st_ref[pl.ds(rf, npar), ns:2 * ns] = c_fi
                st_ref[pl.ds(rbk, npar), 2 * ns:3 * ns] = c_br
                st_ref[pl.ds(rbk, npar), 3 * ns:4 * ns] = c_bi
            return (a_fr * c_fr - a_fi * c_fi + l_fr, a_fr * c_fi + a_fi * c_fr + l_fi,
                    a_br * c_br - a_bi * c_bi + l_br, a_br * c_bi + a_bi * c_br + l_bi)

        return lax.fori_loop(0, nc, step, init)

    h0 = h0_ref[0, 0]
    h0 = [h0[:, k * ns:(k + 1) * ns] for k in range(4)]
    if nseg == 1:
        enter = tuple(h0)
    else:
        zero = jnp.zeros((npar, ns), F32)
        z_fr, z_fi, z_br, z_bi = scan((zero, zero, zero, zero), False)
        s_fr, s_fi, s_br, s_bi = at[4:5], at[5:6], at[6:7], at[7:8]
        rows_f, rows_b = [None] * npar, [None] * npar
        for b in range(bt):
            c_re, c_im = h0[0][b:b + 1], h0[1][b:b + 1]
            for s in range(nseg):
                p = b * nseg + s
                rows_f[p] = (c_re, c_im)
                c_re, c_im = (s_fr * c_re - s_fi * c_im + z_fr[p:p + 1], s_fr * c_im + s_fi * c_re + z_fi[p:p + 1])
            c_re, c_im = h0[2][b:b + 1], h0[3][b:b + 1]
            for s in reversed(range(nseg)):
                p = b * nseg + s
                rows_b[p] = (c_re, c_im)
                c_re, c_im = (s_br * c_re - s_bi * c_im + z_br[p:p + 1], s_br * c_im + s_bi * c_re + z_bi[p:p + 1])
        enter = (jnp.concatenate([r[0] for r in rows_f], axis=0), jnp.concatenate([r[1] for r in rows_f], axis=0),
                 jnp.concatenate([r[0] for r in rows_b], axis=0), jnp.concatenate([r[1] for r in rows_b], axis=0))
    fin = scan(enter, True)
    for b in range(bt):
        last_f, last_b = b * nseg + nseg - 1, b * nseg
        fin_ref[0, 0, b:b + 1, 0:ns] = fin[0][last_f:last_f + 1]
        fin_ref[0, 0, b:b + 1, ns:2 * ns] = fin[1][last_f:last_f + 1]
        fin_ref[0, 0, b:b + 1, 2 * ns:3 * ns] = fin[2][last_b:last_b + 1]
        fin_ref[0, 0, b:b + 1, 3 * ns:4 * ns] = fin[3][last_b:last_b + 1]

    y = (jnp.dot(ubf, m_ref[0], preferred_element_type=F32)
         + jnp.dot(st_ref[...].astype(BF16), f_ref[0], preferred_element_type=F32))
    for t in range(T):
        ys_ref[t] = y[:, t * LANES:(t + 1) * LANES] + d_ref[0] * us_ref[t]
    for p in range(npar):
        for t in range(T):
            y_ref[pl.ds(p * seg + t, nc, stride=T), :] = ys_ref[t, pl.ds(p, nc, stride=npar), :]


def _s5(proj, u_col, h0, e, m, f, at, d_skip, batch, seq, bt, nseg):
    T, S = S5_CHUNK, N_GROUP_SLABS
    rows = bt * seq // T
    const = lambda j, g: (j, 0, 0)
    y, fin = pl.pallas_call(
        functools.partial(_s5_kernel, bt=bt, nseg=nseg, seq=seq),
        grid=(S, batch // bt),
        in_specs=[pl.BlockSpec((bt * seq, LANES), lambda j, g: (g, u_col + j)),
                  pl.BlockSpec((1, 1, bt, SLAB_COLS), lambda j, g: (j, g, 0, 0)),
                  pl.BlockSpec((1, T * LANES, SLAB_COLS), const),
                  pl.BlockSpec((1, T * LANES, T * LANES), const),
                  pl.BlockSpec((1, SLAB_COLS, T * LANES), const),
                  pl.BlockSpec((1, 8, SLAB_STATE), const),
                  pl.BlockSpec((1, 1, LANES), const)],
        out_specs=[pl.BlockSpec((bt * seq, LANES), lambda j, g: (g, j)),
                   pl.BlockSpec((1, 1, bt, SLAB_COLS), lambda j, g: (j, g, 0, 0))],
        out_shape=[jax.ShapeDtypeStruct((batch * seq, S5_WIDTH), F32),
                   jax.ShapeDtypeStruct((S, batch // bt, bt, SLAB_COLS), F32)],
        scratch_shapes=[pltpu.VMEM((T, rows, LANES), F32), pltpu.VMEM((rows, T * LANES), BF16),
                        pltpu.VMEM((rows, SLAB_COLS), F32), pltpu.VMEM((T, rows, LANES), F32)],
        compiler_params=_cparams(("parallel", "arbitrary")),
        name="s5",
    )(proj, h0.reshape(S, batch // bt, bt, SLAB_COLS), e, m, f, at, d_skip.reshape(S, 1, LANES))
    return y, fin.reshape(S, batch, SLAB_COLS)


def _state_to_slabs(state):
    b = state.shape[0]
    s = state.reshape(b, 2, 2, N_GROUP_SLABS, SLAB_STATE).transpose(3, 0, 1, 2, 4)
    return s.reshape(N_GROUP_SLABS, b, SLAB_COLS)


def _slabs_to_state(slabs):
    b = slabs.shape[1]
    s = slabs.reshape(N_GROUP_SLABS, b, 2, 2, SLAB_STATE).transpose(1, 2, 3, 0, 4)
    return s.reshape(b, 2, 2, S5_GROUPS, S5_STATE)


def _merge_kernel(x_ref, mods_ref, ya_ref, yb_ref, ga_ref, gb_ref, wglu_ref, wua_ref, wub_ref, wo_ref, o_ref):
    yb = jax.nn.gelu(yb_ref[...])
    glu = jnp.dot(yb.astype(BF16), wglu_ref[...], preferred_element_type=F32)
    yb = yb * jax.nn.sigmoid(glu)
    a = jnp.dot(ya_ref[...].astype(BF16), wua_ref[...], preferred_element_type=F32)
    b = jnp.dot(yb.astype(BF16), wub_ref[...], preferred_element_type=F32)
    merged = jax.nn.sigmoid(ga_ref[...]) * a + jax.nn.sigmoid(gb_ref[...]) * b
    o = jnp.dot(merged.astype(BF16), wo_ref[...], preferred_element_type=F32)
    o_ref[...] = x_ref[...] + mods_ref[0][5:6] * o


def _merge(x, mods3, row0, cond_tokens, ya, yb, proj, w_glu, w_up_a, w_up_b, w_out):
    n_tok, d = x.shape
    tm = min(256, n_tok)
    assert cond_tokens % tm == 0
    tiles_per_row = cond_tokens // tm
    gate_col = (3 * NA_WIDTH + S5_WIDTH) // d
    assert gate_col * d == 3 * NA_WIDTH + S5_WIDTH
    resident = functools.partial(pl.BlockSpec, pipeline_mode=pl.Buffered(1))
    return pl.pallas_call(
        _merge_kernel,
        grid=(n_tok // tm,),
        in_specs=[pl.BlockSpec((tm, d), lambda i: (i, 0)),
                  pl.BlockSpec((1, N_MOD, d), lambda i: (row0 + i // tiles_per_row, 0, 0)),
                  pl.BlockSpec((tm, NA_WIDTH), lambda i: (i, 0)),
                  pl.BlockSpec((tm, S5_WIDTH), lambda i: (i, 0)),
                  pl.BlockSpec((tm, d), lambda i: (i, gate_col)),
                  pl.BlockSpec((tm, d), lambda i: (i, gate_col + 1)),
                  resident((S5_WIDTH, S5_WIDTH), lambda i: (0, 0)),
                  resident((NA_WIDTH, d), lambda i: (0, 0)),
                  resident((S5_WIDTH, d), lambda i: (0, 0)),
                  resident((d, d), lambda i: (0, 0))],
        out_specs=pl.BlockSpec((tm, d), lambda i: (i, 0)),
        out_shape=jax.ShapeDtypeStruct((n_tok, d), F32),
        compiler_params=_cparams(("parallel",)),
        name="merge",
    )(x, mods3, ya, yb, proj, proj, w_glu, w_up_a, w_up_b, w_out)


def _heads_first(t, batch, seq):
    return t.reshape(batch, seq, NA_HEADS, NA_HEAD_DIM).transpose(0, 2, 1, 3)


def _tokens_first(t):
    b, h, l, dh = t.shape
    return t.transpose(0, 2, 1, 3).reshape(b * l, h * dh)


def kernel(x_prompt, x_sample, cache_k, cache_v, state_ssm, c, c_ctx, w_ada, b_ada, norm_g, ffn_in, ffn_out, w_in, rpb, s5_lam_re, s5_lam_im, s5_log_dt, s5_b_re, s5_b_im, s5_c_re, s5_c_im, s5_d, w_glu, w_up_a, w_up_b, w_out, final_g):
    depth = w_ada.shape[0]
    pb, pl_, d = x_prompt.shape
    sb, sl, _ = x_sample.shape
    past = cache_k.shape[3]
    u_col = 3 * NA_WIDTH // LANES
    mod_rows = 8 * ((1 + sb + 7) // 8)

    xp = x_prompt.reshape(pb * pl_, d)
    xs = x_sample.reshape(sb * sl, d)
    cond = jnp.concatenate([c_ctx[None, :], c, jnp.zeros((mod_rows - 1 - sb, d), F32)], axis=0)
    new_k, new_v, new_s = [], [], []
    for l in range(depth):
        last = l == depth - 1
        mods3 = _mods(cond, w_ada[l], b_ada[l]).reshape(mod_rows, N_MOD, d)
        ffn1_in, ffn1_out = ffn_in[l, 0].astype(BF16), ffn_out[l, 0].astype(BF16)
        ffn2_in, ffn2_out = ffn_in[l, 1].astype(BF16), ffn_out[l, 1].astype(BF16)
        w_in_bf = w_in[l].astype(BF16)
        w_glu_bf, w_up_a_bf, w_up_b_bf, w_out_bf = (w[l].astype(BF16) for w in (w_glu, w_up_a, w_up_b, w_out))
        e, m, f, at = _s5_prep(s5_lam_re[l], s5_lam_im[l], s5_log_dt[l], s5_b_re[l], s5_b_im[l],
                               s5_c_re[l], s5_c_im[l], sl // S5_SEGMENTS)
        bias = _na_bias_tables(rpb[l], sl // GRID_W)

        ptok = pb * pl_
        xp = _ffn(xp, mods3, 0, ptok, norm_g[l, 0], final_g, ffn1_in, ffn1_out, 0, False)
        proj_p = _proj(xp, mods3, 0, ptok, norm_g[l, 1], w_in_bf)
        ya_p = _attn_ctx(proj_p, pb, pl_)
        yb_p, fin_p = _s5(proj_p, u_col, jnp.zeros((N_GROUP_SLABS, pb, SLAB_COLS), F32), e, m, f, at,
                          s5_d[l], pb, pl_, math.gcd(pb, 16), 1)
        xp = _merge(xp, mods3, 0, ptok, ya_p, yb_p, proj_p, w_glu_bf, w_up_a_bf, w_up_b_bf, w_out_bf)
        xp = _ffn(xp, mods3, 0, ptok, norm_g[l, 2], final_g, ffn2_in, ffn2_out, 6, last)
        new_k.append(_heads_first(proj_p[:, NA_WIDTH:2 * NA_WIDTH], pb, pl_))
        new_v.append(_heads_first(proj_p[:, 2 * NA_WIDTH:3 * NA_WIDTH], pb, pl_))
        new_s.append(_slabs_to_state(fin_p))

        xs = _ffn(xs, mods3, 1, sl, norm_g[l, 0], final_g, ffn1_in, ffn1_out, 0, False)
        proj_s = _proj(xs, mods3, 1, sl, norm_g[l, 1], w_in_bf)
        ya_s = _attn_na(proj_s, _tokens_first(cache_k[:, l]), _tokens_first(cache_v[:, l]), bias, sb, sl, past)
        yb_s, _ = _s5(proj_s, u_col, _state_to_slabs(state_ssm[:, l]), e, m, f, at, s5_d[l], sb, sl, 1, S5_SEGMENTS)
        xs = _merge(xs, mods3, 1, sl, ya_s, yb_s, proj_s, w_glu_bf, w_up_a_bf, w_up_b_bf, w_out_bf)
        xs = _ffn(xs, mods3, 1, sl, norm_g[l, 2], final_g, ffn2_in, ffn2_out, 6, last)

    y_prompt = xp.reshape(pb, pl_, d)
    y_sample = xs.reshape(sb, sl, d)
    return (y_prompt, y_sample, jnp.stack(new_k, axis=1), jnp.stack(new_v, axis=1), jnp.stack(new_s, axis=1))
```

```python
import functools
import math

import numpy as np
import jax
import jax.numpy as jnp
from jax import lax
from jax.experimental import pallas as pl
from jax.experimental.pallas import tpu as pltpu

F32 = jnp.float32
BF16 = jnp.bfloat16

GRID_W = 64
NA_HEADS = 16
NA_HEAD_DIM = 64
NA_WIDTH = NA_HEADS * NA_HEAD_DIM
WIN_R = 8
WIN_C = 16
S5_GROUP_CH = 16
S5_WIDTH = 1024
S5_GROUPS = S5_WIDTH // S5_GROUP_CH
S5_STATE = 64
N_MOD = 9
EPS = 1e-6

LANES = 128
HEADS_PER_SLAB = LANES // NA_HEAD_DIM
N_HEAD_SLABS = NA_WIDTH // LANES
GROUPS_PER_SLAB = LANES // S5_GROUP_CH
N_GROUP_SLABS = S5_WIDTH // LANES
SLAB_STATE = GROUPS_PER_SLAB * S5_STATE
SLAB_COLS = 4 * SLAB_STATE
S5_CHUNK = 8
S5_SEGMENTS = 8
NA_QROWS = 4
NA_KROWS = NA_QROWS + WIN_R
NEG_BIAS = -1e30
LOG2E = 1.4426950408889634
VMEM_LIMIT = 56 * 1024 * 1024


def _cparams(sem, limit=VMEM_LIMIT):
    return pltpu.CompilerParams(dimension_semantics=sem, vmem_limit_bytes=limit)


def _silu(x):
    return x * jax.nn.sigmoid(x)


NORM_ROWS = 16
NORM_UNROLL = 8


def _norm_mod_store(x_ref, h_ref, g, shift, scale):
    gs = g * (1.0 + scale)

    def body(i, carry):
        r0 = pl.multiple_of(i * NORM_ROWS, NORM_ROWS)
        x = x_ref[pl.ds(r0, NORM_ROWS), :]
        r = lax.rsqrt(jnp.mean(x * x, axis=-1, keepdims=True) + EPS)
        h_ref[pl.ds(r0, NORM_ROWS), :] = ((x * r) * gs + shift).astype(BF16)
        return carry

    lax.fori_loop(0, x_ref.shape[0] // NORM_ROWS, body, 0, unroll=NORM_UNROLL)


def _mods_kernel(c_ref, w_ref, b_ref, o_ref):
    a = _silu(c_ref[...])
    o_ref[...] = jnp.dot(a, w_ref[...], preferred_element_type=F32,
                         precision=lax.Precision.HIGHEST) + b_ref[...]


def _mods(cond, w_ada, b_ada):
    rows, d = cond.shape
    n = w_ada.shape[1]
    tn = math.gcd(1024, n)
    return pl.pallas_call(
        _mods_kernel,
        grid=(n // tn,),
        in_specs=[pl.BlockSpec((rows, d), lambda j: (0, 0)),
                  pl.BlockSpec((d, tn), lambda j: (0, j)),
                  pl.BlockSpec((1, tn), lambda j: (0, j))],
        out_specs=pl.BlockSpec((rows, tn), lambda j: (0, j)),
        out_shape=jax.ShapeDtypeStruct((rows, n), F32),
        compiler_params=_cparams(("arbitrary",)),
        name="mods",
    )(cond, w_ada, b_ada.reshape(1, n))


def _ffn_kernel(x_hbm, mods_ref, ng_ref, fg_ref, wg_ref, wu_ref, wo_ref, o_ref, xs_ref, h_ref, rs_ref, sem,
                *, mod_base, final_norm):
    i, j = pl.program_id(0), pl.program_id(1)
    tm = xs_ref.shape[0]

    def x_copy(tile):
        return pltpu.make_async_copy(x_hbm.at[pl.ds(tile * tm, tm), :], xs_ref, sem)

    @pl.when((i == 0) & (j == 0))
    def _():
        x_copy(0).start()

    @pl.when(j == 0)
    def _():
        x_copy(i).wait()
        m = mods_ref[0]
        gs = ng_ref[...] * (1.0 + m[mod_base + 1:mod_base + 2])
        shift = m[mod_base:mod_base + 1]

        def body(r, carry):
            r0 = pl.multiple_of(r * NORM_ROWS, NORM_ROWS)
            x = xs_ref[pl.ds(r0, NORM_ROWS), :]
            o_ref[pl.ds(r0, NORM_ROWS), :] = x
            rs = lax.rsqrt(jnp.mean(x * x, axis=-1, keepdims=True) + EPS)
            h_ref[pl.ds(r0, NORM_ROWS), :] = ((x * rs) * gs + shift).astype(BF16)
            return carry

        lax.fori_loop(0, tm // NORM_ROWS, body, 0, unroll=NORM_UNROLL)

    @pl.when((j == 1) & (i + 1 < pl.num_programs(0)))
    def _():
        x_copy(i + 1).start()

    h = h_ref[...]
    g = jnp.dot(h, wg_ref[...], preferred_element_type=F32)
    u = jnp.dot(h, wu_ref[...], preferred_element_type=F32)
    a = (_silu(g) * u).astype(BF16)
    half_gate = 0.5 * mods_ref[0][mod_base + 2:mod_base + 3]
    o_ref[...] += half_gate * jnp.dot(a, wo_ref[...], preferred_element_type=F32)

    if final_norm:
        @pl.when(j == pl.num_programs(1) - 1)
        def _():
            def body(r, carry):
                r0 = pl.multiple_of(r * NORM_ROWS, NORM_ROWS)
                y = o_ref[pl.ds(r0, NORM_ROWS), :]
                rs = lax.rsqrt(jnp.mean(y * y, axis=-1, keepdims=True) + EPS)
                rs_ref[pl.ds(r0, NORM_ROWS), :] = jnp.broadcast_to(rs, (NORM_ROWS, LANES))
                return carry

            lax.fori_loop(0, tm // NORM_ROWS, body, 0, unroll=NORM_UNROLL)
            for c in range(o_ref.shape[1] // LANES):
                cols = slice(c * LANES, (c + 1) * LANES)
                o_ref[:, cols] = (o_ref[:, cols] * rs_ref[...]) * fg_ref[:, cols]


FFN_TM = 1024
FFN_TF = 512


def _ffn(x, mods3, row0, cond_tokens, norm_g, final_g, w_in_bf, w_out_bf, mod_base, final_norm):
    n_tok, d = x.shape
    d_ff = w_out_bf.shape[0]
    tm = min(FFN_TM, n_tok)
    tf = min(FFN_TF, d_ff)
    assert cond_tokens % tm == 0
    tiles_per_row = cond_tokens // tm
    assert d_ff // tf >= 2
    kern = functools.partial(_ffn_kernel, mod_base=mod_base, final_norm=final_norm)
    return pl.pallas_call(
        kern,
        grid=(n_tok // tm, d_ff // tf),
        in_specs=[pl.BlockSpec(memory_space=pl.ANY),
                  pl.BlockSpec((1, N_MOD, d), lambda i, j: (row0 + i // tiles_per_row, 0, 0)),
                  pl.BlockSpec((1, d), lambda i, j: (0, 0)),
                  pl.BlockSpec((1, d), lambda i, j: (0, 0)),
                  pl.BlockSpec((d, tf), lambda i, j: (0, j)),
                  pl.BlockSpec((d, tf), lambda i, j: (0, d_ff // tf + j)),
                  pl.BlockSpec((tf, d), lambda i, j: (j, 0))],
        out_specs=pl.BlockSpec((tm, d), lambda i, j: (i, 0)),
        out_shape=jax.ShapeDtypeStruct((n_tok, d), F32),
        scratch_shapes=[pltpu.VMEM((tm, d), F32), pltpu.VMEM((tm, d), BF16), pltpu.VMEM((tm, LANES), F32),
                        pltpu.SemaphoreType.DMA(())],
        compiler_params=_cparams(("arbitrary", "arbitrary")),
        name="ffn",
    )(x, mods3, norm_g.reshape(1, d), final_g.reshape(1, d), w_in_bf, w_in_bf, w_out_bf)


def _proj_kernel(x_ref, mods_ref, ng_ref, w_ref, o_ref, h_ref):
    @pl.when(pl.program_id(1) == 0)
    def _():
        m = mods_ref[0]
        _norm_mod_store(x_ref, h_ref, ng_ref[...], m[3:4], m[4:5])

    o_ref[...] = jnp.dot(h_ref[...], w_ref[...], preferred_element_type=F32)


def _proj(x, mods3, row0, cond_tokens, norm_g, w_bf):
    n_tok, d = x.shape
    n = w_bf.shape[1]
    tm = min(1024, n_tok)
    tn = math.gcd(1024, d)
    assert cond_tokens % tm == 0
    tiles_per_row = cond_tokens // tm
    return pl.pallas_call(
        _proj_kernel,
        grid=(n_tok // tm, n // tn),
        in_specs=[pl.BlockSpec((tm, d), lambda i, j: (i, 0)),
                  pl.BlockSpec((1, N_MOD, d), lambda i, j: (row0 + i // tiles_per_row, 0, 0)),
                  pl.BlockSpec((1, d), lambda i, j: (0, 0)),
                  pl.BlockSpec((d, tn), lambda i, j: (0, j))],
        out_specs=pl.BlockSpec((tm, tn), lambda i, j: (i, j)),
        out_shape=jax.ShapeDtypeStruct((n_tok, n), F32),
        scratch_shapes=[pltpu.VMEM((tm, d), BF16)],
        compiler_params=_cparams(("parallel", "arbitrary")),
        name="proj",
    )(x, mods3, norm_g.reshape(1, d), w_bf)


def _head_mask(shape, hh):
    lane = lax.broadcasted_iota(jnp.int32, shape, 1)
    return (lane >= hh * NA_HEAD_DIM) & (lane < (hh + 1) * NA_HEAD_DIM)


def _nt_dot(a, b):
    return lax.dot_general(a, b, (((1,), (1,)), ((), ())), preferred_element_type=F32)


Q_SCALE = NA_HEAD_DIM ** -0.5 * LOG2E
CTX_SLABS = 4
NA_SLABS = 2


def _ones_other_head(v, hh):
    return jnp.where(_head_mask(v.shape, hh), v, 1.0)


def _finish_heads(o0, o1):
    first = _head_mask(o0.shape, 0)
    num = jnp.where(first, o0, o1)
    den = pltpu.roll(jnp.where(first, o1, o0), NA_HEAD_DIM, 1)
    return num / den


def _attn_ctx_kernel(q_ref, k_ref, v_ref, o_ref, ko_ref, vo_ref):
    for sl in range(q_ref.shape[1] // LANES):
        cols = slice(sl * LANES, (sl + 1) * LANES)
        q = q_ref[:, cols] * Q_SCALE
        kf = k_ref[:, cols]
        vf = v_ref[:, cols]
        k = kf.astype(BF16)
        outs = []
        for hh in range(HEADS_PER_SLAB):
            head = slice(hh * NA_HEAD_DIM, (hh + 1) * NA_HEAD_DIM)
            ko_ref[0, sl * HEADS_PER_SLAB + hh] = kf[:, head]
            vo_ref[0, sl * HEADS_PER_SLAB + hh] = vf[:, head]
            qm = jnp.where(_head_mask(q.shape, hh), q, 0.0).astype(BF16)
            s = _nt_dot(qm, k)
            p = jnp.exp2(s - jnp.max(s, axis=-1, keepdims=True))
            outs.append(jnp.dot(p.astype(BF16), _ones_other_head(vf, hh).astype(BF16),
                                preferred_element_type=F32))
        o_ref[:, cols] = _finish_heads(*outs)


def _attn_ctx(proj, batch, seq):
    w = CTX_SLABS * LANES
    nsb = NA_WIDTH // w
    hps = CTX_SLABS * HEADS_PER_SLAB
    kv_spec = pl.BlockSpec((1, hps, seq, NA_HEAD_DIM), lambda b, s: (b, s, 0, 0))
    kv_shape = jax.ShapeDtypeStruct((batch, NA_HEADS, seq, NA_HEAD_DIM), F32)
    return pl.pallas_call(
        _attn_ctx_kernel,
        grid=(batch, nsb),
        in_specs=[pl.BlockSpec((seq, w), lambda b, s: (b, s)),
                  pl.BlockSpec((seq, w), lambda b, s: (b, nsb + s)),
                  pl.BlockSpec((seq, w), lambda b, s: (b, 2 * nsb + s))],
        out_specs=[pl.BlockSpec((seq, w), lambda b, s: (b, s)), kv_spec, kv_spec],
        out_shape=[jax.ShapeDtypeStruct((batch * seq, NA_WIDTH), F32), kv_shape, kv_shape],
        compiler_params=_cparams(("parallel", "parallel")),
        name="attn_ctx",
    )(proj, proj, proj)


def _attn_na_kernel(q_ref, k_ref, v_ref, ck_ref, cv_ref, bias_ref, o_ref,
                    kb_ref, vb_ref, ckb_ref, cvb_ref, *, rows):
    rb = pl.program_id(2)
    nsl = q_ref.shape[1] // LANES

    @pl.when(rb == 0)
    def _():
        kb_ref[...] = k_ref[...].astype(BF16)
        vb_ref[...] = v_ref[...].astype(BF16)
        ckb_ref[...] = ck_ref[...].astype(BF16)
        cvb_ref[...] = cv_ref[...].astype(BF16)

    start = jnp.clip(rb * NA_QROWS - WIN_R // 2, 0, rows - NA_KROWS)
    off = pl.multiple_of(start * GRID_W, NA_QROWS * GRID_W)
    for sl in range(nsl):
        cols = slice(sl * LANES, (sl + 1) * LANES)
        kw = kb_ref[pl.ds(off, NA_KROWS * GRID_W), cols]
        vw = vb_ref[pl.ds(off, NA_KROWS * GRID_W), cols]
        ck = ckb_ref[:, cols]
        cv = cvb_ref[:, cols]
        q = q_ref[:, cols] * Q_SCALE
        outs = []
        for hh in range(HEADS_PER_SLAB):
            qm = jnp.where(_head_mask(q.shape, hh), q, 0.0).astype(BF16)
            s_loc = _nt_dot(qm, kw) + bias_ref[0, sl * HEADS_PER_SLAB + hh]
            s_ctx = _nt_dot(qm, ck)
            m = jnp.maximum(jnp.max(s_loc, axis=-1, keepdims=True), jnp.max(s_ctx, axis=-1, keepdims=True))
            p_loc = jnp.exp2(s_loc - m)
            p_ctx = jnp.exp2(s_ctx - m)
            l = jnp.sum(p_loc, axis=-1, keepdims=True) + jnp.sum(p_ctx, axis=-1, keepdims=True)
            o = (jnp.dot(p_loc.astype(BF16), vw, preferred_element_type=F32)
                 + jnp.dot(p_ctx.astype(BF16), cv, preferred_element_type=F32))
            outs.append(o / l)
        o_ref[:, cols] = jnp.where(_head_mask(q.shape, 0), outs[0], outs[1])


def _na_bias_kernel(rpb_ref, o_ref, *, rows):
    shape = (GRID_W, LANES)
    qc = lax.broadcasted_iota(jnp.int32, shape, 0)
    lane = lax.broadcasted_iota(jnp.int32, shape, 1)
    kc = lane % GRID_W
    cs = jnp.clip(qc - WIN_C // 2, 0, GRID_W - WIN_C)
    valid_c = (kc >= cs) & (kc < cs + WIN_C)
    first_row = lane < GRID_W
    r = rpb_ref[0] * LOG2E
    for p, r0 in enumerate((0, NA_QROWS, rows - NA_QROWS)):
        start = min(max(r0 - WIN_R // 2, 0), rows - NA_KROWS)
        for qr in range(NA_QROWS):
            q_row = r0 + qr
            ws = min(max(q_row - WIN_R // 2, 0), rows - WIN_R)
            for k2 in range(NA_KROWS // 2):
                halves = []
                for kk in range(2):
                    k_row = start + 2 * k2 + kk
                    if ws <= k_row < ws + WIN_R:
                        dr = k_row - q_row + WIN_R - 1
                        row = jnp.broadcast_to(r[dr:dr + 1, :], shape)
                        shift = (LANES - (WIN_C - 1) + GRID_W * kk) % LANES
                        halves.append(pltpu.roll(row, shift, 1, stride=1, stride_axis=0))
                    else:
                        halves.append(jnp.full(shape, NEG_BIAS, F32))
                tile = jnp.where(valid_c, jnp.where(first_row, halves[0], halves[1]), NEG_BIAS)
                o_ref[p, 0, qr * GRID_W:(qr + 1) * GRID_W, k2 * LANES:(k2 + 1) * LANES] = tile


def _na_bias_tables(rpb, rows):
    h, nr, nc = rpb.shape
    rpb_pad = jnp.zeros((h, 16, LANES), F32).at[:, :nr, :nc].set(rpb)
    tq, tk = NA_QROWS * GRID_W, NA_KROWS * GRID_W
    return pl.pallas_call(
        functools.partial(_na_bias_kernel, rows=rows),
        grid=(h,),
        in_specs=[pl.BlockSpec((1, 16, LANES), lambda i: (i, 0, 0))],
        out_specs=pl.BlockSpec((3, 1, tq, tk), lambda i: (0, i, 0, 0)),
        out_shape=jax.ShapeDtypeStruct((3, h, tq, tk), F32),
        compiler_params=_cparams(("parallel",)),
        name="na_bias",
    )(rpb_pad)


def _attn_na(proj, ck_tok, cv_tok, bias, batch, seq, past):
    rows = seq // GRID_W
    assert rows >= NA_KROWS and rows % NA_QROWS == 0
    nrb = rows // NA_QROWS
    tq = NA_QROWS * GRID_W

    def bias_map(b, s, r):
        return ((r > 0).astype(jnp.int32) + (r == nrb - 1).astype(jnp.int32), s, 0, 0)

    w = NA_SLABS * LANES
    nsb = NA_WIDTH // w
    return pl.pallas_call(
        functools.partial(_attn_na_kernel, rows=rows),
        grid=(batch, nsb, nrb),
        in_specs=[pl.BlockSpec((tq, w), lambda b, s, r: (b * nrb + r, s)),
                  pl.BlockSpec((seq, w), lambda b, s, r: (b, nsb + s)),
                  pl.BlockSpec((seq, w), lambda b, s, r: (b, 2 * nsb + s)),
                  pl.BlockSpec((past, w), lambda b, s, r: (b, s)),
                  pl.BlockSpec((past, w), lambda b, s, r: (b, s)),
                  pl.BlockSpec((1, NA_SLABS * HEADS_PER_SLAB, tq, NA_KROWS * GRID_W), bias_map)],
        out_specs=pl.BlockSpec((tq, w), lambda b, s, r: (b * nrb + r, s)),
        out_shape=jax.ShapeDtypeStruct((batch * seq, NA_WIDTH), F32),
        scratch_shapes=[pltpu.VMEM((seq, w), BF16), pltpu.VMEM((seq, w), BF16),
                        pltpu.VMEM((past, w), BF16), pltpu.VMEM((past, w), BF16)],
        compiler_params=_cparams(("parallel", "parallel", "arbitrary")),
        name="attn_na",
    )(proj, proj, proj, ck_tok, cv_tok, bias)


def _s5_prep_kernel(lr_ref, li_ref, ld_ref, btr_ref, bti_ref, cr_ref, ci_ref,
                    e_ref, m_ref, f_ref, at_ref, *, seg_tokens):
    T = S5_CHUNK
    ns = SLAB_STATE
    rb = lax.broadcasted_iota(jnp.int32, (LANES, ns), 0) // S5_GROUP_CH
    cb = lax.broadcasted_iota(jnp.int32, (LANES, ns), 1) // S5_STATE
    mask_b = rb == cb

    def powers(ldr, ldi, n):
        mag = jnp.exp(ldr * float(n))
        return mag * jnp.cos(ldi * float(n)), mag * jnp.sin(ldi * float(n))

    kt = [[None] * T, [None] * T]
    for d in range(2):
        lam_re, lam_im = lr_ref[0, d], li_ref[0, d]
        dt = jnp.exp(ld_ref[0, d])
        ldr, ldi = lam_re * dt, lam_im * dt
        a_re, a_im = powers(ldr, ldi, 1)
        mag2 = lam_re * lam_re + lam_im * lam_im
        f_re = ((a_re - 1.0) * lam_re + a_im * lam_im) / mag2
        f_im = (a_im * lam_re - (a_re - 1.0) * lam_im) / mag2
        btr = jnp.where(mask_b, btr_ref[0, d], 0.0)
        bti = jnp.where(mask_b, bti_ref[0, d], 0.0)
        cr = jnp.where(mask_b, cr_ref[0, d], 0.0)
        ci = jnp.where(mask_b, ci_ref[0, d], 0.0)
        ctr, cti = cr.T, ci.T
        c0 = d * 2 * ns
        e_res, e_ims = [], []
        for n in range(T):
            an_re, an_im = powers(ldr, ldi, n)
            w_re = f_re * an_re - f_im * an_im
            w_im = f_re * an_im + f_im * an_re
            e_re = btr * w_re - bti * w_im
            e_im = btr * w_im + bti * w_re
            t = T - 1 - n if d == 0 else n
            e_ref[0, t * LANES:(t + 1) * LANES, c0:c0 + ns] = e_re.astype(BF16)
            e_ref[0, t * LANES:(t + 1) * LANES, c0 + ns:c0 + 2 * ns] = e_im.astype(BF16)
            e_res.append(e_re)
            e_ims.append(e_im)
        k_all = (jnp.dot(jnp.concatenate(e_res, axis=0), ctr, preferred_element_type=F32,
                         precision=lax.Precision.HIGHEST)
                 - jnp.dot(jnp.concatenate(e_ims, axis=0), cti, preferred_element_type=F32,
                           precision=lax.Precision.HIGHEST))
        for n in range(T):
            kt[d][n] = k_all[n * LANES:(n + 1) * LANES]
        for tp in range(T):
            n = tp + 1 if d == 0 else T - tp
            an_re, an_im = powers(ldr, ldi, n)
            f_ref[0, c0:c0 + ns, tp * LANES:(tp + 1) * LANES] = (cr * an_re - ci * an_im).T.astype(BF16)
            f_ref[0, c0 + ns:c0 + 2 * ns, tp * LANES:(tp + 1) * LANES] = (-(cr * an_im + ci * an_re)).T.astype(BF16)
        at_re, at_im = powers(ldr, ldi, T)
        at_ref[0, 2 * d:2 * d + 1, :] = at_re
        at_ref[0, 2 * d + 1:2 * d + 2, :] = at_im
        as_re, as_im = powers(ldr, ldi, seg_tokens)
        at_ref[0, 4 + 2 * d:5 + 2 * d, :] = as_re
        at_ref[0, 5 + 2 * d:6 + 2 * d, :] = as_im
    for t in range(T):
        for tp in range(T):
            if tp > t:
                blk = kt[0][tp - t]
            elif tp < t:
                blk = kt[1][t - tp]
            else:
                blk = kt[0][0] + kt[1][0]
            m_ref[0, t * LANES:(t + 1) * LANES, tp * LANES:(tp + 1) * LANES] = blk.astype(BF16)


def _s5_prep(lam_re, lam_im, log_dt, b_re, b_im, c_re, c_im, seg_tokens):
    T, S, G8, ns = S5_CHUNK, N_GROUP_SLABS, GROUPS_PER_SLAB, SLAB_STATE

    def state_vec(v):
        return v.reshape(2, S, ns).transpose(1, 0, 2)

    lr, li = state_vec(lam_re), state_vec(lam_im)
    ld = state_vec(jnp.broadcast_to(log_dt[:, :, None], lam_re.shape))

    def bt(v):
        v = v.reshape(2, S, G8, S5_STATE, S5_GROUP_CH).transpose(1, 0, 2, 4, 3).reshape(S, 2, LANES, S5_STATE)
        return jnp.tile(v, (1, 1, 1, G8))

    def crow(v):
        v = v.reshape(2, S, G8, S5_GROUP_CH, S5_STATE).transpose(1, 0, 2, 3, 4).reshape(S, 2, LANES, S5_STATE)
        return jnp.tile(v, (1, 1, 1, G8))

    row = lambda v: v[:, :, None, :]
    spec_row = pl.BlockSpec((1, 2, 1, ns), lambda j: (j, 0, 0, 0))
    spec_bt = pl.BlockSpec((1, 2, LANES, ns), lambda j: (j, 0, 0, 0))
    return pl.pallas_call(
        functools.partial(_s5_prep_kernel, seg_tokens=seg_tokens),
        grid=(S,),
        in_specs=[spec_row, spec_row, spec_row, spec_bt, spec_bt, spec_bt, spec_bt],
        out_specs=[pl.BlockSpec((1, T * LANES, SLAB_COLS), lambda j: (j, 0, 0)),
                   pl.BlockSpec((1, T * LANES, T * LANES), lambda j: (j, 0, 0)),
                   pl.BlockSpec((1, SLAB_COLS, T * LANES), lambda j: (j, 0, 0)),
                   pl.BlockSpec((1, 8, ns), lambda j: (j, 0, 0))],
        out_shape=[jax.ShapeDtypeStruct((S, T * LANES, SLAB_COLS), BF16),
                   jax.ShapeDtypeStruct((S, T * LANES, T * LANES), BF16),
                   jax.ShapeDtypeStruct((S, SLAB_COLS, T * LANES), BF16),
                   jax.ShapeDtypeStruct((S, 8, ns), F32)],
        compiler_params=_cparams(("parallel",)),
        name="s5_prep",
    )(row(lr), row(li), row(ld), bt(b_re), bt(b_im), crow(c_re), crow(c_im))


def _s5_kernel(u_ref, h0_ref, e_ref, m_ref, f_ref, at_ref, d_ref, y_ref, fin_ref,
               us_ref, ubf_ref, st_ref, ys_ref, *, bt, nseg, seq):
    T = S5_CHUNK
    ns = SLAB_STATE
    npar = bt * nseg
    seg = seq // nseg
    nc = seg // T
    for p in range(npar):
        for t in range(T):
            us_ref[t, pl.ds(p, nc, stride=npar), :] = u_ref[pl.ds(p * seg + t, nc, stride=T), :]
    for t in range(T):
        ubf_ref[:, t * LANES:(t + 1) * LANES] = us_ref[t].astype(BF16)
    ubf = ubf_ref[...]
    st_ref[...] = jnp.dot(ubf, e_ref[0], preferred_element_type=F32)

    at = at_ref[0]
    a_fr, a_fi, a_br, a_bi = at[0:1], at[1:2], at[2:3], at[3:4]

    def scan(init, keep_entering):
        def step(i, carry):
            c_fr, c_fi, c_br, c_bi = carry
            rf = pl.multiple_of(i * npar, npar)
            rbk = pl.multiple_of((nc - 1 - i) * npar, npar)
            l_fr = st_ref[pl.ds(rf, npar), 0:ns]
            l_fi = st_ref[pl.ds(rf, npar), ns:2 * ns]
            l_br = st_ref[pl.ds(rbk, npar), 2 * ns:3 * ns]
            l_bi = st_ref[pl.ds(rbk, npar), 3 * ns:4 * ns]
            if keep_entering:
                st_ref[pl.ds(rf, npar), 0:ns] = c_fr
                st_ref[pl.ds(rf, npar), ns:2 * ns] = c_fi
                st_ref[pl.ds(rbk, npar), 2 * ns:3 * ns] = c_br
                st_ref[pl.ds(rbk, npar), 3 * ns:4 * ns] = c_bi
            return (a_fr * c_fr - a_fi * c_fi + l_fr, a_fr * c_fi + a_fi * c_fr + l_fi,
                    a_br * c_br - a_bi * c_bi + l_br, a_br * c_bi + a_bi * c_br + l_bi)

        return lax.fori_loop(0, nc, step, init)

    h0 = h0_ref[0, 0]
    h0 = [h0[:, k * ns:(k + 1) * ns] for k in range(4)]
    if nseg == 1:
        enter = tuple(h0)
    else:
        zero = jnp.zeros((npar, ns), F32)
        z_fr, z_fi, z_br, z_bi = scan((zero, zero, zero, zero), False)
        s_fr, s_fi, s_br, s_bi = at[4:5], at[5:6], at[6:7], at[7:8]
        rows_f, rows_b = [None] * npar, [None] * npar
        for b in range(bt):
            c_re, c_im = h0[0][b:b + 1], h0[1][b:b + 1]
            for s in range(nseg):
                p = b * nseg + s
                rows_f[p] = (c_re, c_im)
                c_re, c_im = (s_fr * c_re - s_fi * c_im + z_fr[p:p + 1], s_fr * c_im + s_fi * c_re + z_fi[p:p + 1])
            c_re, c_im = h0[2][b:b + 1], h0[3][b:b + 1]
            for s in reversed(range(nseg)):
                p = b * nseg + s
                rows_b[p] = (c_re, c_im)
                c_re, c_im = (s_br * c_re - s_bi * c_im + z_br[p:p + 1], s_br * c_im + s_bi * c_re + z_bi[p:p + 1])
        enter = (jnp.concatenate([r[0] for r in rows_f], axis=0), jnp.concatenate([r[1] for r in rows_f], axis=0),
                 jnp.concatenate([r[0] for r in rows_b], axis=0), jnp.concatenate([r[1] for r in rows_b], axis=0))
    fin = scan(enter, True)
    for b in range(bt):
        last_f, last_b = b * nseg + nseg - 1, b * nseg
        fin_ref[0, 0, b:b + 1, 0:ns] = fin[0][last_f:last_f + 1]
        fin_ref[0, 0, b:b + 1, ns:2 * ns] = fin[1][last_f:last_f + 1]
        fin_ref[0, 0, b:b + 1, 2 * ns:3 * ns] = fin[2][last_b:last_b + 1]
        fin_ref[0, 0, b:b + 1, 3 * ns:4 * ns] = fin[3][last_b:last_b + 1]

    y = (jnp.dot(ubf, m_ref[0], preferred_element_type=F32)
         + jnp.dot(st_ref[...].astype(BF16), f_ref[0], preferred_element_type=F32))
    for t in range(T):
        ys_ref[t] = y[:, t * LANES:(t + 1) * LANES] + d_ref[0] * us_ref[t]
    for p in range(npar):
        for t in range(T):
            y_ref[pl.ds(p * seg + t, nc, stride=T), :] = ys_ref[t, pl.ds(p, nc, stride=npar), :]


def _s5(proj, u_col, h0, e, m, f, at, d_skip, batch, seq, bt, nseg):
    T, S = S5_CHUNK, N_GROUP_SLABS
    rows = bt * seq // T
    const = lambda j, g: (j, 0, 0)
    y, fin = pl.pallas_call(
        functools.partial(_s5_kernel, bt=bt, nseg=nseg, seq=seq),
        grid=(S, batch // bt),
        in_specs=[pl.BlockSpec((bt * seq, LANES), lambda j, g: (g, u_col + j)),
                  pl.BlockSpec((1, 1, bt, SLAB_COLS), lambda j, g: (j, g, 0, 0)),
                  pl.BlockSpec((1, T * LANES, SLAB_COLS), const),
                  pl.BlockSpec((1, T * LANES, T * LANES), const),
                  pl.BlockSpec((1, SLAB_COLS, T * LANES), const),
                  pl.BlockSpec((1, 8, SLAB_STATE), const),
                  pl.BlockSpec((1, 1, LANES), const)],
        out_specs=[pl.BlockSpec((bt * seq, LANES), lambda j, g: (g, j)),
                   pl.BlockSpec((1, 1, bt, SLAB_COLS), lambda j, g: (j, g, 0, 0))],
        out_shape=[jax.ShapeDtypeStruct((batch * seq, S5_WIDTH), F32),
                   jax.ShapeDtypeStruct((S, batch // bt, bt, SLAB_COLS), F32)],
        scratch_shapes=[pltpu.VMEM((T, rows, LANES), F32), pltpu.VMEM((rows, T * LANES), BF16),
                        pltpu.VMEM((rows, SLAB_COLS), F32), pltpu.VMEM((T, rows, LANES), F32)],
        compiler_params=_cparams(("parallel", "arbitrary")),
        name="s5",
    )(proj, h0.reshape(S, batch // bt, bt, SLAB_COLS), e, m, f, at, d_skip.reshape(S, 1, LANES))
    return y, fin.reshape(S, batch, SLAB_COLS)


def _state_to_slabs(state):
    b = state.shape[0]
    s = state.reshape(b, 2, 2, N_GROUP_SLABS, SLAB_STATE).transpose(3, 0, 1, 2, 4)
    return s.reshape(N_GROUP_SLABS, b, SLAB_COLS)


def _slabs_to_state(slabs):
    b = slabs.shape[1]
    s = slabs.reshape(N_GROUP_SLABS, b, 2, 2, SLAB_STATE).transpose(1, 2, 3, 0, 4)
    return s.reshape(b, 2, 2, S5_GROUPS, S5_STATE)


def _merge_kernel(x_ref, mods_ref, ya_ref, yb_ref, ga_ref, gb_ref, wglu_ref, wua_ref, wub_ref, wo_ref, o_ref):
    yb = jax.nn.gelu(yb_ref[...])
    glu = jnp.dot(yb.astype(BF16), wglu_ref[...], preferred_element_type=F32)
    yb = yb * jax.nn.sigmoid(glu)
    a = jnp.dot(ya_ref[...].astype(BF16), wua_ref[...], preferred_element_type=F32)
    b = jnp.dot(yb.astype(BF16), wub_ref[...], preferred_element_type=F32)
    merged = jax.nn.sigmoid(ga_ref[...]) * a + jax.nn.sigmoid(gb_ref[...]) * b
    o = jnp.dot(merged.astype(BF16), wo_ref[...], preferred_element_type=F32)
    o_ref[...] = x_ref[...] + mods_ref[0][5:6] * o


def _merge(x, mods3, row0, cond_tokens, ya, yb, proj, w_glu, w_up_a, w_up_b, w_out):
    n_tok, d = x.shape
    tm = min(256, n_tok)
    assert cond_tokens % tm == 0
    tiles_per_row = cond_tokens // tm
    gate_col = (3 * NA_WIDTH + S5_WIDTH) // d
    assert gate_col * d == 3 * NA_WIDTH + S5_WIDTH
    resident = functools.partial(pl.BlockSpec, pipeline_mode=pl.Buffered(1))
    return pl.pallas_call(
        _merge_kernel,
        grid=(n_tok // tm,),
        in_specs=[pl.BlockSpec((tm, d), lambda i: (i, 0)),
                  pl.BlockSpec((1, N_MOD, d), lambda i: (row0 + i // tiles_per_row, 0, 0)),
                  pl.BlockSpec((tm, NA_WIDTH), lambda i: (i, 0)),
                  pl.BlockSpec((tm, S5_WIDTH), lambda i: (i, 0)),
                  pl.BlockSpec((tm, d), lambda i: (i, gate_col)),
                  pl.BlockSpec((tm, d), lambda i: (i, gate_col + 1)),
                  resident((S5_WIDTH, S5_WIDTH), lambda i: (0, 0)),
                  resident((NA_WIDTH, d), lambda i: (0, 0)),
                  resident((S5_WIDTH, d), lambda i: (0, 0)),
                  resident((d, d), lambda i: (0, 0))],
        out_specs=pl.BlockSpec((tm, d), lambda i: (i, 0)),
        out_shape=jax.ShapeDtypeStruct((n_tok, d), F32),
        compiler_params=_cparams(("parallel",)),
        name="merge",
    )(x, mods3, ya, yb, proj, proj, w_glu, w_up_a, w_up_b, w_out)


def _heads_first(t, batch, seq):
    return t.reshape(batch, seq, NA_HEADS, NA_HEAD_DIM).transpose(0, 2, 1, 3)


def _tokens_first(t):
    b, h, l, dh = t.shape
    return t.transpose(0, 2, 1, 3).reshape(b * l, h * dh)


def kernel(x_prompt, x_sample, cache_k, cache_v, state_ssm, c, c_ctx, w_ada, b_ada, norm_g, ffn_in, ffn_out, w_in, rpb, s5_lam_re, s5_lam_im, s5_log_dt, s5_b_re, s5_b_im, s5_c_re, s5_c_im, s5_d, w_glu, w_up_a, w_up_b, w_out, final_g):
    depth = w_ada.shape[0]
    pb, pl_, d = x_prompt.shape
    sb, sl, _ = x_sample.shape
    past = cache_k.shape[3]
    u_col = 3 * NA_WIDTH // LANES
    mod_rows = 8 * ((1 + sb + 7) // 8)

    xp = x_prompt.reshape(pb * pl_, d)
    xs = x_sample.reshape(sb * sl, d)
    cond = jnp.concatenate([c_ctx[None, :], c, jnp.zeros((mod_rows - 1 - sb, d), F32)], axis=0)
    new_k, new_v, new_s = [], [], []
    for l in range(depth):
        last = l == depth - 1
        mods3 = _mods(cond, w_ada[l], b_ada[l]).reshape(mod_rows, N_MOD, d)
        ffn1_in, ffn1_out = ffn_in[l, 0].astype(BF16), ffn_out[l, 0].astype(BF16)
        ffn2_in, ffn2_out = ffn_in[l, 1].astype(BF16), ffn_out[l, 1].astype(BF16)
        w_in_bf = w_in[l].astype(BF16)
        w_glu_bf, w_up_a_bf, w_up_b_bf, w_out_bf = (w[l].astype(BF16) for w in (w_glu, w_up_a, w_up_b, w_out))
        e, m, f, at = _s5_prep(s5_lam_re[l], s5_lam_im[l], s5_log_dt[l], s5_b_re[l], s5_b_im[l],
                               s5_c_re[l], s5_c_im[l], sl // S5_SEGMENTS)
        bias = _na_bias_tables(rpb[l], sl // GRID_W)

        ptok = pb * pl_
        xp = _ffn(xp, mods3, 0, ptok, norm_g[l, 0], final_g, ffn1_in, ffn1_out, 0, False)
        proj_p = _proj(xp, mods3, 0, ptok, norm_g[l, 1], w_in_bf)
        ya_p, k_p, v_p = _attn_ctx(proj_p, pb, pl_)
        yb_p, fin_p = _s5(proj_p, u_col, jnp.zeros((N_GROUP_SLABS, pb, SLAB_COLS), F32), e, m, f, at,
                          s5_d[l], pb, pl_, math.gcd(pb, 16), 1)
        xp = _merge(xp, mods3, 0, ptok, ya_p, yb_p, proj_p, w_glu_bf, w_up_a_bf, w_up_b_bf, w_out_bf)
        xp = _ffn(xp, mods3, 0, ptok, norm_g[l, 2], final_g, ffn2_in, ffn2_out, 6, last)
        new_k.append(k_p)
        new_v.append(v_p)
        new_s.append(_slabs_to_state(fin_p))

        xs = _ffn(xs, mods3, 1, sl, norm_g[l, 0], final_g, ffn1_in, ffn1_out, 0, False)
        proj_s = _proj(xs, mods3, 1, sl, norm_g[l, 1], w_in_bf)
        ya_s = _attn_na(proj_s, _tokens_first(cache_k[:, l]), _tokens_first(cache_v[:, l]), bias, sb, sl, past)
        yb_s, _ = _s5(proj_s, u_col, _state_to_slabs(state_ssm[:, l]), e, m, f, at, s5_d[l], sb, sl, 1, S5_SEGMENTS)
        xs = _merge(xs, mods3, 1, sl, ya_s, yb_s, proj_s, w_glu_bf, w_up_a_bf, w_up_b_bf, w_out_bf)
        xs = _ffn(xs, mods3, 1, sl, norm_g[l, 2], final_g, ffn2_in, ffn2_out, 6, last)

    y_prompt = xp.reshape(pb, pl_, d)
    y_sample = xs.reshape(sb, sl, d)
    return (y_prompt, y_sample, jnp.stack(new_k, axis=1), jnp.stack(new_v, axis=1), jnp.stack(new_s, axis=1))
```

```python
import functools
import math

import numpy as np
import jax
import jax.numpy as jnp
from jax import lax
from jax.experimental import pallas as pl
from jax.experimental.pallas import tpu as pltpu

F32 = jnp.float32
BF16 = jnp.bfloat16

GRID_W = 64
NA_HEADS = 16
NA_HEAD_DIM = 64
NA_WIDTH = NA_HEADS * NA_HEAD_DIM
WIN_R = 8
WIN_C = 16
S5_GROUP_CH = 16
S5_WIDTH = 1024
S5_GROUPS = S5_WIDTH // S5_GROUP_CH
S5_STATE = 64
N_MOD = 9
EPS = 1e-6

LANES = 128
HEADS_PER_SLAB = LANES // NA_HEAD_DIM
N_HEAD_SLABS = NA_WIDTH // LANES
GROUPS_PER_SLAB = LANES // S5_GROUP_CH
N_GROUP_SLABS = S5_WIDTH // LANES
SLAB_STATE = GROUPS_PER_SLAB * S5_STATE
SLAB_COLS = 4 * SLAB_STATE
S5_CHUNK = 8
S5_SEGMENTS = 8
NA_QROWS = 4
NA_KROWS = NA_QROWS + WIN_R
NEG_BIAS = -1e30
LOG2E = 1.4426950408889634
VMEM_LIMIT = 56 * 1024 * 1024


def _cparams(sem, limit=VMEM_LIMIT):
    return pltpu.CompilerParams(dimension_semantics=sem, vmem_limit_bytes=limit)


def _silu(x):
    return x * jax.nn.sigmoid(x)


NORM_ROWS = 16
NORM_UNROLL = 8


def _norm_mod_store(x_ref, h_ref, g, shift, scale):
    gs = g * (1.0 + scale)

    def body(i, carry):
        r0 = pl.multiple_of(i * NORM_ROWS, NORM_ROWS)
        x = x_ref[pl.ds(r0, NORM_ROWS), :]
        r = lax.rsqrt(jnp.mean(x * x, axis=-1, keepdims=True) + EPS)
        h_ref[pl.ds(r0, NORM_ROWS), :] = ((x * r) * gs + shift).astype(BF16)
        return carry

    lax.fori_loop(0, x_ref.shape[0] // NORM_ROWS, body, 0, unroll=NORM_UNROLL)


def _mods_kernel(c_ref, w_ref, b_ref, o_ref):
    a = _silu(c_ref[...])
    o_ref[...] = jnp.dot(a, w_ref[...], preferred_element_type=F32,
                         precision=lax.Precision.HIGHEST) + b_ref[...]


def _mods(cond, w_ada, b_ada):
    rows, d = cond.shape
    n = w_ada.shape[1]
    tn = math.gcd(1024, n)
    return pl.pallas_call(
        _mods_kernel,
        grid=(n // tn,),
        in_specs=[pl.BlockSpec((rows, d), lambda j: (0, 0)),
                  pl.BlockSpec((d, tn), lambda j: (0, j)),
                  pl.BlockSpec((1, tn), lambda j: (0, j))],
        out_specs=pl.BlockSpec((rows, tn), lambda j: (0, j)),
        out_shape=jax.ShapeDtypeStruct((rows, n), F32),
        compiler_params=_cparams(("arbitrary",)),
        name="mods",
    )(cond, w_ada, b_ada.reshape(1, n))


def _ffn_kernel(x_hbm, mods_ref, ng_ref, fg_ref, wg_ref, wu_ref, wo_ref, o_ref, xs_ref, h_ref, rs_ref, sem,
                *, mod_base, final_norm):
    i, j = pl.program_id(0), pl.program_id(1)
    tm = xs_ref.shape[0]

    def x_copy(tile):
        return pltpu.make_async_copy(x_hbm.at[pl.ds(tile * tm, tm), :], xs_ref, sem)

    @pl.when((i == 0) & (j == 0))
    def _():
        x_copy(0).start()

    @pl.when(j == 0)
    def _():
        x_copy(i).wait()
        m = mods_ref[0]
        gs = ng_ref[...] * (1.0 + m[mod_base + 1:mod_base + 2])
        shift = m[mod_base:mod_base + 1]

        def body(r, carry):
            r0 = pl.multiple_of(r * NORM_ROWS, NORM_ROWS)
            x = xs_ref[pl.ds(r0, NORM_ROWS), :]
            o_ref[pl.ds(r0, NORM_ROWS), :] = x
            rs = lax.rsqrt(jnp.mean(x * x, axis=-1, keepdims=True) + EPS)
            h_ref[pl.ds(r0, NORM_ROWS), :] = ((x * rs) * gs + shift).astype(BF16)
            return carry

        lax.fori_loop(0, tm // NORM_ROWS, body, 0, unroll=NORM_UNROLL)

    @pl.when((j == 1) & (i + 1 < pl.num_programs(0)))
    def _():
        x_copy(i + 1).start()

    h = h_ref[...]
    g = jnp.dot(h, wg_ref[...], preferred_element_type=F32)
    u = jnp.dot(h, wu_ref[...], preferred_element_type=F32)
    a = (_silu(g) * u).astype(BF16)
    half_gate = 0.5 * mods_ref[0][mod_base + 2:mod_base + 3]
    o_ref[...] += half_gate * jnp.dot(a, wo_ref[...], preferred_element_type=F32)

    if final_norm:
        @pl.when(j == pl.num_programs(1) - 1)
        def _():
            def body(r, carry):
                r0 = pl.multiple_of(r * NORM_ROWS, NORM_ROWS)
                y = o_ref[pl.ds(r0, NORM_ROWS), :]
                rs = lax.rsqrt(jnp.mean(y * y, axis=-1, keepdims=True) + EPS)
                rs_ref[pl.ds(r0, NORM_ROWS), :] = jnp.broadcast_to(rs, (NORM_ROWS, LANES))
                return carry

            lax.fori_loop(0, tm // NORM_ROWS, body, 0, unroll=NORM_UNROLL)
            for c in range(o_ref.shape[1] // LANES):
                cols = slice(c * LANES, (c + 1) * LANES)
                o_ref[:, cols] = (o_ref[:, cols] * rs_ref[...]) * fg_ref[:, cols]


FFN_TM = 1024
FFN_TF = 512


def _ffn(x, mods3, row0, cond_tokens, norm_g, final_g, w_in_bf, w_out_bf, layer, which, mod_base, final_norm):
    n_tok, d = x.shape
    d_ff = w_out_bf.shape[2]
    tm = min(FFN_TM, n_tok)
    tf = min(FFN_TF, d_ff)
    assert cond_tokens % tm == 0
    tiles_per_row = cond_tokens // tm
    assert d_ff // tf >= 2
    kern = functools.partial(_ffn_kernel, mod_base=mod_base, final_norm=final_norm)
    return pl.pallas_call(
        kern,
        grid=(n_tok // tm, d_ff // tf),
        in_specs=[pl.BlockSpec(memory_space=pl.ANY),
                  pl.BlockSpec((1, N_MOD, d), lambda i, j: (row0 + i // tiles_per_row, 0, 0)),
                  pl.BlockSpec((1, d), lambda i, j: (0, 0)),
                  pl.BlockSpec((1, d), lambda i, j: (0, 0)),
                  pl.BlockSpec((None, None, d, tf), lambda i, j: (layer, which, 0, j)),
                  pl.BlockSpec((None, None, d, tf), lambda i, j: (layer, which, 0, d_ff // tf + j)),
                  pl.BlockSpec((None, None, tf, d), lambda i, j: (layer, which, j, 0))],
        out_specs=pl.BlockSpec((tm, d), lambda i, j: (i, 0)),
        out_shape=jax.ShapeDtypeStruct((n_tok, d), F32),
        scratch_shapes=[pltpu.VMEM((tm, d), F32), pltpu.VMEM((tm, d), BF16), pltpu.VMEM((tm, LANES), F32),
                        pltpu.SemaphoreType.DMA(())],
        compiler_params=_cparams(("arbitrary", "arbitrary")),
        name="ffn",
    )(x, mods3, norm_g.reshape(1, d), final_g.reshape(1, d), w_in_bf, w_in_bf, w_out_bf)


def _proj_kernel(x_ref, mods_ref, ng_ref, w_ref, o_ref, h_ref):
    @pl.when(pl.program_id(1) == 0)
    def _():
        m = mods_ref[0]
        _norm_mod_store(x_ref, h_ref, ng_ref[...], m[3:4], m[4:5])

    o_ref[...] = jnp.dot(h_ref[...], w_ref[...], preferred_element_type=F32)


def _proj(x, mods3, row0, cond_tokens, norm_g, w_bf):
    n_tok, d = x.shape
    n = w_bf.shape[1]
    tm = min(1024, n_tok)
    tn = math.gcd(1024, d)
    assert cond_tokens % tm == 0
    tiles_per_row = cond_tokens // tm
    return pl.pallas_call(
        _proj_kernel,
        grid=(n_tok // tm, n // tn),
        in_specs=[pl.BlockSpec((tm, d), lambda i, j: (i, 0)),
                  pl.BlockSpec((1, N_MOD, d), lambda i, j: (row0 + i // tiles_per_row, 0, 0)),
                  pl.BlockSpec((1, d), lambda i, j: (0, 0)),
                  pl.BlockSpec((d, tn), lambda i, j: (0, j))],
        out_specs=pl.BlockSpec((tm, tn), lambda i, j: (i, j)),
        out_shape=jax.ShapeDtypeStruct((n_tok, n), F32),
        scratch_shapes=[pltpu.VMEM((tm, d), BF16)],
        compiler_params=_cparams(("parallel", "arbitrary")),
        name="proj",
    )(x, mods3, norm_g.reshape(1, d), w_bf)


def _head_mask(shape, hh):
    lane = lax.broadcasted_iota(jnp.int32, shape, 1)
    return (lane >= hh * NA_HEAD_DIM) & (lane < (hh + 1) * NA_HEAD_DIM)


def _nt_dot(a, b):
    return lax.dot_general(a, b, (((1,), (1,)), ((), ())), preferred_element_type=F32)


Q_SCALE = NA_HEAD_DIM ** -0.5 * LOG2E
CTX_SLABS = 4
NA_SLABS = 2


def _ones_other_head(v, hh):
    return jnp.where(_head_mask(v.shape, hh), v, 1.0)


def _finish_heads(o0, o1):
    first = _head_mask(o0.shape, 0)
    num = jnp.where(first, o0, o1)
    den = pltpu.roll(jnp.where(first, o1, o0), NA_HEAD_DIM, 1)
    return num / den


def _attn_ctx_kernel(q_ref, k_ref, v_ref, o_ref, ko_ref, vo_ref):
    for sl in range(q_ref.shape[1] // LANES):
        cols = slice(sl * LANES, (sl + 1) * LANES)
        q = q_ref[:, cols] * Q_SCALE
        kf = k_ref[:, cols]
        vf = v_ref[:, cols]
        k = kf.astype(BF16)
        outs = []
        for hh in range(HEADS_PER_SLAB):
            head = slice(hh * NA_HEAD_DIM, (hh + 1) * NA_HEAD_DIM)
            ko_ref[0, sl * HEADS_PER_SLAB + hh] = kf[:, head]
            vo_ref[0, sl * HEADS_PER_SLAB + hh] = vf[:, head]
            qm = jnp.where(_head_mask(q.shape, hh), q, 0.0).astype(BF16)
            s = _nt_dot(qm, k)
            p = jnp.exp2(s - jnp.max(s, axis=-1, keepdims=True))
            outs.append(jnp.dot(p.astype(BF16), _ones_other_head(vf, hh).astype(BF16),
                                preferred_element_type=F32))
        o_ref[:, cols] = _finish_heads(*outs)


def _attn_ctx(proj, batch, seq):
    w = CTX_SLABS * LANES
    nsb = NA_WIDTH // w
    hps = CTX_SLABS * HEADS_PER_SLAB
    kv_spec = pl.BlockSpec((1, hps, seq, NA_HEAD_DIM), lambda b, s: (b, s, 0, 0))
    kv_shape = jax.ShapeDtypeStruct((batch, NA_HEADS, seq, NA_HEAD_DIM), F32)
    return pl.pallas_call(
        _attn_ctx_kernel,
        grid=(batch, nsb),
        in_specs=[pl.BlockSpec((seq, w), lambda b, s: (b, s)),
                  pl.BlockSpec((seq, w), lambda b, s: (b, nsb + s)),
                  pl.BlockSpec((seq, w), lambda b, s: (b, 2 * nsb + s))],
        out_specs=[pl.BlockSpec((seq, w), lambda b, s: (b, s)), kv_spec, kv_spec],
        out_shape=[jax.ShapeDtypeStruct((batch * seq, NA_WIDTH), F32), kv_shape, kv_shape],
        compiler_params=_cparams(("parallel", "parallel")),
        name="attn_ctx",
    )(proj, proj, proj)


def _attn_na_kernel(q_ref, k_ref, v_ref, ck_ref, cv_ref, bias_ref, o_ref,
                    kb_ref, vb_ref, ckb_ref, cvb_ref, *, rows):
    rb = pl.program_id(2)
    nsl = q_ref.shape[1] // LANES

    @pl.when(rb == 0)
    def _():
        kb_ref[...] = k_ref[...].astype(BF16)
        vb_ref[...] = v_ref[...].astype(BF16)
        ckb_ref[...] = ck_ref[...].astype(BF16)
        cvb_ref[...] = cv_ref[...].astype(BF16)

    start = jnp.clip(rb * NA_QROWS - WIN_R // 2, 0, rows - NA_KROWS)
    off = pl.multiple_of(start * GRID_W, NA_QROWS * GRID_W)
    heads = [(sl, hh) for sl in range(nsl) for hh in range(HEADS_PER_SLAB)]
    cols = [slice(sl * LANES, (sl + 1) * LANES) for sl in range(nsl)]
    scores = []
    for sl, hh in heads:
        q = q_ref[:, cols[sl]] * Q_SCALE
        qm = jnp.where(_head_mask(q.shape, hh), q, 0.0).astype(BF16)
        s_loc = _nt_dot(qm, kb_ref[pl.ds(off, NA_KROWS * GRID_W), cols[sl]]) + bias_ref[0, sl * HEADS_PER_SLAB + hh]
        s_ctx = _nt_dot(qm, ckb_ref[:, cols[sl]])
        scores.append((s_loc, s_ctx))
    probs = []
    for s_loc, s_ctx in scores:
        m = jnp.maximum(jnp.max(s_loc, axis=-1, keepdims=True), jnp.max(s_ctx, axis=-1, keepdims=True))
        p_loc = jnp.exp2(s_loc - m)
        p_ctx = jnp.exp2(s_ctx - m)
        l = jnp.sum(p_loc, axis=-1, keepdims=True) + jnp.sum(p_ctx, axis=-1, keepdims=True)
        probs.append((p_loc.astype(BF16), p_ctx.astype(BF16), l))
    outs = []
    for (sl, hh), (p_loc, p_ctx, l) in zip(heads, probs):
        o = (jnp.dot(p_loc, vb_ref[pl.ds(off, NA_KROWS * GRID_W), cols[sl]], preferred_element_type=F32)
             + jnp.dot(p_ctx, cvb_ref[:, cols[sl]], preferred_element_type=F32))
        outs.append(o / l)
    for sl in range(nsl):
        first = _head_mask(outs[0].shape, 0)
        o_ref[:, cols[sl]] = jnp.where(first, outs[sl * HEADS_PER_SLAB], outs[sl * HEADS_PER_SLAB + 1])


def _na_bias_kernel(rpb_ref, o_ref, *, rows):
    shape = (GRID_W, LANES)
    qc = lax.broadcasted_iota(jnp.int32, shape, 0)
    lane = lax.broadcasted_iota(jnp.int32, shape, 1)
    kc = lane % GRID_W
    cs = jnp.clip(qc - WIN_C // 2, 0, GRID_W - WIN_C)
    valid_c = (kc >= cs) & (kc < cs + WIN_C)
    first_row = lane < GRID_W
    r = rpb_ref[0] * LOG2E
    for p, r0 in enumerate((0, NA_QROWS, rows - NA_QROWS)):
        start = min(max(r0 - WIN_R // 2, 0), rows - NA_KROWS)
        for qr in range(NA_QROWS):
            q_row = r0 + qr
            ws = min(max(q_row - WIN_R // 2, 0), rows - WIN_R)
            for k2 in range(NA_KROWS // 2):
                halves = []
                for kk in range(2):
                    k_row = start + 2 * k2 + kk
                    if ws <= k_row < ws + WIN_R:
                        dr = k_row - q_row + WIN_R - 1
                        row = jnp.broadcast_to(r[dr:dr + 1, :], shape)
                        shift = (LANES - (WIN_C - 1) + GRID_W * kk) % LANES
                        halves.append(pltpu.roll(row, shift, 1, stride=1, stride_axis=0))
                    else:
                        halves.append(jnp.full(shape, NEG_BIAS, F32))
                tile = jnp.where(valid_c, jnp.where(first_row, halves[0], halves[1]), NEG_BIAS)
                o_ref[p, 0, qr * GRID_W:(qr + 1) * GRID_W, k2 * LANES:(k2 + 1) * LANES] = tile


def _na_bias_tables(rpb, rows):
    h, nr, nc = rpb.shape
    rpb_pad = jnp.zeros((h, 16, LANES), F32).at[:, :nr, :nc].set(rpb)
    tq, tk = NA_QROWS * GRID_W, NA_KROWS * GRID_W
    return pl.pallas_call(
        functools.partial(_na_bias_kernel, rows=rows),
        grid=(h,),
        in_specs=[pl.BlockSpec((1, 16, LANES), lambda i: (i, 0, 0))],
        out_specs=pl.BlockSpec((3, 1, tq, tk), lambda i: (0, i, 0, 0)),
        out_shape=jax.ShapeDtypeStruct((3, h, tq, tk), F32),
        compiler_params=_cparams(("parallel",)),
        name="na_bias",
    )(rpb_pad)


def _attn_na(proj, ck_tok, cv_tok, bias, batch, seq, past):
    rows = seq // GRID_W
    assert rows >= NA_KROWS and rows % NA_QROWS == 0
    nrb = rows // NA_QROWS
    tq = NA_QROWS * GRID_W

    def bias_map(b, s, r):
        return ((r > 0).astype(jnp.int32) + (r == nrb - 1).astype(jnp.int32), s, 0, 0)

    w = NA_SLABS * LANES
    nsb = NA_WIDTH // w
    return pl.pallas_call(
        functools.partial(_attn_na_kernel, rows=rows),
        grid=(batch, nsb, nrb),
        in_specs=[pl.BlockSpec((tq, w), lambda b, s, r: (b * nrb + r, s)),
                  pl.BlockSpec((seq, w), lambda b, s, r: (b, nsb + s)),
                  pl.BlockSpec((seq, w), lambda b, s, r: (b, 2 * nsb + s)),
                  pl.BlockSpec((past, w), lambda b, s, r: (b, s)),
                  pl.BlockSpec((past, w), lambda b, s, r: (b, s)),
                  pl.BlockSpec((1, NA_SLABS * HEADS_PER_SLAB, tq, NA_KROWS * GRID_W), bias_map)],
        out_specs=pl.BlockSpec((tq, w), lambda b, s, r: (b * nrb + r, s)),
        out_shape=jax.ShapeDtypeStruct((batch * seq, NA_WIDTH), F32),
        scratch_shapes=[pltpu.VMEM((seq, w), BF16), pltpu.VMEM((seq, w), BF16),
                        pltpu.VMEM((past, w), BF16), pltpu.VMEM((past, w), BF16)],
        compiler_params=_cparams(("parallel", "parallel", "arbitrary")),
        name="attn_na",
    )(proj, proj, proj, ck_tok, cv_tok, bias)


def _s5_prep_kernel(lr_ref, li_ref, ld_ref, btr_ref, bti_ref, cr_ref, ci_ref,
                    e_ref, m_ref, f_ref, at_ref, *, seg_tokens):
    T = S5_CHUNK
    ns = SLAB_STATE
    rb = lax.broadcasted_iota(jnp.int32, (LANES, ns), 0) // S5_GROUP_CH
    cb = lax.broadcasted_iota(jnp.int32, (LANES, ns), 1) // S5_STATE
    mask_b = rb == cb

    def powers(ldr, ldi, n):
        mag = jnp.exp(ldr * float(n))
        return mag * jnp.cos(ldi * float(n)), mag * jnp.sin(ldi * float(n))

    kt = [[None] * T, [None] * T]
    for d in range(2):
        lam_re, lam_im = lr_ref[0, d], li_ref[0, d]
        dt = jnp.exp(ld_ref[0, d])
        ldr, ldi = lam_re * dt, lam_im * dt
        a_re, a_im = powers(ldr, ldi, 1)
        mag2 = lam_re * lam_re + lam_im * lam_im
        f_re = ((a_re - 1.0) * lam_re + a_im * lam_im) / mag2
        f_im = (a_im * lam_re - (a_re - 1.0) * lam_im) / mag2
        btr = jnp.where(mask_b, btr_ref[0, d], 0.0)
        bti = jnp.where(mask_b, bti_ref[0, d], 0.0)
        cr = jnp.where(mask_b, cr_ref[0, d], 0.0)
        ci = jnp.where(mask_b, ci_ref[0, d], 0.0)
        ctr, cti = cr.T, ci.T
        c0 = d * 2 * ns
        e_res, e_ims = [], []
        for n in range(T):
            an_re, an_im = powers(ldr, ldi, n)
            w_re = f_re * an_re - f_im * an_im
            w_im = f_re * an_im + f_im * an_re
            e_re = btr * w_re - bti * w_im
            e_im = btr * w_im + bti * w_re
            t = T - 1 - n if d == 0 else n
            e_ref[0, t * LANES:(t + 1) * LANES, c0:c0 + ns] = e_re.astype(BF16)
            e_ref[0, t * LANES:(t + 1) * LANES, c0 + ns:c0 + 2 * ns] = e_im.astype(BF16)
            e_res.append(e_re)
            e_ims.append(e_im)
        k_all = (jnp.dot(jnp.concatenate(e_res, axis=0), ctr, preferred_element_type=F32,
                         precision=lax.Precision.HIGHEST)
                 - jnp.dot(jnp.concatenate(e_ims, axis=0), cti, preferred_element_type=F32,
                           precision=lax.Precision.HIGHEST))
        for n in range(T):
            kt[d][n] = k_all[n * LANES:(n + 1) * LANES]
        for tp in range(T):
            n = tp + 1 if d == 0 else T - tp
            an_re, an_im = powers(ldr, ldi, n)
            f_ref[0, c0:c0 + ns, tp * LANES:(tp + 1) * LANES] = (cr * an_re - ci * an_im).T.astype(BF16)
            f_ref[0, c0 + ns:c0 + 2 * ns, tp * LANES:(tp + 1) * LANES] = (-(cr * an_im + ci * an_re)).T.astype(BF16)
        at_re, at_im = powers(ldr, ldi, T)
        at_ref[0, 2 * d:2 * d + 1, :] = at_re
        at_ref[0, 2 * d + 1:2 * d + 2, :] = at_im
        as_re, as_im = powers(ldr, ldi, seg_tokens)
        at_ref[0, 4 + 2 * d:5 + 2 * d, :] = as_re
        at_ref[0, 5 + 2 * d:6 + 2 * d, :] = as_im
    for t in range(T):
        for tp in range(T):
            if tp > t:
                blk = kt[0][tp - t]
            elif tp < t:
                blk = kt[1][t - tp]
            else:
                blk = kt[0][0] + kt[1][0]
            m_ref[0, t * LANES:(t + 1) * LANES, tp * LANES:(tp + 1) * LANES] = blk.astype(BF16)


def _s5_prep(lam_re, lam_im, log_dt, b_re, b_im, c_re, c_im, seg_tokens):
    T, S, G8, ns = S5_CHUNK, N_GROUP_SLABS, GROUPS_PER_SLAB, SLAB_STATE

    def state_vec(v):
        return v.reshape(2, S, ns).transpose(1, 0, 2)

    lr, li = state_vec(lam_re), state_vec(lam_im)
    ld = state_vec(jnp.broadcast_to(log_dt[:, :, None], lam_re.shape))

    def bt(v):
        v = v.reshape(2, S, G8, S5_STATE, S5_GROUP_CH).transpose(1, 0, 2, 4, 3).reshape(S, 2, LANES, S5_STATE)
        return jnp.tile(v, (1, 1, 1, G8))

    def crow(v):
        v = v.reshape(2, S, G8, S5_GROUP_CH, S5_STATE).transpose(1, 0, 2, 3, 4).reshape(S, 2, LANES, S5_STATE)
        return jnp.tile(v, (1, 1, 1, G8))

    row = lambda v: v[:, :, None, :]
    spec_row = pl.BlockSpec((1, 2, 1, ns), lambda j: (j, 0, 0, 0))
    spec_bt = pl.BlockSpec((1, 2, LANES, ns), lambda j: (j, 0, 0, 0))
    return pl.pallas_call(
        functools.partial(_s5_prep_kernel, seg_tokens=seg_tokens),
        grid=(S,),
        in_specs=[spec_row, spec_row, spec_row, spec_bt, spec_bt, spec_bt, spec_bt],
        out_specs=[pl.BlockSpec((1, T * LANES, SLAB_COLS), lambda j: (j, 0, 0)),
                   pl.BlockSpec((1, T * LANES, T * LANES), lambda j: (j, 0, 0)),
                   pl.BlockSpec((1, SLAB_COLS, T * LANES), lambda j: (j, 0, 0)),
                   pl.BlockSpec((1, 8, ns), lambda j: (j, 0, 0))],
        out_shape=[jax.ShapeDtypeStruct((S, T * LANES, SLAB_COLS), BF16),
                   jax.ShapeDtypeStruct((S, T * LANES, T * LANES), BF16),
                   jax.ShapeDtypeStruct((S, SLAB_COLS, T * LANES), BF16),
                   jax.ShapeDtypeStruct((S, 8, ns), F32)],
        compiler_params=_cparams(("parallel",)),
        name="s5_prep",
    )(row(lr), row(li), row(ld), bt(b_re), bt(b_im), crow(c_re), crow(c_im))


def _s5_kernel(u_ref, h0_ref, e_ref, m_ref, f_ref, at_ref, d_ref, y_ref, fin_ref,
               us_ref, ubf_ref, st_ref, ys_ref, *, bt, nseg, seq):
    T = S5_CHUNK
    ns = SLAB_STATE
    npar = bt * nseg
    seg = seq // nseg
    nc = seg // T
    for p in range(npar):
        for t in range(T):
            us_ref[t, pl.ds(p, nc, stride=npar), :] = u_ref[pl.ds(p * seg + t, nc, stride=T), :]
    for t in range(T):
        ubf_ref[:, t * LANES:(t + 1) * LANES] = us_ref[t].astype(BF16)
    ubf = ubf_ref[...]
    st_ref[...] = jnp.dot(ubf, e_ref[0], preferred_element_type=F32)

    at = at_ref[0]
    a_fr, a_fi, a_br, a_bi = at[0:1], at[1:2], at[2:3], at[3:4]

    def scan(init, keep_entering):
        def step(i, carry):
            c_fr, c_fi, c_br, c_bi = carry
            rf = pl.multiple_of(i * npar, npar)
            rbk = pl.multiple_of((nc - 1 - i) * npar, npar)
            l_fr = st_ref[pl.ds(rf, npar), 0:ns]
            l_fi = st_ref[pl.ds(rf, npar), ns:2 * ns]
            l_br = st_ref[pl.ds(rbk, npar), 2 * ns:3 * ns]
            l_bi = st_ref[pl.ds(rbk, npar), 3 * ns:4 * ns]
            if keep_entering:
                st_ref[pl.ds(rf, npar), 0:ns] = c_fr
                st_ref[pl.ds(rf, npar), ns:2 * ns] = c_fi
                st_ref[pl.ds(rbk, npar), 2 * ns:3 * ns] = c_br
                st_ref[pl.ds(rbk, npar), 3 * ns:4 * ns] = c_bi
            return (a_fr * c_fr - a_fi * c_fi + l_fr, a_fr * c_fi + a_fi * c_fr + l_fi,
                    a_br * c_br - a_bi * c_bi + l_br, a_br * c_bi + a_bi * c_br + l_bi)

        return lax.fori_loop(0, nc, step, init)

    h0 = h0_ref[0, 0]
    h0 = [h0[:, k * ns:(k + 1) * ns] for k in range(4)]
    if nseg == 1:
        enter = tuple(h0)
    else:
        zero = jnp.zeros((npar, ns), F32)
        z_fr, z_fi, z_br, z_bi = scan((zero, zero, zero, zero), False)
        s_fr, s_fi, s_br, s_bi = at[4:5], at[5:6], at[6:7], at[7:8]
        rows_f, rows_b = [None] * npar, [None] * npar
        for b in range(bt):
            c_re, c_im = h0[0][b:b + 1], h0[1][b:b + 1]
            for s in range(nseg):
                p = b * nseg + s
                rows_f[p] = (c_re, c_im)
                c_re, c_im = (s_fr * c_re - s_fi * c_im + z_fr[p:p + 1], s_fr * c_im + s_fi * c_re + z_fi[p:p + 1])
            c_re, c_im = h0[2][b:b + 1], h0[3][b:b + 1]
            for s in reversed(range(nseg)):
                p = b * nseg + s
                rows_b[p] = (c_re, c_im)
                c_re, c_im = (s_br * c_re - s_bi * c_im + z_br[p:p + 1], s_br * c_im + s_bi * c_re + z_bi[p:p + 1])
        enter = (jnp.concatenate([r[0] for r in rows_f], axis=0), jnp.concatenate([r[1] for r in rows_f], axis=0),
                 jnp.concatenate([r[0] for r in rows_b], axis=0), jnp.concatenate([r[1] for r in rows_b], axis=0))
    fin = scan(enter, True)
    for b in range(bt):
        last_f, last_b = b * nseg + nseg - 1, b * nseg
        fin_ref[0, 0, b:b + 1, 0:ns] = fin[0][last_f:last_f + 1]
        fin_ref[0, 0, b:b + 1, ns:2 * ns] = fin[1][last_f:last_f + 1]
        fin_ref[0, 0, b:b + 1, 2 * ns:3 * ns] = fin[2][last_b:last_b + 1]
        fin_ref[0, 0, b:b + 1, 3 * ns:4 * ns] = fin[3][last_b:last_b + 1]

    y = (jnp.dot(ubf, m_ref[0], preferred_element_type=F32)
         + jnp.dot(st_ref[...].astype(BF16), f_ref[0], preferred_element_type=F32))
    for t in range(T):
        ys_ref[t] = y[:, t * LANES:(t + 1) * LANES] + d_ref[0] * us_ref[t]
    for p in range(npar):
        for t in range(T):
            y_ref[pl.ds(p * seg + t, nc, stride=T), :] = ys_ref[t, pl.ds(p, nc, stride=npar), :]


def _s5(proj, u_col, h0, e, m, f, at, d_skip, batch, seq, bt, nseg):
    T, S = S5_CHUNK, N_GROUP_SLABS
    rows = bt * seq // T
    const = lambda j, g: (j, 0, 0)
    y, fin = pl.pallas_call(
        functools.partial(_s5_kernel, bt=bt, nseg=nseg, seq=seq),
        grid=(S, batch // bt),
        in_specs=[pl.BlockSpec((bt * seq, LANES), lambda j, g: (g, u_col + j)),
                  pl.BlockSpec((1, 1, bt, SLAB_COLS), lambda j, g: (j, g, 0, 0)),
                  pl.BlockSpec((1, T * LANES, SLAB_COLS), const),
                  pl.BlockSpec((1, T * LANES, T * LANES), const),
                  pl.BlockSpec((1, SLAB_COLS, T * LANES), const),
                  pl.BlockSpec((1, 8, SLAB_STATE), const),
                  pl.BlockSpec((1, 1, LANES), const)],
        out_specs=[pl.BlockSpec((bt * seq, LANES), lambda j, g: (g, j)),
                   pl.BlockSpec((1, 1, bt, SLAB_COLS), lambda j, g: (j, g, 0, 0))],
        out_shape=[jax.ShapeDtypeStruct((batch * seq, S5_WIDTH), F32),
                   jax.ShapeDtypeStruct((S, batch // bt, bt, SLAB_COLS), F32)],
        scratch_shapes=[pltpu.VMEM((T, rows, LANES), F32), pltpu.VMEM((rows, T * LANES), BF16),
                        pltpu.VMEM((rows, SLAB_COLS), F32), pltpu.VMEM((T, rows, LANES), F32)],
        compiler_params=_cparams(("parallel", "arbitrary")),
        name="s5",
    )(proj, h0.reshape(S, batch // bt, bt, SLAB_COLS), e, m, f, at, d_skip.reshape(S, 1, LANES))
    return y, fin.reshape(S, batch, SLAB_COLS)


def _state_to_slabs(state):
    b = state.shape[0]
    s = state.reshape(b, 2, 2, N_GROUP_SLABS, SLAB_STATE).transpose(3, 0, 1, 2, 4)
    return s.reshape(N_GROUP_SLABS, b, SLAB_COLS)


def _slabs_to_state(slabs):
    b = slabs.shape[1]
    s = slabs.reshape(N_GROUP_SLABS, b, 2, 2, SLAB_STATE).transpose(1, 2, 3, 0, 4)
    return s.reshape(b, 2, 2, S5_GROUPS, S5_STATE)


def _merge_kernel(x_ref, mods_ref, ya_ref, yb_ref, ga_ref, gb_ref, wglu_ref, wua_ref, wub_ref, wo_ref, o_ref):
    yb = jax.nn.gelu(yb_ref[...])
    glu = jnp.dot(yb.astype(BF16), wglu_ref[...], preferred_element_type=F32)
    yb = yb * jax.nn.sigmoid(glu)
    a = jnp.dot(ya_ref[...].astype(BF16), wua_ref[...], preferred_element_type=F32)
    b = jnp.dot(yb.astype(BF16), wub_ref[...], preferred_element_type=F32)
    merged = jax.nn.sigmoid(ga_ref[...]) * a + jax.nn.sigmoid(gb_ref[...]) * b
    o = jnp.dot(merged.astype(BF16), wo_ref[...], preferred_element_type=F32)
    o_ref[...] = x_ref[...] + mods_ref[0][5:6] * o


def _merge(x, mods3, row0, cond_tokens, ya, yb, proj, w_glu, w_up_a, w_up_b, w_out):
    n_tok, d = x.shape
    tm = min(256, n_tok)
    assert cond_tokens % tm == 0
    tiles_per_row = cond_tokens // tm
    gate_col = (3 * NA_WIDTH + S5_WIDTH) // d
    assert gate_col * d == 3 * NA_WIDTH + S5_WIDTH
    resident = functools.partial(pl.BlockSpec, pipeline_mode=pl.Buffered(1))
    return pl.pallas_call(
        _merge_kernel,
        grid=(n_tok // tm,),
        in_specs=[pl.BlockSpec((tm, d), lambda i: (i, 0)),
                  pl.BlockSpec((1, N_MOD, d), lambda i: (row0 + i // tiles_per_row, 0, 0)),
                  pl.BlockSpec((tm, NA_WIDTH), lambda i: (i, 0)),
                  pl.BlockSpec((tm, S5_WIDTH), lambda i: (i, 0)),
                  pl.BlockSpec((tm, d), lambda i: (i, gate_col)),
                  pl.BlockSpec((tm, d), lambda i: (i, gate_col + 1)),
                  resident((S5_WIDTH, S5_WIDTH), lambda i: (0, 0)),
                  resident((NA_WIDTH, d), lambda i: (0, 0)),
                  resident((S5_WIDTH, d), lambda i: (0, 0)),
                  resident((d, d), lambda i: (0, 0))],
        out_specs=pl.BlockSpec((tm, d), lambda i: (i, 0)),
        out_shape=jax.ShapeDtypeStruct((n_tok, d), F32),
        compiler_params=_cparams(("parallel",)),
        name="merge",
    )(x, mods3, ya, yb, proj, proj, w_glu, w_up_a, w_up_b, w_out)


def _heads_first(t, batch, seq):
    return t.reshape(batch, seq, NA_HEADS, NA_HEAD_DIM).transpose(0, 2, 1, 3)


def _tokens_first(t):
    b, h, l, dh = t.shape
    return t.transpose(0, 2, 1, 3).reshape(b * l, h * dh)


def kernel(x_prompt, x_sample, cache_k, cache_v, state_ssm, c, c_ctx, w_ada, b_ada, norm_g, ffn_in, ffn_out, w_in, rpb, s5_lam_re, s5_lam_im, s5_log_dt, s5_b_re, s5_b_im, s5_c_re, s5_c_im, s5_d, w_glu, w_up_a, w_up_b, w_out, final_g):
    depth = w_ada.shape[0]
    pb, pl_, d = x_prompt.shape
    sb, sl, _ = x_sample.shape
    past = cache_k.shape[3]
    u_col = 3 * NA_WIDTH // LANES
    mod_rows = 8 * ((1 + sb + 7) // 8)

    xp = x_prompt.reshape(pb * pl_, d)
    xs = x_sample.reshape(sb * sl, d)
    cond = jnp.concatenate([c_ctx[None, :], c, jnp.zeros((mod_rows - 1 - sb, d), F32)], axis=0)
    ffn_in_bf, ffn_out_bf = ffn_in.astype(BF16), ffn_out.astype(BF16)
    new_k, new_v, new_s = [], [], []
    for l in range(depth):
        last = l == depth - 1
        mods3 = _mods(cond, w_ada[l], b_ada[l]).reshape(mod_rows, N_MOD, d)
        w_in_bf = w_in[l].astype(BF16)
        w_glu_bf, w_up_a_bf, w_up_b_bf, w_out_bf = (w[l].astype(BF16) for w in (w_glu, w_up_a, w_up_b, w_out))
        e, m, f, at = _s5_prep(s5_lam_re[l], s5_lam_im[l], s5_log_dt[l], s5_b_re[l], s5_b_im[l],
                               s5_c_re[l], s5_c_im[l], sl // S5_SEGMENTS)
        bias = _na_bias_tables(rpb[l], sl // GRID_W)

        ptok = pb * pl_
        xp = _ffn(xp, mods3, 0, ptok, norm_g[l, 0], final_g, ffn_in_bf, ffn_out_bf, l, 0, 0, False)
        proj_p = _proj(xp, mods3, 0, ptok, norm_g[l, 1], w_in_bf)
        ya_p, k_p, v_p = _attn_ctx(proj_p, pb, pl_)
        yb_p, fin_p = _s5(proj_p, u_col, jnp.zeros((N_GROUP_SLABS, pb, SLAB_COLS), F32), e, m, f, at,
                          s5_d[l], pb, pl_, math.gcd(pb, 16), 1)
        xp = _merge(xp, mods3, 0, ptok, ya_p, yb_p, proj_p, w_glu_bf, w_up_a_bf, w_up_b_bf, w_out_bf)
        xp = _ffn(xp, mods3, 0, ptok, norm_g[l, 2], final_g, ffn_in_bf, ffn_out_bf, l, 1, 6, last)
        new_k.append(k_p)
        new_v.append(v_p)
        new_s.append(_slabs_to_state(fin_p))

        xs = _ffn(xs, mods3, 1, sl, norm_g[l, 0], final_g, ffn_in_bf, ffn_out_bf, l, 0, 0, False)
        proj_s = _proj(xs, mods3, 1, sl, norm_g[l, 1], w_in_bf)
        ya_s = _attn_na(proj_s, _tokens_first(cache_k[:, l]), _tokens_first(cache_v[:, l]), bias, sb, sl, past)
        yb_s, _ = _s5(proj_s, u_col, _state_to_slabs(state_ssm[:, l]), e, m, f, at, s5_d[l], sb, sl, 1, S5_SEGMENTS)
        xs = _merge(xs, mods3, 1, sl, ya_s, yb_s, proj_s, w_glu_bf, w_up_a_bf, w_up_b_bf, w_out_bf)
        xs = _ffn(xs, mods3, 1, sl, norm_g[l, 2], final_g, ffn_in_bf, ffn_out_bf, l, 1, 6, last)

    y_prompt = xp.reshape(pb, pl_, d)
    y_sample = xs.reshape(sb, sl, d)
    return (y_prompt, y_sample, jnp.stack(new_k, axis=1), jnp.stack(new_v, axis=1), jnp.stack(new_s, axis=1))
```

```python
import functools
import math

import jax
import jax.numpy as jnp
from jax import lax
from jax.experimental import pallas as pl
from jax.experimental.pallas import tpu as pltpu

F32 = jnp.float32
BF16 = jnp.bfloat16

GRID_W = 64
NA_HEADS = 16
NA_HEAD_DIM = 64
NA_WIDTH = NA_HEADS * NA_HEAD_DIM
WIN_R = 8
WIN_C = 16
S5_GROUP_CH = 16
S5_WIDTH = 1024
S5_GROUPS = S5_WIDTH // S5_GROUP_CH
S5_STATE = 64
N_MOD = 9
EPS = 1e-6

LANES = 128
HEADS_PER_SLAB = LANES // NA_HEAD_DIM
N_HEAD_SLABS = NA_WIDTH // LANES
GROUPS_PER_SLAB = LANES // S5_GROUP_CH
N_GROUP_SLABS = S5_WIDTH // LANES
SLAB_STATE = GROUPS_PER_SLAB * S5_STATE
SLAB_COLS = 4 * SLAB_STATE
S5_CHUNK = 8
S5_SEGMENTS = 8
NA_QROWS = 4
NA_KROWS = NA_QROWS + WIN_R
NEG_BIAS = -1e30
LOG2E = 1.4426950408889634
VMEM_LIMIT = 56 * 1024 * 1024


def _cparams(sem, limit=VMEM_LIMIT):
    return pltpu.CompilerParams(dimension_semantics=sem, vmem_limit_bytes=limit)


def _silu(x):
    return x * jax.nn.sigmoid(x)


NORM_ROWS = 16
NORM_UNROLL = 8


def _norm_mod_store(x_ref, h_ref, g, shift, scale):
    gs = g * (1.0 + scale)

    def body(i, carry):
        r0 = pl.multiple_of(i * NORM_ROWS, NORM_ROWS)
        x = x_ref[pl.ds(r0, NORM_ROWS), :]
        r = lax.rsqrt(jnp.mean(x * x, axis=-1, keepdims=True) + EPS)
        h_ref[pl.ds(r0, NORM_ROWS), :] = ((x * r) * gs + shift).astype(BF16)
        return carry

    lax.fori_loop(0, x_ref.shape[0] // NORM_ROWS, body, 0, unroll=NORM_UNROLL)


def _split_bf16(x):
    hi = x.astype(BF16)
    return hi, (x - hi.astype(F32)).astype(BF16)


def _dot3(a, b):
    a_hi, a_lo = _split_bf16(a)
    b_hi, b_lo = _split_bf16(b)
    return (jnp.dot(a_hi, b_hi, preferred_element_type=F32)
            + (jnp.dot(a_lo, b_hi, preferred_element_type=F32) + jnp.dot(a_hi, b_lo, preferred_element_type=F32)))


def _mods_kernel(c_ref, w_ref, b_ref, o_ref):
    o_ref[...] = _dot3(_silu(c_ref[...]), w_ref[...]) + b_ref[...]


def _mods(cond, w_ada, b_ada):
    rows, d = cond.shape
    n = w_ada.shape[1]
    tn = math.gcd(1024, n)
    return pl.pallas_call(
        _mods_kernel,
        grid=(n // tn,),
        in_specs=[pl.BlockSpec((rows, d), lambda j: (0, 0)),
                  pl.BlockSpec((d, tn), lambda j: (0, j)),
                  pl.BlockSpec((1, tn), lambda j: (0, j))],
        out_specs=pl.BlockSpec((rows, tn), lambda j: (0, j)),
        out_shape=jax.ShapeDtypeStruct((rows, n), F32),
        compiler_params=_cparams(("arbitrary",)),
        name="mods",
    )(cond, w_ada, b_ada.reshape(1, n))


def _ffn_kernel(x_hbm, mods_ref, ng_ref, fg_ref, wg_ref, wu_ref, wo_ref, o_ref, xs_ref, h_ref, rs_ref, sem,
                *, mod_base, final_norm):
    i, j = pl.program_id(0), pl.program_id(1)
    tm = xs_ref.shape[0]

    def x_copy(tile):
        return pltpu.make_async_copy(x_hbm.at[pl.ds(tile * tm, tm), :], xs_ref, sem)

    @pl.when((i == 0) & (j == 0))
    def _():
        x_copy(0).start()

    @pl.when(j == 0)
    def _():
        x_copy(i).wait()
        m = mods_ref[0]
        gs = ng_ref[...] * (1.0 + m[mod_base + 1:mod_base + 2])
        shift = m[mod_base:mod_base + 1]

        def body(r, carry):
            r0 = pl.multiple_of(r * NORM_ROWS, NORM_ROWS)
            x = xs_ref[pl.ds(r0, NORM_ROWS), :]
            o_ref[pl.ds(r0, NORM_ROWS), :] = x
            rs = lax.rsqrt(jnp.mean(x * x, axis=-1, keepdims=True) + EPS)
            h_ref[pl.ds(r0, NORM_ROWS), :] = ((x * rs) * gs + shift).astype(BF16)
            return carry

        lax.fori_loop(0, tm // NORM_ROWS, body, 0, unroll=NORM_UNROLL)

    @pl.when((j == 1) & (i + 1 < pl.num_programs(0)))
    def _():
        x_copy(i + 1).start()

    h = h_ref[...]
    g = jnp.dot(h, wg_ref[...], preferred_element_type=F32)
    u = jnp.dot(h, wu_ref[...], preferred_element_type=F32)
    a = (_silu(g) * u).astype(BF16)
    half_gate = 0.5 * mods_ref[0][mod_base + 2:mod_base + 3]
    o_ref[...] += half_gate * jnp.dot(a, wo_ref[...], preferred_element_type=F32)

    if final_norm:
        @pl.when(j == pl.num_programs(1) - 1)
        def _():
            def body(r, carry):
                r0 = pl.multiple_of(r * NORM_ROWS, NORM_ROWS)
                y = o_ref[pl.ds(r0, NORM_ROWS), :]
                rs = lax.rsqrt(jnp.mean(y * y, axis=-1, keepdims=True) + EPS)
                rs_ref[pl.ds(r0, NORM_ROWS), :] = jnp.broadcast_to(rs, (NORM_ROWS, LANES))
                return carry

            lax.fori_loop(0, tm // NORM_ROWS, body, 0, unroll=NORM_UNROLL)
            for c in range(o_ref.shape[1] // LANES):
                cols = slice(c * LANES, (c + 1) * LANES)
                o_ref[:, cols] = (o_ref[:, cols] * rs_ref[...]) * fg_ref[:, cols]


FFN_TM = 1024
FFN_TF = 512


def _ffn(x, mods3, row0, cond_tokens, norm_g, final_g, w_in_bf, w_out_bf, layer, which, mod_base, final_norm):
    n_tok, d = x.shape
    d_ff = w_out_bf.shape[2]
    tm = min(FFN_TM, n_tok)
    tf = min(FFN_TF, d_ff)
    assert cond_tokens % tm == 0
    tiles_per_row = cond_tokens // tm
    assert d_ff // tf >= 2
    kern = functools.partial(_ffn_kernel, mod_base=mod_base, final_norm=final_norm)
    return pl.pallas_call(
        kern,
        grid=(n_tok // tm, d_ff // tf),
        in_specs=[pl.BlockSpec(memory_space=pl.ANY),
                  pl.BlockSpec((1, N_MOD, d), lambda i, j: (row0 + i // tiles_per_row, 0, 0)),
                  pl.BlockSpec((1, d), lambda i, j: (0, 0)),
                  pl.BlockSpec((1, d), lambda i, j: (0, 0)),
                  pl.BlockSpec((None, None, d, tf), lambda i, j: (layer, which, 0, j)),
                  pl.BlockSpec((None, None, d, tf), lambda i, j: (layer, which, 0, d_ff // tf + j)),
                  pl.BlockSpec((None, None, tf, d), lambda i, j: (layer, which, j, 0))],
        out_specs=pl.BlockSpec((tm, d), lambda i, j: (i, 0)),
        out_shape=jax.ShapeDtypeStruct((n_tok, d), F32),
        scratch_shapes=[pltpu.VMEM((tm, d), F32), pltpu.VMEM((tm, d), BF16), pltpu.VMEM((tm, LANES), F32),
                        pltpu.SemaphoreType.DMA(())],
        compiler_params=_cparams(("arbitrary", "arbitrary")),
        name="ffn",
    )(x, mods3, norm_g.reshape(1, d), final_g.reshape(1, d), w_in_bf, w_in_bf, w_out_bf)


def _proj_kernel(x_hbm, mods_ref, ng_ref, w_ref, o_ref, xs_ref, h_ref, sem):
    i, j = pl.program_id(0), pl.program_id(1)
    tm = xs_ref.shape[0]

    def x_copy(tile):
        return pltpu.make_async_copy(x_hbm.at[pl.ds(tile * tm, tm), :], xs_ref, sem)

    @pl.when((i == 0) & (j == 0))
    def _():
        x_copy(0).start()

    @pl.when(j == 0)
    def _():
        x_copy(i).wait()
        m = mods_ref[0]
        _norm_mod_store(xs_ref, h_ref, ng_ref[...], m[3:4], m[4:5])

    @pl.when((j == 1) & (i + 1 < pl.num_programs(0)))
    def _():
        x_copy(i + 1).start()

    o_ref[...] = jnp.dot(h_ref[...], w_ref[...], preferred_element_type=F32)


def _proj(x, mods3, row0, cond_tokens, norm_g, w_bf):
    n_tok, d = x.shape
    n = w_bf.shape[1]
    tm = min(1024, n_tok)
    tn = math.gcd(2048, d)
    assert cond_tokens % tm == 0 and n // tn >= 2
    tiles_per_row = cond_tokens // tm
    return pl.pallas_call(
        _proj_kernel,
        grid=(n_tok // tm, n // tn),
        in_specs=[pl.BlockSpec(memory_space=pl.ANY),
                  pl.BlockSpec((1, N_MOD, d), lambda i, j: (row0 + i // tiles_per_row, 0, 0)),
                  pl.BlockSpec((1, d), lambda i, j: (0, 0)),
                  pl.BlockSpec((d, tn), lambda i, j: (0, j))],
        out_specs=pl.BlockSpec((tm, tn), lambda i, j: (i, j)),
        out_shape=jax.ShapeDtypeStruct((n_tok, n), F32),
        scratch_shapes=[pltpu.VMEM((tm, d), F32), pltpu.VMEM((tm, d), BF16), pltpu.SemaphoreType.DMA(())],
        compiler_params=_cparams(("arbitrary", "arbitrary")),
        name="proj",
    )(x, mods3, norm_g.reshape(1, d), w_bf)


def _head_mask(shape, hh):
    lane = lax.broadcasted_iota(jnp.int32, shape, 1)
    return (lane >= hh * NA_HEAD_DIM) & (lane < (hh + 1) * NA_HEAD_DIM)


def _nt_dot(a, b):
    return lax.dot_general(a, b, (((1,), (1,)), ((), ())), preferred_element_type=F32)


Q_SCALE = NA_HEAD_DIM ** -0.5 * LOG2E
CTX_SLABS = 4
NA_SLABS = 2


def _ones_other_head(v, hh):
    return jnp.where(_head_mask(v.shape, hh), v, 1.0)


def _finish_heads(o0, o1):
    first = _head_mask(o0.shape, 0)
    num = jnp.where(first, o0, o1)
    den = pltpu.roll(jnp.where(first, o1, o0), NA_HEAD_DIM, 1)
    return num / den


def _attn_ctx_kernel(q_ref, k_ref, v_ref, o_ref, ko_ref, vo_ref):
    for sl in range(q_ref.shape[1] // LANES):
        cols = slice(sl * LANES, (sl + 1) * LANES)
        q = q_ref[:, cols] * Q_SCALE
        kf = k_ref[:, cols]
        vf = v_ref[:, cols]
        k = kf.astype(BF16)
        outs = []
        for hh in range(HEADS_PER_SLAB):
            head = slice(hh * NA_HEAD_DIM, (hh + 1) * NA_HEAD_DIM)
            ko_ref[0, sl * HEADS_PER_SLAB + hh] = kf[:, head]
            vo_ref[0, sl * HEADS_PER_SLAB + hh] = vf[:, head]
            qm = jnp.where(_head_mask(q.shape, hh), q, 0.0).astype(BF16)
            s = _nt_dot(qm, k)
            p = jnp.exp2(s - jnp.max(s, axis=-1, keepdims=True))
            outs.append(jnp.dot(p.astype(BF16), _ones_other_head(vf, hh).astype(BF16),
                                preferred_element_type=F32))
        o_ref[:, cols] = _finish_heads(*outs)


def _attn_ctx(proj, batch, seq):
    w = CTX_SLABS * LANES
    nsb = NA_WIDTH // w
    hps = CTX_SLABS * HEADS_PER_SLAB
    kv_spec = pl.BlockSpec((1, hps, seq, NA_HEAD_DIM), lambda b, s: (b, s, 0, 0))
    kv_shape = jax.ShapeDtypeStruct((batch, NA_HEADS, seq, NA_HEAD_DIM), F32)
    return pl.pallas_call(
        _attn_ctx_kernel,
        grid=(batch, nsb),
        in_specs=[pl.BlockSpec((seq, w), lambda b, s: (b, s)),
                  pl.BlockSpec((seq, w), lambda b, s: (b, nsb + s)),
                  pl.BlockSpec((seq, w), lambda b, s: (b, 2 * nsb + s))],
        out_specs=[pl.BlockSpec((seq, w), lambda b, s: (b, s)), kv_spec, kv_spec],
        out_shape=[jax.ShapeDtypeStruct((batch * seq, NA_WIDTH), F32), kv_shape, kv_shape],
        compiler_params=_cparams(("parallel", "parallel")),
        name="attn_ctx",
    )(proj, proj, proj)


def _attn_na_kernel(q_ref, k_ref, v_ref, ck_ref, cv_ref, bias_ref, o_ref,
                    kb_ref, vb_ref, ckb_ref, cvb_ref, *, rows):
    rb = pl.program_id(2)
    nsl = q_ref.shape[1] // LANES

    @pl.when(rb == 0)
    def _():
        kb_ref[...] = k_ref[...].astype(BF16)
        vb_ref[...] = v_ref[...].astype(BF16)
        for hd in range(ck_ref.shape[0]):
            lanes = slice(hd * NA_HEAD_DIM, (hd + 1) * NA_HEAD_DIM)
            ckb_ref[:, lanes] = ck_ref[hd].astype(BF16)
            cvb_ref[:, lanes] = cv_ref[hd].astype(BF16)

    start = jnp.clip(rb * NA_QROWS - WIN_R // 2, 0, rows - NA_KROWS)
    off = pl.multiple_of(start * GRID_W, NA_QROWS * GRID_W)
    heads = [(sl, hh) for sl in range(nsl) for hh in range(HEADS_PER_SLAB)]
    cols = [slice(sl * LANES, (sl + 1) * LANES) for sl in range(nsl)]
    scores = []
    for sl, hh in heads:
        q = q_ref[:, cols[sl]] * Q_SCALE
        qm = jnp.where(_head_mask(q.shape, hh), q, 0.0).astype(BF16)
        s_loc = _nt_dot(qm, kb_ref[pl.ds(off, NA_KROWS * GRID_W), cols[sl]]) + bias_ref[0, sl * HEADS_PER_SLAB + hh]
        s_ctx = _nt_dot(qm, ckb_ref[:, cols[sl]])
        scores.append((s_loc, s_ctx))
    probs = []
    for s_loc, s_ctx in scores:
        m = jnp.maximum(jnp.max(s_loc, axis=-1, keepdims=True), jnp.max(s_ctx, axis=-1, keepdims=True))
        p_loc = jnp.exp2(s_loc - m)
        p_ctx = jnp.exp2(s_ctx - m)
        l = jnp.sum(p_loc, axis=-1, keepdims=True) + jnp.sum(p_ctx, axis=-1, keepdims=True)
        probs.append((p_loc.astype(BF16), p_ctx.astype(BF16), l))
    outs = []
    for (sl, hh), (p_loc, p_ctx, l) in zip(heads, probs):
        o = (jnp.dot(p_loc, vb_ref[pl.ds(off, NA_KROWS * GRID_W), cols[sl]], preferred_element_type=F32)
             + jnp.dot(p_ctx, cvb_ref[:, cols[sl]], preferred_element_type=F32))
        outs.append(o / l)
    for sl in range(nsl):
        first = _head_mask(outs[0].shape, 0)
        o_ref[:, cols[sl]] = jnp.where(first, outs[sl * HEADS_PER_SLAB], outs[sl * HEADS_PER_SLAB + 1])


def _na_bias_kernel(rpb_ref, o_ref, *, rows):
    shape = (GRID_W, LANES)
    qc = lax.broadcasted_iota(jnp.int32, shape, 0)
    lane = lax.broadcasted_iota(jnp.int32, shape, 1)
    kc = lane % GRID_W
    cs = jnp.clip(qc - WIN_C // 2, 0, GRID_W - WIN_C)
    valid_c = (kc >= cs) & (kc < cs + WIN_C)
    first_row = lane < GRID_W
    r = rpb_ref[0] * LOG2E
    for p, r0 in enumerate((0, NA_QROWS, rows - NA_QROWS)):
        start = min(max(r0 - WIN_R // 2, 0), rows - NA_KROWS)
        for qr in range(NA_QROWS):
            q_row = r0 + qr
            ws = min(max(q_row - WIN_R // 2, 0), rows - WIN_R)
            for k2 in range(NA_KROWS // 2):
                halves = []
                for kk in range(2):
                    k_row = start + 2 * k2 + kk
                    if ws <= k_row < ws + WIN_R:
                        dr = k_row - q_row + WIN_R - 1
                        row = jnp.broadcast_to(r[dr:dr + 1, :], shape)
                        shift = (LANES - (WIN_C - 1) + GRID_W * kk) % LANES
                        halves.append(pltpu.roll(row, shift, 1, stride=1, stride_axis=0))
                    else:
                        halves.append(jnp.full(shape, NEG_BIAS, F32))
                tile = jnp.where(valid_c, jnp.where(first_row, halves[0], halves[1]), NEG_BIAS)
                o_ref[p, 0, qr * GRID_W:(qr + 1) * GRID_W, k2 * LANES:(k2 + 1) * LANES] = tile


def _na_bias_tables(rpb, rows):
    h, nr, nc = rpb.shape
    rpb_pad = jnp.zeros((h, 16, LANES), F32).at[:, :nr, :nc].set(rpb)
    tq, tk = NA_QROWS * GRID_W, NA_KROWS * GRID_W
    return pl.pallas_call(
        functools.partial(_na_bias_kernel, rows=rows),
        grid=(h,),
        in_specs=[pl.BlockSpec((1, 16, LANES), lambda i: (i, 0, 0))],
        out_specs=pl.BlockSpec((3, 1, tq, tk), lambda i: (0, i, 0, 0)),
        out_shape=jax.ShapeDtypeStruct((3, h, tq, tk), F32),
        compiler_params=_cparams(("parallel",)),
        name="na_bias",
    )(rpb_pad)


def _attn_na(proj, cache_k, cache_v, layer, bias, batch, seq):
    past = cache_k.shape[3]
    rows = seq // GRID_W
    assert rows >= NA_KROWS and rows % NA_QROWS == 0
    nrb = rows // NA_QROWS
    tq = NA_QROWS * GRID_W

    def bias_map(b, s, r):
        return ((r > 0).astype(jnp.int32) + (r == nrb - 1).astype(jnp.int32), s, 0, 0)

    w = NA_SLABS * LANES
    nsb = NA_WIDTH // w
    hps = NA_SLABS * HEADS_PER_SLAB
    ctx_spec = pl.BlockSpec((None, None, hps, past, NA_HEAD_DIM), lambda b, s, r: (b, layer, s, 0, 0))
    return pl.pallas_call(
        functools.partial(_attn_na_kernel, rows=rows),
        grid=(batch, nsb, nrb),
        in_specs=[pl.BlockSpec((tq, w), lambda b, s, r: (b * nrb + r, s)),
                  pl.BlockSpec((seq, w), lambda b, s, r: (b, nsb + s)),
                  pl.BlockSpec((seq, w), lambda b, s, r: (b, 2 * nsb + s)),
                  ctx_spec, ctx_spec,
                  pl.BlockSpec((1, NA_SLABS * HEADS_PER_SLAB, tq, NA_KROWS * GRID_W), bias_map)],
        out_specs=pl.BlockSpec((tq, w), lambda b, s, r: (b * nrb + r, s)),
        out_shape=jax.ShapeDtypeStruct((batch * seq, NA_WIDTH), F32),
        scratch_shapes=[pltpu.VMEM((seq, w), BF16), pltpu.VMEM((seq, w), BF16),
                        pltpu.VMEM((past, w), BF16), pltpu.VMEM((past, w), BF16)],
        compiler_params=_cparams(("parallel", "parallel", "arbitrary")),
        name="attn_na",
    )(proj, proj, proj, cache_k, cache_v, bias)


def _s5_prep_kernel(lr_ref, li_ref, ld_ref, btr_ref, bti_ref, cr_ref, ci_ref,
                    e_ref, m_ref, f_ref, at_ref, *, seg_tokens):
    T = S5_CHUNK
    ns = SLAB_STATE
    rb = lax.broadcasted_iota(jnp.int32, (LANES, ns), 0) // S5_GROUP_CH
    cb = lax.broadcasted_iota(jnp.int32, (LANES, ns), 1) // S5_STATE
    mask_b = rb == cb

    def powers(ldr, ldi, n):
        mag = jnp.exp(ldr * float(n))
        return mag * jnp.cos(ldi * float(n)), mag * jnp.sin(ldi * float(n))

    kt = [[None] * T, [None] * T]
    for d in range(2):
        lam_re, lam_im = lr_ref[0, d], li_ref[0, d]
        dt = jnp.exp(ld_ref[0, d])
        ldr, ldi = lam_re * dt, lam_im * dt
        a_re, a_im = powers(ldr, ldi, 1)
        mag2 = lam_re * lam_re + lam_im * lam_im
        f_re = ((a_re - 1.0) * lam_re + a_im * lam_im) / mag2
        f_im = (a_im * lam_re - (a_re - 1.0) * lam_im) / mag2
        btr = jnp.where(mask_b, btr_ref[0, d], 0.0)
        bti = jnp.where(mask_b, bti_ref[0, d], 0.0)
        cr = jnp.where(mask_b, cr_ref[0, d], 0.0)
        ci = jnp.where(mask_b, ci_ref[0, d], 0.0)
        ctr, cti = cr.T, ci.T
        c0 = d * 2 * ns
        e_res, e_ims = [], []
        for n in range(T):
            an_re, an_im = powers(ldr, ldi, n)
            w_re = f_re * an_re - f_im * an_im
            w_im = f_re * an_im + f_im * an_re
            e_re = btr * w_re - bti * w_im
            e_im = btr * w_im + bti * w_re
            t = T - 1 - n if d == 0 else n
            e_ref[0, t * LANES:(t + 1) * LANES, c0:c0 + ns] = e_re.astype(BF16)
            e_ref[0, t * LANES:(t + 1) * LANES, c0 + ns:c0 + 2 * ns] = e_im.astype(BF16)
            e_res.append(e_re)
            e_ims.append(e_im)
        k_all = _dot3(jnp.concatenate(e_res, axis=0), ctr) - _dot3(jnp.concatenate(e_ims, axis=0), cti)
        for n in range(T):
            kt[d][n] = k_all[n * LANES:(n + 1) * LANES]
        for tp in range(T):
            n = tp + 1 if d == 0 else T - tp
            an_re, an_im = powers(ldr, ldi, n)
            f_ref[0, c0:c0 + ns, tp * LANES:(tp + 1) * LANES] = (cr * an_re - ci * an_im).T.astype(BF16)
            f_ref[0, c0 + ns:c0 + 2 * ns, tp * LANES:(tp + 1) * LANES] = (-(cr * an_im + ci * an_re)).T.astype(BF16)
        at_re, at_im = powers(ldr, ldi, T)
        at_ref[0, 2 * d:2 * d + 1, :] = at_re
        at_ref[0, 2 * d + 1:2 * d + 2, :] = at_im
        as_re, as_im = powers(ldr, ldi, seg_tokens)
        at_ref[0, 4 + 2 * d:5 + 2 * d, :] = as_re
        at_ref[0, 5 + 2 * d:6 + 2 * d, :] = as_im
    for t in range(T):
        for tp in range(T):
            if tp > t:
                blk = kt[0][tp - t]
            elif tp < t:
                blk = kt[1][t - tp]
            else:
                blk = kt[0][0] + kt[1][0]
            m_ref[0, t * LANES:(t + 1) * LANES, tp * LANES:(tp + 1) * LANES] = blk.astype(BF16)


def _s5_prep(lam_re, lam_im, log_dt, b_re, b_im, c_re, c_im, seg_tokens):
    T, S, G8, ns = S5_CHUNK, N_GROUP_SLABS, GROUPS_PER_SLAB, SLAB_STATE

    def state_vec(v):
        return v.reshape(2, S, ns).transpose(1, 0, 2)

    lr, li = state_vec(lam_re), state_vec(lam_im)
    ld = state_vec(jnp.broadcast_to(log_dt[:, :, None], lam_re.shape))

    def bt(v):
        v = v.reshape(2, S, G8, S5_STATE, S5_GROUP_CH).transpose(1, 0, 2, 4, 3).reshape(S, 2, LANES, S5_STATE)
        return jnp.tile(v, (1, 1, 1, G8))

    def crow(v):
        v = v.reshape(2, S, G8, S5_GROUP_CH, S5_STATE).transpose(1, 0, 2, 3, 4).reshape(S, 2, LANES, S5_STATE)
        return jnp.tile(v, (1, 1, 1, G8))

    row = lambda v: v[:, :, None, :]
    spec_row = pl.BlockSpec((1, 2, 1, ns), lambda j: (j, 0, 0, 0))
    spec_bt = pl.BlockSpec((1, 2, LANES, ns), lambda j: (j, 0, 0, 0))
    return pl.pallas_call(
        functools.partial(_s5_prep_kernel, seg_tokens=seg_tokens),
        grid=(S,),
        in_specs=[spec_row, spec_row, spec_row, spec_bt, spec_bt, spec_bt, spec_bt],
        out_specs=[pl.BlockSpec((1, T * LANES, SLAB_COLS), lambda j: (j, 0, 0)),
                   pl.BlockSpec((1, T * LANES, T * LANES), lambda j: (j, 0, 0)),
                   pl.BlockSpec((1, SLAB_COLS, T * LANES), lambda j: (j, 0, 0)),
                   pl.BlockSpec((1, 8, ns), lambda j: (j, 0, 0))],
        out_shape=[jax.ShapeDtypeStruct((S, T * LANES, SLAB_COLS), BF16),
                   jax.ShapeDtypeStruct((S, T * LANES, T * LANES), BF16),
                   jax.ShapeDtypeStruct((S, SLAB_COLS, T * LANES), BF16),
                   jax.ShapeDtypeStruct((S, 8, ns), F32)],
        compiler_params=_cparams(("parallel",)),
        name="s5_prep",
    )(row(lr), row(li), row(ld), bt(b_re), bt(b_im), crow(c_re), crow(c_im))


def _s5_kernel(u_ref, h0_ref, e_ref, m_ref, f_ref, at_ref, d_ref, y_ref, fin_ref,
               us_ref, ubf_ref, st_ref, ys_ref, *, bt, nseg, seq):
    T = S5_CHUNK
    ns = SLAB_STATE
    npar = bt * nseg
    seg = seq // nseg
    nc = seg // T
    for p in range(npar):
        for t in range(T):
            us_ref[t, pl.ds(p, nc, stride=npar), :] = u_ref[pl.ds(p * seg + t, nc, stride=T), :]
    for t in range(T):
        ubf_ref[:, t * LANES:(t + 1) * LANES] = us_ref[t].astype(BF16)
    ubf = ubf_ref[...]
    st_ref[...] = jnp.dot(ubf, e_ref[0], preferred_element_type=F32)

    at = at_ref[0]
    a_fr, a_fi, a_br, a_bi = at[0:1], at[1:2], at[2:3], at[3:4]

    def scan(init, keep_entering):
        def step(i, carry):
            c_fr, c_fi, c_br, c_bi = carry
            rf = pl.multiple_of(i * npar, npar)
            rbk = pl.multiple_of((nc - 1 - i) * npar, npar)
            l_fr = st_ref[pl.ds(rf, npar), 0:ns]
            l_fi = st_ref[pl.ds(rf, npar), ns:2 * ns]
            l_br = st_ref[pl.ds(rbk, npar), 2 * ns:3 * ns]
            l_bi = st_ref[pl.ds(rbk, npar), 3 * ns:4 * ns]
            if keep_entering:
                st_ref[pl.ds(rf, npar), 0:ns] = c_fr
                st_ref[pl.ds(rf, npar), ns:2 * ns] = c_fi
                st_ref[pl.ds(rbk, npar), 2 * ns:3 * ns] = c_br
                st_ref[pl.ds(rbk, npar), 3 * ns:4 * ns] = c_bi
            return (a_fr * c_fr - a_fi * c_fi + l_fr, a_fr * c_fi + a_fi * c_fr + l_fi,
                    a_br * c_br - a_bi * c_bi + l_br, a_br * c_bi + a_bi * c_br + l_bi)

        return lax.fori_loop(0, nc, step, init)

    h0 = h0_ref[0, 0]
    h0 = [h0[:, k * ns:(k + 1) * ns] for k in range(4)]
    if nseg == 1:
        enter = tuple(h0)
    else:
        zero = jnp.zeros((npar, ns), F32)
        z_fr, z_fi, z_br, z_bi = scan((zero, zero, zero, zero), False)
        s_fr, s_fi, s_br, s_bi = at[4:5], at[5:6], at[6:7], at[7:8]
        rows_f, rows_b = [None] * npar, [None] * npar
        for b in range(bt):
            c_re, c_im = h0[0][b:b + 1], h0[1][b:b + 1]
            for s in range(nseg):
                p = b * nseg + s
                rows_f[p] = (c_re, c_im)
                c_re, c_im = (s_fr * c_re - s_fi * c_im + z_fr[p:p + 1], s_fr * c_im + s_fi * c_re + z_fi[p:p + 1])
            c_re, c_im = h0[2][b:b + 1], h0[3][b:b + 1]
            for s in reversed(range(nseg)):
                p = b * nseg + s
                rows_b[p] = (c_re, c_im)
                c_re, c_im = (s_br * c_re - s_bi * c_im + z_br[p:p + 1], s_br * c_im + s_bi * c_re + z_bi[p:p + 1])
        enter = (jnp.concatenate([r[0] for r in rows_f], axis=0), jnp.concatenate([r[1] for r in rows_f], axis=0),
                 jnp.concatenate([r[0] for r in rows_b], axis=0), jnp.concatenate([r[1] for r in rows_b], axis=0))
    fin = scan(enter, True)
    for b in range(bt):
        last_f, last_b = b * nseg + nseg - 1, b * nseg
        fin_ref[0, 0, b:b + 1, 0:ns] = fin[0][last_f:last_f + 1]
        fin_ref[0, 0, b:b + 1, ns:2 * ns] = fin[1][last_f:last_f + 1]
        fin_ref[0, 0, b:b + 1, 2 * ns:3 * ns] = fin[2][last_b:last_b + 1]
        fin_ref[0, 0, b:b + 1, 3 * ns:4 * ns] = fin[3][last_b:last_b + 1]

    y = (jnp.dot(ubf, m_ref[0], preferred_element_type=F32)
         + jnp.dot(st_ref[...].astype(BF16), f_ref[0], preferred_element_type=F32))
    for t in range(T):
        ys_ref[t] = y[:, t * LANES:(t + 1) * LANES] + d_ref[0] * us_ref[t]
    for p in range(npar):
        for t in range(T):
            y_ref[pl.ds(p * seg + t, nc, stride=T), :] = ys_ref[t, pl.ds(p, nc, stride=npar), :]


def _s5(proj, u_col, h0, e, m, f, at, d_skip, batch, seq, bt, nseg):
    T, S = S5_CHUNK, N_GROUP_SLABS
    rows = bt * seq // T
    const = lambda j, g: (j, 0, 0)
    y, fin = pl.pallas_call(
        functools.partial(_s5_kernel, bt=bt, nseg=nseg, seq=seq),
        grid=(S, batch // bt),
        in_specs=[pl.BlockSpec((bt * seq, LANES), lambda j, g: (g, u_col + j)),
                  pl.BlockSpec((1, 1, bt, SLAB_COLS), lambda j, g: (j, g, 0, 0)),
                  pl.BlockSpec((1, T * LANES, SLAB_COLS), const),
                  pl.BlockSpec((1, T * LANES, T * LANES), const),
                  pl.BlockSpec((1, SLAB_COLS, T * LANES), const),
                  pl.BlockSpec((1, 8, SLAB_STATE), const),
                  pl.BlockSpec((1, 1, LANES), const)],
        out_specs=[pl.BlockSpec((bt * seq, LANES), lambda j, g: (g, j)),
                   pl.BlockSpec((1, 1, bt, SLAB_COLS), lambda j, g: (j, g, 0, 0))],
        out_shape=[jax.ShapeDtypeStruct((batch * seq, S5_WIDTH), F32),
                   jax.ShapeDtypeStruct((S, batch // bt, bt, SLAB_COLS), F32)],
        scratch_shapes=[pltpu.VMEM((T, rows, LANES), F32), pltpu.VMEM((rows, T * LANES), BF16),
                        pltpu.VMEM((rows, SLAB_COLS), F32), pltpu.VMEM((T, rows, LANES), F32)],
        compiler_params=_cparams(("parallel", "arbitrary")),
        name="s5",
    )(proj, h0.reshape(S, batch // bt, bt, SLAB_COLS), e, m, f, at, d_skip.reshape(S, 1, LANES))
    return y, fin.reshape(S, batch, SLAB_COLS)


def _state_to_slabs(state):
    b = state.shape[0]
    s = state.reshape(b, 2, 2, N_GROUP_SLABS, SLAB_STATE).transpose(3, 0, 1, 2, 4)
    return s.reshape(N_GROUP_SLABS, b, SLAB_COLS)


def _slabs_to_state(slabs):
    b = slabs.shape[1]
    s = slabs.reshape(N_GROUP_SLABS, b, 2, 2, SLAB_STATE).transpose(1, 2, 3, 0, 4)
    return s.reshape(b, 2, 2, S5_GROUPS, S5_STATE)


def _merge_kernel(x_ref, mods_ref, ya_ref, yb_ref, ga_ref, gb_ref, wglu_ref, wua_ref, wub_ref, wo_ref, o_ref):
    yb = jax.nn.gelu(yb_ref[...])
    glu = jnp.dot(yb.astype(BF16), wglu_ref[...], preferred_element_type=F32)
    yb = yb * jax.nn.sigmoid(glu)
    a = jnp.dot(ya_ref[...].astype(BF16), wua_ref[...], preferred_element_type=F32)
    b = jnp.dot(yb.astype(BF16), wub_ref[...], preferred_element_type=F32)
    merged = jax.nn.sigmoid(ga_ref[...]) * a + jax.nn.sigmoid(gb_ref[...]) * b
    o = jnp.dot(merged.astype(BF16), wo_ref[...], preferred_element_type=F32)
    o_ref[...] = x_ref[...] + mods_ref[0][5:6] * o


def _merge(x, mods3, row0, cond_tokens, ya, yb, proj, w_glu, w_up_a, w_up_b, w_out):
    n_tok, d = x.shape
    tm = min(256, n_tok)
    assert cond_tokens % tm == 0
    tiles_per_row = cond_tokens // tm
    gate_col = (3 * NA_WIDTH + S5_WIDTH) // d
    assert gate_col * d == 3 * NA_WIDTH + S5_WIDTH
    resident = functools.partial(pl.BlockSpec, pipeline_mode=pl.Buffered(1))
    return pl.pallas_call(
        _merge_kernel,
        grid=(n_tok // tm,),
        in_specs=[pl.BlockSpec((tm, d), lambda i: (i, 0)),
                  pl.BlockSpec((1, N_MOD, d), lambda i: (row0 + i // tiles_per_row, 0, 0)),
                  pl.BlockSpec((tm, NA_WIDTH), lambda i: (i, 0)),
                  pl.BlockSpec((tm, S5_WIDTH), lambda i: (i, 0)),
                  pl.BlockSpec((tm, d), lambda i: (i, gate_col)),
                  pl.BlockSpec((tm, d), lambda i: (i, gate_col + 1)),
                  resident((S5_WIDTH, S5_WIDTH), lambda i: (0, 0)),
                  resident((NA_WIDTH, d), lambda i: (0, 0)),
                  resident((S5_WIDTH, d), lambda i: (0, 0)),
                  resident((d, d), lambda i: (0, 0))],
        out_specs=pl.BlockSpec((tm, d), lambda i: (i, 0)),
        out_shape=jax.ShapeDtypeStruct((n_tok, d), F32),
        compiler_params=_cparams(("parallel",)),
        name="merge",
    )(x, mods3, ya, yb, proj, proj, w_glu, w_up_a, w_up_b, w_out)


def kernel(x_prompt, x_sample, cache_k, cache_v, state_ssm, c, c_ctx, w_ada, b_ada, norm_g, ffn_in, ffn_out, w_in, rpb, s5_lam_re, s5_lam_im, s5_log_dt, s5_b_re, s5_b_im, s5_c_re, s5_c_im, s5_d, w_glu, w_up_a, w_up_b, w_out, final_g):
    depth = w_ada.shape[0]
    pb, pl_, d = x_prompt.shape
    sb, sl, _ = x_sample.shape
    u_col = 3 * NA_WIDTH // LANES
    mod_rows = 8 * ((1 + sb + 7) // 8)

    xp = x_prompt.reshape(pb * pl_, d)
    xs = x_sample.reshape(sb * sl, d)
    cond = jnp.concatenate([c_ctx[None, :], c, jnp.zeros((mod_rows - 1 - sb, d), F32)], axis=0)
    ffn_in_bf, ffn_out_bf = ffn_in.astype(BF16), ffn_out.astype(BF16)
    new_k, new_v, new_s = [], [], []
    for l in range(depth):
        last = l == depth - 1
        mods3 = _mods(cond, w_ada[l], b_ada[l]).reshape(mod_rows, N_MOD, d)
        w_in_bf = w_in[l].astype(BF16)
        w_glu_bf, w_up_a_bf, w_up_b_bf, w_out_bf = (w[l].astype(BF16) for w in (w_glu, w_up_a, w_up_b, w_out))
        e, m, f, at = _s5_prep(s5_lam_re[l], s5_lam_im[l], s5_log_dt[l], s5_b_re[l], s5_b_im[l],
                               s5_c_re[l], s5_c_im[l], sl // S5_SEGMENTS)
        bias = _na_bias_tables(rpb[l], sl // GRID_W)

        ptok = pb * pl_
        xp = _ffn(xp, mods3, 0, ptok, norm_g[l, 0], final_g, ffn_in_bf, ffn_out_bf, l, 0, 0, False)
        proj_p = _proj(xp, mods3, 0, ptok, norm_g[l, 1], w_in_bf)
        ya_p, k_p, v_p = _attn_ctx(proj_p, pb, pl_)
        yb_p, fin_p = _s5(proj_p, u_col, jnp.zeros((N_GROUP_SLABS, pb, SLAB_COLS), F32), e, m, f, at,
                          s5_d[l], pb, pl_, math.gcd(pb, 16), 1)
        xp = _merge(xp, mods3, 0, ptok, ya_p, yb_p, proj_p, w_glu_bf, w_up_a_bf, w_up_b_bf, w_out_bf)
        xp = _ffn(xp, mods3, 0, ptok, norm_g[l, 2], final_g, ffn_in_bf, ffn_out_bf, l, 1, 6, last)
        new_k.append(k_p)
        new_v.append(v_p)
        new_s.append(_slabs_to_state(fin_p))

        xs = _ffn(xs, mods3, 1, sl, norm_g[l, 0], final_g, ffn_in_bf, ffn_out_bf, l, 0, 0, False)
        proj_s = _proj(xs, mods3, 1, sl, norm_g[l, 1], w_in_bf)
        ya_s = _attn_na(proj_s, cache_k, cache_v, l, bias, sb, sl)
        yb_s, _ = _s5(proj_s, u_col, _state_to_slabs(state_ssm[:, l]), e, m, f, at, s5_d[l], sb, sl, 1, S5_SEGMENTS)
        xs = _merge(xs, mods3, 1, sl, ya_s, yb_s, proj_s, w_glu_bf, w_up_a_bf, w_up_b_bf, w_out_bf)
        xs = _ffn(xs, mods3, 1, sl, norm_g[l, 2], final_g, ffn_in_bf, ffn_out_bf, l, 1, 6, last)

    y_prompt = xp.reshape(pb, pl_, d)
    y_sample = xs.reshape(sb, sl, d)
    return (y_prompt, y_sample, jnp.stack(new_k, axis=1), jnp.stack(new_v, axis=1), jnp.stack(new_s, axis=1))
```

```python
import functools
import math

import jax
import jax.numpy as jnp
from jax import lax
from jax.experimental import pallas as pl
from jax.experimental.pallas import tpu as pltpu

F32 = jnp.float32
BF16 = jnp.bfloat16

GRID_W = 64
NA_HEADS = 16
NA_HEAD_DIM = 64
NA_WIDTH = NA_HEADS * NA_HEAD_DIM
WIN_R = 8
WIN_C = 16
S5_GROUP_CH = 16
S5_WIDTH = 1024
S5_GROUPS = S5_WIDTH // S5_GROUP_CH
S5_STATE = 64
N_MOD = 9
EPS = 1e-6

LANES = 128
HEADS_PER_SLAB = LANES // NA_HEAD_DIM
N_HEAD_SLABS = NA_WIDTH // LANES
GROUPS_PER_SLAB = LANES // S5_GROUP_CH
N_GROUP_SLABS = S5_WIDTH // LANES
SLAB_STATE = GROUPS_PER_SLAB * S5_STATE
SLAB_COLS = 4 * SLAB_STATE
HALF_LANES = LANES // 2
HALF_STATE = SLAB_STATE // 2
S5_CHUNK = 8
S5_HALF_IN = S5_CHUNK * HALF_LANES
S5_HALF_STATES = 4 * HALF_STATE
S5_SEGMENTS = 8
NA_QROWS = 4
NA_KROWS = NA_QROWS + WIN_R
NEG_BIAS = -1e30
LOG2E = 1.4426950408889634
VMEM_LIMIT = 56 * 1024 * 1024


def _cparams(sem, limit=VMEM_LIMIT):
    return pltpu.CompilerParams(dimension_semantics=sem, vmem_limit_bytes=limit)


def _silu(x):
    return x * jax.nn.sigmoid(x)


NORM_ROWS = 16
NORM_UNROLL = 8


def _norm_mod_store(x_ref, h_ref, g, shift, scale):
    gs = g * (1.0 + scale)

    def body(i, carry):
        r0 = pl.multiple_of(i * NORM_ROWS, NORM_ROWS)
        x = x_ref[pl.ds(r0, NORM_ROWS), :]
        r = lax.rsqrt(jnp.mean(x * x, axis=-1, keepdims=True) + EPS)
        h_ref[pl.ds(r0, NORM_ROWS), :] = ((x * r) * gs + shift).astype(BF16)
        return carry

    lax.fori_loop(0, x_ref.shape[0] // NORM_ROWS, body, 0, unroll=NORM_UNROLL)


def _split_bf16(x):
    hi = x.astype(BF16)
    return hi, (x - hi.astype(F32)).astype(BF16)


def _dot3(a, b):
    a_hi, a_lo = _split_bf16(a)
    b_hi, b_lo = _split_bf16(b)
    return (jnp.dot(a_hi, b_hi, preferred_element_type=F32)
            + (jnp.dot(a_lo, b_hi, preferred_element_type=F32) + jnp.dot(a_hi, b_lo, preferred_element_type=F32)))


def _mods_kernel(c_ref, w_ref, b_ref, o_ref):
    o_ref[...] = _dot3(_silu(c_ref[...]), w_ref[...]) + b_ref[...]


def _mods(cond, w_ada, b_ada):
    rows, d = cond.shape
    n = w_ada.shape[1]
    tn = math.gcd(1024, n)
    return pl.pallas_call(
        _mods_kernel,
        grid=(n // tn,),
        in_specs=[pl.BlockSpec((rows, d), lambda j: (0, 0)),
                  pl.BlockSpec((d, tn), lambda j: (0, j)),
                  pl.BlockSpec((1, tn), lambda j: (0, j))],
        out_specs=pl.BlockSpec((rows, tn), lambda j: (0, j)),
        out_shape=jax.ShapeDtypeStruct((rows, n), F32),
        compiler_params=_cparams(("arbitrary",)),
        name="mods",
    )(cond, w_ada, b_ada.reshape(1, n))


def _ffn_kernel(x_hbm, mods_ref, ng_ref, fg_ref, wg_ref, wu_ref, wo_ref, o_ref, xs_ref, h_ref, rs_ref, sem,
                *, mod_base, final_norm):
    i, j = pl.program_id(0), pl.program_id(1)
    tm = xs_ref.shape[0]

    def x_copy(tile):
        return pltpu.make_async_copy(x_hbm.at[pl.ds(tile * tm, tm), :], xs_ref, sem)

    @pl.when((i == 0) & (j == 0))
    def _():
        x_copy(0).start()

    @pl.when(j == 0)
    def _():
        x_copy(i).wait()
        m = mods_ref[0]
        gs = ng_ref[...] * (1.0 + m[mod_base + 1:mod_base + 2])
        shift = m[mod_base:mod_base + 1]

        def body(r, carry):
            r0 = pl.multiple_of(r * NORM_ROWS, NORM_ROWS)
            x = xs_ref[pl.ds(r0, NORM_ROWS), :]
            o_ref[pl.ds(r0, NORM_ROWS), :] = x
            rs = lax.rsqrt(jnp.mean(x * x, axis=-1, keepdims=True) + EPS)
            h_ref[pl.ds(r0, NORM_ROWS), :] = ((x * rs) * gs + shift).astype(BF16)
            return carry

        lax.fori_loop(0, tm // NORM_ROWS, body, 0, unroll=NORM_UNROLL)

    @pl.when((j == 1) & (i + 1 < pl.num_programs(0)))
    def _():
        x_copy(i + 1).start()

    h = h_ref[...]
    g = jnp.dot(h, wg_ref[...], preferred_element_type=F32)
    u = jnp.dot(h, wu_ref[...], preferred_element_type=F32)
    a = (_silu(g) * u).astype(BF16)
    half_gate = 0.5 * mods_ref[0][mod_base + 2:mod_base + 3]
    o_ref[...] += half_gate * jnp.dot(a, wo_ref[...], preferred_element_type=F32)

    if final_norm:
        @pl.when(j == pl.num_programs(1) - 1)
        def _():
            def body(r, carry):
                r0 = pl.multiple_of(r * NORM_ROWS, NORM_ROWS)
                y = o_ref[pl.ds(r0, NORM_ROWS), :]
                rs = lax.rsqrt(jnp.mean(y * y, axis=-1, keepdims=True) + EPS)
                rs_ref[pl.ds(r0, NORM_ROWS), :] = jnp.broadcast_to(rs, (NORM_ROWS, LANES))
                return carry

            lax.fori_loop(0, tm // NORM_ROWS, body, 0, unroll=NORM_UNROLL)
            for c in range(o_ref.shape[1] // LANES):
                cols = slice(c * LANES, (c + 1) * LANES)
                o_ref[:, cols] = (o_ref[:, cols] * rs_ref[...]) * fg_ref[:, cols]


FFN_TM = 1024
FFN_TF = 512


def _ffn(x, mods3, row0, cond_tokens, norm_g, final_g, w_in_bf, w_out_bf, layer, which, mod_base, final_norm):
    n_tok, d = x.shape
    d_ff = w_out_bf.shape[2]
    tm = min(FFN_TM, n_tok)
    tf = min(FFN_TF, d_ff)
    assert cond_tokens % tm == 0
    tiles_per_row = cond_tokens // tm
    assert d_ff // tf >= 2
    kern = functools.partial(_ffn_kernel, mod_base=mod_base, final_norm=final_norm)
    return pl.pallas_call(
        kern,
        grid=(n_tok // tm, d_ff // tf),
        in_specs=[pl.BlockSpec(memory_space=pl.ANY),
                  pl.BlockSpec((1, N_MOD, d), lambda i, j: (row0 + i // tiles_per_row, 0, 0)),
                  pl.BlockSpec((1, d), lambda i, j: (0, 0)),
                  pl.BlockSpec((1, d), lambda i, j: (0, 0)),
                  pl.BlockSpec((None, None, d, tf), lambda i, j: (layer, which, 0, j)),
                  pl.BlockSpec((None, None, d, tf), lambda i, j: (layer, which, 0, d_ff // tf + j)),
                  pl.BlockSpec((None, None, tf, d), lambda i, j: (layer, which, j, 0))],
        out_specs=pl.BlockSpec((tm, d), lambda i, j: (i, 0)),
        out_shape=jax.ShapeDtypeStruct((n_tok, d), F32),
        scratch_shapes=[pltpu.VMEM((tm, d), F32), pltpu.VMEM((tm, d), BF16), pltpu.VMEM((tm, LANES), F32),
                        pltpu.SemaphoreType.DMA(())],
        compiler_params=_cparams(("arbitrary", "arbitrary")),
        name="ffn",
    )(x, mods3, norm_g.reshape(1, d), final_g.reshape(1, d), w_in_bf, w_in_bf, w_out_bf)


def _proj_kernel(x_hbm, mods_ref, ng_ref, w_ref, o_ref, xs_ref, h_ref, sem):
    i, j = pl.program_id(0), pl.program_id(1)
    tm = xs_ref.shape[0]

    def x_copy(tile):
        return pltpu.make_async_copy(x_hbm.at[pl.ds(tile * tm, tm), :], xs_ref, sem)

    @pl.when((i == 0) & (j == 0))
    def _():
        x_copy(0).start()

    @pl.when(j == 0)
    def _():
        x_copy(i).wait()
        m = mods_ref[0]
        _norm_mod_store(xs_ref, h_ref, ng_ref[...], m[3:4], m[4:5])

    @pl.when((j == 1) & (i + 1 < pl.num_programs(0)))
    def _():
        x_copy(i + 1).start()

    o_ref[...] = jnp.dot(h_ref[...], w_ref[...], preferred_element_type=F32)


def _proj(x, mods3, row0, cond_tokens, norm_g, w_bf):
    n_tok, d = x.shape
    n = w_bf.shape[1]
    tm = min(1024, n_tok)
    tn = math.gcd(2048, d)
    assert cond_tokens % tm == 0 and n // tn >= 2
    tiles_per_row = cond_tokens // tm
    return pl.pallas_call(
        _proj_kernel,
        grid=(n_tok // tm, n // tn),
        in_specs=[pl.BlockSpec(memory_space=pl.ANY),
                  pl.BlockSpec((1, N_MOD, d), lambda i, j: (row0 + i // tiles_per_row, 0, 0)),
                  pl.BlockSpec((1, d), lambda i, j: (0, 0)),
                  pl.BlockSpec((d, tn), lambda i, j: (0, j))],
        out_specs=pl.BlockSpec((tm, tn), lambda i, j: (i, j)),
        out_shape=jax.ShapeDtypeStruct((n_tok, n), F32),
        scratch_shapes=[pltpu.VMEM((tm, d), F32), pltpu.VMEM((tm, d), BF16), pltpu.SemaphoreType.DMA(())],
        compiler_params=_cparams(("arbitrary", "arbitrary")),
        name="proj",
    )(x, mods3, norm_g.reshape(1, d), w_bf)


def _head_mask(shape, hh):
    lane = lax.broadcasted_iota(jnp.int32, shape, 1)
    return (lane >= hh * NA_HEAD_DIM) & (lane < (hh + 1) * NA_HEAD_DIM)


def _nt_dot(a, b):
    return lax.dot_general(a, b, (((1,), (1,)), ((), ())), preferred_element_type=F32)


Q_SCALE = NA_HEAD_DIM ** -0.5 * LOG2E
CTX_SLABS = 4
NA_SLABS = 2


def _ones_other_head(v, hh):
    return jnp.where(_head_mask(v.shape, hh), v, 1.0)


def _finish_heads(o0, o1):
    first = _head_mask(o0.shape, 0)
    num = jnp.where(first, o0, o1)
    den = pltpu.roll(jnp.where(first, o1, o0), NA_HEAD_DIM, 1)
    return num / den


def _attn_ctx_kernel(q_ref, k_ref, v_ref, o_ref, ko_ref, vo_ref):
    for sl in range(q_ref.shape[1] // LANES):
        cols = slice(sl * LANES, (sl + 1) * LANES)
        q = q_ref[:, cols] * Q_SCALE
        kf = k_ref[:, cols]
        vf = v_ref[:, cols]
        k = kf.astype(BF16)
        outs = []
        for hh in range(HEADS_PER_SLAB):
            head = slice(hh * NA_HEAD_DIM, (hh + 1) * NA_HEAD_DIM)
            ko_ref[0, sl * HEADS_PER_SLAB + hh] = kf[:, head]
            vo_ref[0, sl * HEADS_PER_SLAB + hh] = vf[:, head]
            qm = jnp.where(_head_mask(q.shape, hh), q, 0.0).astype(BF16)
            s = _nt_dot(qm, k)
            p = jnp.exp2(s - jnp.max(s, axis=-1, keepdims=True))
            outs.append(jnp.dot(p.astype(BF16), _ones_other_head(vf, hh).astype(BF16),
                                preferred_element_type=F32))
        o_ref[:, cols] = _finish_heads(*outs)


def _attn_ctx(proj, batch, seq):
    w = CTX_SLABS * LANES
    nsb = NA_WIDTH // w
    hps = CTX_SLABS * HEADS_PER_SLAB
    kv_spec = pl.BlockSpec((1, hps, seq, NA_HEAD_DIM), lambda b, s: (b, s, 0, 0))
    kv_shape = jax.ShapeDtypeStruct((batch, NA_HEADS, seq, NA_HEAD_DIM), F32)
    return pl.pallas_call(
        _attn_ctx_kernel,
        grid=(batch, nsb),
        in_specs=[pl.BlockSpec((seq, w), lambda b, s: (b, s)),
                  pl.BlockSpec((seq, w), lambda b, s: (b, nsb + s)),
                  pl.BlockSpec((seq, w), lambda b, s: (b, 2 * nsb + s))],
        out_specs=[pl.BlockSpec((seq, w), lambda b, s: (b, s)), kv_spec, kv_spec],
        out_shape=[jax.ShapeDtypeStruct((batch * seq, NA_WIDTH), F32), kv_shape, kv_shape],
        compiler_params=_cparams(("parallel", "parallel")),
        name="attn_ctx",
    )(proj, proj, proj)


def _attn_na_kernel(q_ref, k_ref, v_ref, ck_ref, cv_ref, bias_ref, o_ref,
                    kb_ref, vb_ref, ckb_ref, cvb_ref, *, rows):
    rb = pl.program_id(2)
    nsl = q_ref.shape[1] // LANES

    @pl.when(rb == 0)
    def _():
        kb_ref[...] = k_ref[...].astype(BF16)
        vb_ref[...] = v_ref[...].astype(BF16)
        for hd in range(ck_ref.shape[0]):
            lanes = slice(hd * NA_HEAD_DIM, (hd + 1) * NA_HEAD_DIM)
            ckb_ref[:, lanes] = ck_ref[hd].astype(BF16)
            cvb_ref[:, lanes] = cv_ref[hd].astype(BF16)

    start = jnp.clip(rb * NA_QROWS - WIN_R // 2, 0, rows - NA_KROWS)
    off = pl.multiple_of(start * GRID_W, NA_QROWS * GRID_W)
    heads = [(sl, hh) for sl in range(nsl) for hh in range(HEADS_PER_SLAB)]
    cols = [slice(sl * LANES, (sl + 1) * LANES) for sl in range(nsl)]
    scores = []
    for sl, hh in heads:
        q = q_ref[:, cols[sl]] * Q_SCALE
        qm = jnp.where(_head_mask(q.shape, hh), q, 0.0).astype(BF16)
        s_loc = _nt_dot(qm, kb_ref[pl.ds(off, NA_KROWS * GRID_W), cols[sl]]) + bias_ref[0, sl * HEADS_PER_SLAB + hh]
        s_ctx = _nt_dot(qm, ckb_ref[:, cols[sl]])
        scores.append((s_loc, s_ctx))
    probs = []
    for s_loc, s_ctx in scores:
        m = jnp.maximum(jnp.max(s_loc, axis=-1, keepdims=True), jnp.max(s_ctx, axis=-1, keepdims=True))
        p_loc = jnp.exp2(s_loc - m)
        p_ctx = jnp.exp2(s_ctx - m)
        l = jnp.sum(p_loc, axis=-1, keepdims=True) + jnp.sum(p_ctx, axis=-1, keepdims=True)
        probs.append((p_loc.astype(BF16), p_ctx.astype(BF16), l))
    outs = []
    for (sl, hh), (p_loc, p_ctx, l) in zip(heads, probs):
        o = (jnp.dot(p_loc, vb_ref[pl.ds(off, NA_KROWS * GRID_W), cols[sl]], preferred_element_type=F32)
             + jnp.dot(p_ctx, cvb_ref[:, cols[sl]], preferred_element_type=F32))
        outs.append(o / l)
    for sl in range(nsl):
        first = _head_mask(outs[0].shape, 0)
        o_ref[:, cols[sl]] = jnp.where(first, outs[sl * HEADS_PER_SLAB], outs[sl * HEADS_PER_SLAB + 1])


def _na_bias_kernel(rpb_ref, o_ref, *, rows):
    shape = (GRID_W, LANES)
    qc = lax.broadcasted_iota(jnp.int32, shape, 0)
    lane = lax.broadcasted_iota(jnp.int32, shape, 1)
    kc = lane % GRID_W
    cs = jnp.clip(qc - WIN_C // 2, 0, GRID_W - WIN_C)
    valid_c = (kc >= cs) & (kc < cs + WIN_C)
    first_row = lane < GRID_W
    r = rpb_ref[0] * LOG2E
    for p, r0 in enumerate((0, NA_QROWS, rows - NA_QROWS)):
        start = min(max(r0 - WIN_R // 2, 0), rows - NA_KROWS)
        for qr in range(NA_QROWS):
            q_row = r0 + qr
            ws = min(max(q_row - WIN_R // 2, 0), rows - WIN_R)
            for k2 in range(NA_KROWS // 2):
                halves = []
                for kk in range(2):
                    k_row = start + 2 * k2 + kk
                    if ws <= k_row < ws + WIN_R:
                        dr = k_row - q_row + WIN_R - 1
                        row = jnp.broadcast_to(r[dr:dr + 1, :], shape)
                        shift = (LANES - (WIN_C - 1) + GRID_W * kk) % LANES
                        halves.append(pltpu.roll(row, shift, 1, stride=1, stride_axis=0))
                    else:
                        halves.append(jnp.full(shape, NEG_BIAS, F32))
                tile = jnp.where(valid_c, jnp.where(first_row, halves[0], halves[1]), NEG_BIAS)
                o_ref[p, 0, qr * GRID_W:(qr + 1) * GRID_W, k2 * LANES:(k2 + 1) * LANES] = tile


def _na_bias_tables(rpb, rows):
    h, nr, nc = rpb.shape
    rpb_pad = jnp.zeros((h, 16, LANES), F32).at[:, :nr, :nc].set(rpb)
    tq, tk = NA_QROWS * GRID_W, NA_KROWS * GRID_W
    return pl.pallas_call(
        functools.partial(_na_bias_kernel, rows=rows),
        grid=(h,),
        in_specs=[pl.BlockSpec((1, 16, LANES), lambda i: (i, 0, 0))],
        out_specs=pl.BlockSpec((3, 1, tq, tk), lambda i: (0, i, 0, 0)),
        out_shape=jax.ShapeDtypeStruct((3, h, tq, tk), F32),
        compiler_params=_cparams(("parallel",)),
        name="na_bias",
    )(rpb_pad)


def _attn_na(proj, cache_k, cache_v, layer, bias, batch, seq):
    past = cache_k.shape[3]
    rows = seq // GRID_W
    assert rows >= NA_KROWS and rows % NA_QROWS == 0
    nrb = rows // NA_QROWS
    tq = NA_QROWS * GRID_W

    def bias_map(b, s, r):
        return ((r > 0).astype(jnp.int32) + (r == nrb - 1).astype(jnp.int32), s, 0, 0)

    w = NA_SLABS * LANES
    nsb = NA_WIDTH // w
    hps = NA_SLABS * HEADS_PER_SLAB
    ctx_spec = pl.BlockSpec((None, None, hps, past, NA_HEAD_DIM), lambda b, s, r: (b, layer, s, 0, 0))
    return pl.pallas_call(
        functools.partial(_attn_na_kernel, rows=rows),
        grid=(batch, nsb, nrb),
        in_specs=[pl.BlockSpec((tq, w), lambda b, s, r: (b * nrb + r, s)),
                  pl.BlockSpec((seq, w), lambda b, s, r: (b, nsb + s)),
                  pl.BlockSpec((seq, w), lambda b, s, r: (b, 2 * nsb + s)),
                  ctx_spec, ctx_spec,
                  pl.BlockSpec((1, NA_SLABS * HEADS_PER_SLAB, tq, NA_KROWS * GRID_W), bias_map)],
        out_specs=pl.BlockSpec((tq, w), lambda b, s, r: (b * nrb + r, s)),
        out_shape=jax.ShapeDtypeStruct((batch * seq, NA_WIDTH), F32),
        scratch_shapes=[pltpu.VMEM((seq, w), BF16), pltpu.VMEM((seq, w), BF16),
                        pltpu.VMEM((past, w), BF16), pltpu.VMEM((past, w), BF16)],
        compiler_params=_cparams(("parallel", "parallel", "arbitrary")),
        name="attn_na",
    )(proj, proj, proj, cache_k, cache_v, bias)


def _pair_halves(a, b, q, low):
    if q == 0:
        return jnp.where(low, a, pltpu.roll(b, HALF_LANES, 1))
    return jnp.where(low, pltpu.roll(a, HALF_LANES, 1), b)


def _s5_prep_kernel(lr_ref, li_ref, ld_ref, btr_ref, bti_ref, cr_ref, ci_ref,
                    e_ref, m_ref, f_ref, at_ref, *, seg_tokens):
    T = S5_CHUNK
    ns = SLAB_STATE
    rb = lax.broadcasted_iota(jnp.int32, (LANES, ns), 0) // S5_GROUP_CH
    cb = lax.broadcasted_iota(jnp.int32, (LANES, ns), 1) // S5_STATE
    mask_b = rb == cb

    def powers(ldr, ldi, n):
        mag = jnp.exp(ldr * float(n))
        return mag * jnp.cos(ldi * float(n)), mag * jnp.sin(ldi * float(n))

    hl, hs = HALF_LANES, HALF_STATE
    low64 = lax.broadcasted_iota(jnp.int32, (hl, LANES), 1) < hl
    kt = [[None] * T, [None] * T]
    for d in range(2):
        lam_re, lam_im = lr_ref[0, d], li_ref[0, d]
        dt = jnp.exp(ld_ref[0, d])
        ldr, ldi = lam_re * dt, lam_im * dt
        a_re, a_im = powers(ldr, ldi, 1)
        mag2 = lam_re * lam_re + lam_im * lam_im
        f_re = ((a_re - 1.0) * lam_re + a_im * lam_im) / mag2
        f_im = (a_im * lam_re - (a_re - 1.0) * lam_im) / mag2
        btr = jnp.where(mask_b, btr_ref[0, d], 0.0)
        bti = jnp.where(mask_b, bti_ref[0, d], 0.0)
        cr = jnp.where(mask_b, cr_ref[0, d], 0.0)
        ci = jnp.where(mask_b, ci_ref[0, d], 0.0)
        ctr, cti = cr.T, ci.T
        e_res, e_ims = [], []
        for n in range(T):
            an_re, an_im = powers(ldr, ldi, n)
            w_re = f_re * an_re - f_im * an_im
            w_im = f_re * an_im + f_im * an_re
            e_re = btr * w_re - bti * w_im
            e_im = btr * w_im + bti * w_re
            t = T - 1 - n if d == 0 else n
            for q in range(2):
                rows, cols = slice(q * hl, (q + 1) * hl), slice(q * hs, (q + 1) * hs)
                e_ref[0, q, t * hl:(t + 1) * hl, 2 * d * hs:(2 * d + 1) * hs] = e_re[rows, cols].astype(BF16)
                e_ref[0, q, t * hl:(t + 1) * hl, (2 * d + 1) * hs:(2 * d + 2) * hs] = e_im[rows, cols].astype(BF16)
            e_res.append(e_re)
            e_ims.append(e_im)
        k_all = _dot3(jnp.concatenate(e_res, axis=0), ctr) - _dot3(jnp.concatenate(e_ims, axis=0), cti)
        for n in range(T):
            kt[d][n] = k_all[n * LANES:(n + 1) * LANES]
        coef = []
        for tp in range(T):
            n = tp + 1 if d == 0 else T - tp
            an_re, an_im = powers(ldr, ldi, n)
            coef.append((cr * an_re - ci * an_im, -(cr * an_im + ci * an_re)))
        for q in range(2):
            rows, cols = slice(q * hl, (q + 1) * hl), slice(q * hs, (q + 1) * hs)
            for pr in range(T // 2):
                for ri in range(2):
                    tile = jnp.concatenate([coef[2 * pr][ri][rows, cols], coef[2 * pr + 1][ri][rows, cols]], axis=0)
                    f_ref[0, q, (2 * d + ri) * hs:(2 * d + ri + 1) * hs, pr * LANES:(pr + 1) * LANES] = (
                        tile.T.astype(BF16))
        at_re, at_im = powers(ldr, ldi, T)
        at_ref[0, 2 * d:2 * d + 1, :] = at_re
        at_ref[0, 2 * d + 1:2 * d + 2, :] = at_im
        as_re, as_im = powers(ldr, ldi, seg_tokens)
        at_ref[0, 4 + 2 * d:5 + 2 * d, :] = as_re
        at_ref[0, 5 + 2 * d:6 + 2 * d, :] = as_im
    def lag_block(t, tp):
        if tp > t:
            return kt[0][tp - t]
        if tp < t:
            return kt[1][t - tp]
        return kt[0][0] + kt[1][0]

    for q in range(2):
        rows = slice(q * hl, (q + 1) * hl)
        for t in range(T):
            for pr in range(T // 2):
                tile = _pair_halves(lag_block(t, 2 * pr)[rows], lag_block(t, 2 * pr + 1)[rows], q, low64)
                m_ref[0, q, t * hl:(t + 1) * hl, pr * LANES:(pr + 1) * LANES] = tile.astype(BF16)


def _s5_prep(lam_re, lam_im, log_dt, b_re, b_im, c_re, c_im, seg_tokens):
    T, S, G8, ns = S5_CHUNK, N_GROUP_SLABS, GROUPS_PER_SLAB, SLAB_STATE

    def state_vec(v):
        return v.reshape(2, S, ns).transpose(1, 0, 2)

    lr, li = state_vec(lam_re), state_vec(lam_im)
    ld = state_vec(jnp.broadcast_to(log_dt[:, :, None], lam_re.shape))

    def bt(v):
        v = v.reshape(2, S, G8, S5_STATE, S5_GROUP_CH).transpose(1, 0, 2, 4, 3).reshape(S, 2, LANES, S5_STATE)
        return jnp.tile(v, (1, 1, 1, G8))

    def crow(v):
        v = v.reshape(2, S, G8, S5_GROUP_CH, S5_STATE).transpose(1, 0, 2, 3, 4).reshape(S, 2, LANES, S5_STATE)
        return jnp.tile(v, (1, 1, 1, G8))

    row = lambda v: v[:, :, None, :]
    spec_row = pl.BlockSpec((1, 2, 1, ns), lambda j: (j, 0, 0, 0))
    spec_bt = pl.BlockSpec((1, 2, LANES, ns), lambda j: (j, 0, 0, 0))
    return pl.pallas_call(
        functools.partial(_s5_prep_kernel, seg_tokens=seg_tokens),
        grid=(S,),
        in_specs=[spec_row, spec_row, spec_row, spec_bt, spec_bt, spec_bt, spec_bt],
        out_specs=[pl.BlockSpec((1, 2, S5_HALF_IN, S5_HALF_STATES), lambda j: (j, 0, 0, 0)),
                   pl.BlockSpec((1, 2, S5_HALF_IN, S5_HALF_IN), lambda j: (j, 0, 0, 0)),
                   pl.BlockSpec((1, 2, S5_HALF_STATES, S5_HALF_IN), lambda j: (j, 0, 0, 0)),
                   pl.BlockSpec((1, 8, ns), lambda j: (j, 0, 0))],
        out_shape=[jax.ShapeDtypeStruct((S, 2, S5_HALF_IN, S5_HALF_STATES), BF16),
                   jax.ShapeDtypeStruct((S, 2, S5_HALF_IN, S5_HALF_IN), BF16),
                   jax.ShapeDtypeStruct((S, 2, S5_HALF_STATES, S5_HALF_IN), BF16),
                   jax.ShapeDtypeStruct((S, 8, ns), F32)],
        compiler_params=_cparams(("parallel",)),
        name="s5_prep",
    )(row(lr), row(li), row(ld), bt(b_re), bt(b_im), crow(c_re), crow(c_im))


def _s5_kernel(u_ref, h0_ref, e_ref, m_ref, f_ref, at_ref, d_ref, y_ref, fin_ref,
               us_ref, ubf_ref, st_ref, ys_ref, *, bt, nseg, seq):
    T = S5_CHUNK
    ns = SLAB_STATE
    npar = bt * nseg
    seg = seq // nseg
    nc = seg // T
    for p in range(npar):
        for t in range(T):
            us_ref[t, pl.ds(p, nc, stride=npar), :] = u_ref[pl.ds(p * seg + t, nc, stride=T), :]
    hl, hs = HALF_LANES, HALF_STATE
    low = lax.broadcasted_iota(jnp.int32, (us_ref.shape[1], LANES), 1) < hl
    for q in range(2):
        for pr in range(T // 2):
            pair = _pair_halves(us_ref[2 * pr], us_ref[2 * pr + 1], q, low)
            ubf_ref[q, :, pr * LANES:(pr + 1) * LANES] = pair.astype(BF16)
        local = jnp.dot(ubf_ref[q], e_ref[0, q], preferred_element_type=F32)
        for k in range(4):
            st_ref[:, k * ns + q * hs:k * ns + (q + 1) * hs] = local[:, k * hs:(k + 1) * hs]

    at = at_ref[0]
    a_fr, a_fi, a_br, a_bi = at[0:1], at[1:2], at[2:3], at[3:4]

    def scan(init, keep_entering):
        def step(i, carry):
            c_fr, c_fi, c_br, c_bi = carry
            rf = pl.multiple_of(i * npar, npar)
            rbk = pl.multiple_of((nc - 1 - i) * npar, npar)
            l_fr = st_ref[pl.ds(rf, npar), 0:ns]
            l_fi = st_ref[pl.ds(rf, npar), ns:2 * ns]
            l_br = st_ref[pl.ds(rbk, npar), 2 * ns:3 * ns]
            l_bi = st_ref[pl.ds(rbk, npar), 3 * ns:4 * ns]
            if keep_entering:
                st_ref[pl.ds(rf, npar), 0:ns] = c_fr
                st_ref[pl.ds(rf, npar), ns:2 * ns] = c_fi
                st_ref[pl.ds(rbk, npar), 2 * ns:3 * ns] = c_br
                st_ref[pl.ds(rbk, npar), 3 * ns:4 * ns] = c_bi
            return (a_fr * c_fr - a_fi * c_fi + l_fr, a_fr * c_fi + a_fi * c_fr + l_fi,
                    a_br * c_br - a_bi * c_bi + l_br, a_br * c_bi + a_bi * c_br + l_bi)

        return lax.fori_loop(0, nc, step, init)

    h0 = h0_ref[0, 0]
    h0 = [h0[:, k * ns:(k + 1) * ns] for k in range(4)]
    if nseg == 1:
        enter = tuple(h0)
    else:
        zero = jnp.zeros((npar, ns), F32)
        z_fr, z_fi, z_br, z_bi = scan((zero, zero, zero, zero), False)
        s_fr, s_fi, s_br, s_bi = at[4:5], at[5:6], at[6:7], at[7:8]
        rows_f, rows_b = [None] * npar, [None] * npar
        for b in range(bt):
            c_re, c_im = h0[0][b:b + 1], h0[1][b:b + 1]
            for s in range(nseg):
                p = b * nseg + s
                rows_f[p] = (c_re, c_im)
                c_re, c_im = (s_fr * c_re - s_fi * c_im + z_fr[p:p + 1], s_fr * c_im + s_fi * c_re + z_fi[p:p + 1])
            c_re, c_im = h0[2][b:b + 1], h0[3][b:b + 1]
            for s in reversed(range(nseg)):
                p = b * nseg + s
                rows_b[p] = (c_re, c_im)
                c_re, c_im = (s_br * c_re - s_bi * c_im + z_br[p:p + 1], s_br * c_im + s_bi * c_re + z_bi[p:p + 1])
        enter = (jnp.concatenate([r[0] for r in rows_f], axis=0), jnp.concatenate([r[1] for r in rows_f], axis=0),
                 jnp.concatenate([r[0] for r in rows_b], axis=0), jnp.concatenate([r[1] for r in rows_b], axis=0))
    fin = scan(enter, True)
    for b in range(bt):
        last_f, last_b = b * nseg + nseg - 1, b * nseg
        fin_ref[0, 0, b:b + 1, 0:ns] = fin[0][last_f:last_f + 1]
        fin_ref[0, 0, b:b + 1, ns:2 * ns] = fin[1][last_f:last_f + 1]
        fin_ref[0, 0, b:b + 1, 2 * ns:3 * ns] = fin[2][last_b:last_b + 1]
        fin_ref[0, 0, b:b + 1, 3 * ns:4 * ns] = fin[3][last_b:last_b + 1]

    yq = []
    for q in range(2):
        entering = jnp.concatenate([st_ref[:, k * ns + q * hs:k * ns + (q + 1) * hs] for k in range(4)], axis=1)
        yq.append(jnp.dot(ubf_ref[q], m_ref[0, q], preferred_element_type=F32)
                  + jnp.dot(entering.astype(BF16), f_ref[0, q], preferred_element_type=F32))
    for pr in range(T // 2):
        y0 = yq[0][:, pr * LANES:(pr + 1) * LANES]
        y1 = yq[1][:, pr * LANES:(pr + 1) * LANES]
        ys_ref[2 * pr] = jnp.where(low, y0, pltpu.roll(y1, hl, 1)) + d_ref[0] * us_ref[2 * pr]
        ys_ref[2 * pr + 1] = jnp.where(low, pltpu.roll(y0, hl, 1), y1) + d_ref[0] * us_ref[2 * pr + 1]
    for p in range(npar):
        for t in range(T):
            y_ref[pl.ds(p * seg + t, nc, stride=T), :] = ys_ref[t, pl.ds(p, nc, stride=npar), :]


def _s5(proj, u_col, h0, e, m, f, at, d_skip, batch, seq, bt, nseg):
    T, S = S5_CHUNK, N_GROUP_SLABS
    rows = bt * seq // T
    const = lambda j, g: (j, 0, 0)
    const4 = lambda j, g: (j, 0, 0, 0)
    y, fin = pl.pallas_call(
        functools.partial(_s5_kernel, bt=bt, nseg=nseg, seq=seq),
        grid=(S, batch // bt),
        in_specs=[pl.BlockSpec((bt * seq, LANES), lambda j, g: (g, u_col + j)),
                  pl.BlockSpec((1, 1, bt, SLAB_COLS), lambda j, g: (j, g, 0, 0)),
                  pl.BlockSpec((1, 2, S5_HALF_IN, S5_HALF_STATES), const4),
                  pl.BlockSpec((1, 2, S5_HALF_IN, S5_HALF_IN), const4),
                  pl.BlockSpec((1, 2, S5_HALF_STATES, S5_HALF_IN), const4),
                  pl.BlockSpec((1, 8, SLAB_STATE), const),
                  pl.BlockSpec((1, 1, LANES), const)],
        out_specs=[pl.BlockSpec((bt * seq, LANES), lambda j, g: (g, j)),
                   pl.BlockSpec((1, 1, bt, SLAB_COLS), lambda j, g: (j, g, 0, 0))],
        out_shape=[jax.ShapeDtypeStruct((batch * seq, S5_WIDTH), F32),
                   jax.ShapeDtypeStruct((S, batch // bt, bt, SLAB_COLS), F32)],
        scratch_shapes=[pltpu.VMEM((T, rows, LANES), F32), pltpu.VMEM((2, rows, S5_HALF_IN), BF16),
                        pltpu.VMEM((rows, SLAB_COLS), F32), pltpu.VMEM((T, rows, LANES), F32)],
        compiler_params=_cparams(("parallel", "arbitrary")),
        name="s5",
    )(proj, h0.reshape(S, batch // bt, bt, SLAB_COLS), e, m, f, at, d_skip.reshape(S, 1, LANES))
    return y, fin.reshape(S, batch, SLAB_COLS)


def _state_to_slabs(state):
    b = state.shape[0]
    s = state.reshape(b, 2, 2, N_GROUP_SLABS, SLAB_STATE).transpose(3, 0, 1, 2, 4)
    return s.reshape(N_GROUP_SLABS, b, SLAB_COLS)


def _slabs_to_state(slabs):
    b = slabs.shape[1]
    s = slabs.reshape(N_GROUP_SLABS, b, 2, 2, SLAB_STATE).transpose(1, 2, 3, 0, 4)
    return s.reshape(b, 2, 2, S5_GROUPS, S5_STATE)


def _merge_kernel(x_ref, mods_ref, ya_ref, yb_ref, ga_ref, gb_ref, wglu_ref, wua_ref, wub_ref, wo_ref, o_ref):
    yb = jax.nn.gelu(yb_ref[...])
    glu = jnp.dot(yb.astype(BF16), wglu_ref[...], preferred_element_type=F32)
    yb = yb * jax.nn.sigmoid(glu)
    a = jnp.dot(ya_ref[...].astype(BF16), wua_ref[...], preferred_element_type=F32)
    b = jnp.dot(yb.astype(BF16), wub_ref[...], preferred_element_type=F32)
    merged = jax.nn.sigmoid(ga_ref[...]) * a + jax.nn.sigmoid(gb_ref[...]) * b
    o = jnp.dot(merged.astype(BF16), wo_ref[...], preferred_element_type=F32)
    o_ref[...] = x_ref[...] + mods_ref[0][5:6] * o


def _merge(x, mods3, row0, cond_tokens, ya, yb, proj, w_glu, w_up_a, w_up_b, w_out):
    n_tok, d = x.shape
    tm = min(256, n_tok)
    assert cond_tokens % tm == 0
    tiles_per_row = cond_tokens // tm
    gate_col = (3 * NA_WIDTH + S5_WIDTH) // d
    assert gate_col * d == 3 * NA_WIDTH + S5_WIDTH
    resident = functools.partial(pl.BlockSpec, pipeline_mode=pl.Buffered(1))
    return pl.pallas_call(
        _merge_kernel,
        grid=(n_tok // tm,),
        in_specs=[pl.BlockSpec((tm, d), lambda i: (i, 0)),
                  pl.BlockSpec((1, N_MOD, d), lambda i: (row0 + i // tiles_per_row, 0, 0)),
                  pl.BlockSpec((tm, NA_WIDTH), lambda i: (i, 0)),
                  pl.BlockSpec((tm, S5_WIDTH), lambda i: (i, 0)),
                  pl.BlockSpec((tm, d), lambda i: (i, gate_col)),
                  pl.BlockSpec((tm, d), lambda i: (i, gate_col + 1)),
                  resident((S5_WIDTH, S5_WIDTH), lambda i: (0, 0)),
                  resident((NA_WIDTH, d), lambda i: (0, 0)),
                  resident((S5_WIDTH, d), lambda i: (0, 0)),
                  resident((d, d), lambda i: (0, 0))],
        out_specs=pl.BlockSpec((tm, d), lambda i: (i, 0)),
        out_shape=jax.ShapeDtypeStruct((n_tok, d), F32),
        compiler_params=_cparams(("parallel",)),
        name="merge",
    )(x, mods3, ya, yb, proj, proj, w_glu, w_up_a, w_up_b, w_out)


def kernel(x_prompt, x_sample, cache_k, cache_v, state_ssm, c, c_ctx, w_ada, b_ada, norm_g, ffn_in, ffn_out, w_in, rpb, s5_lam_re, s5_lam_im, s5_log_dt, s5_b_re, s5_b_im, s5_c_re, s5_c_im, s5_d, w_glu, w_up_a, w_up_b, w_out, final_g):
    depth = w_ada.shape[0]
    pb, pl_, d = x_prompt.shape
    sb, sl, _ = x_sample.shape
    u_col = 3 * NA_WIDTH // LANES
    mod_rows = 8 * ((1 + sb + 7) // 8)

    xp = x_prompt.reshape(pb * pl_, d)
    xs = x_sample.reshape(sb * sl, d)
    cond = jnp.concatenate([c_ctx[None, :], c, jnp.zeros((mod_rows - 1 - sb, d), F32)], axis=0)
    ffn_in_bf, ffn_out_bf = ffn_in.astype(BF16), ffn_out.astype(BF16)
    new_k, new_v, new_s = [], [], []
    for l in range(depth):
        last = l == depth - 1
        mods3 = _mods(cond, w_ada[l], b_ada[l]).reshape(mod_rows, N_MOD, d)
        w_in_bf = w_in[l].astype(BF16)
        w_glu_bf, w_up_a_bf, w_up_b_bf, w_out_bf = (w[l].astype(BF16) for w in (w_glu, w_up_a, w_up_b, w_out))
        e, m, f, at = _s5_prep(s5_lam_re[l], s5_lam_im[l], s5_log_dt[l], s5_b_re[l], s5_b_im[l],
                               s5_c_re[l], s5_c_im[l], sl // S5_SEGMENTS)
        bias = _na_bias_tables(rpb[l], sl // GRID_W)

        ptok = pb * pl_
        xp = _ffn(xp, mods3, 0, ptok, norm_g[l, 0], final_g, ffn_in_bf, ffn_out_bf, l, 0, 0, False)
        proj_p = _proj(xp, mods3, 0, ptok, norm_g[l, 1], w_in_bf)
        ya_p, k_p, v_p = _attn_ctx(proj_p, pb, pl_)
        yb_p, fin_p = _s5(proj_p, u_col, jnp.zeros((N_GROUP_SLABS, pb, SLAB_COLS), F32), e, m, f, at,
                          s5_d[l], pb, pl_, math.gcd(pb, 16), 1)
        xp = _merge(xp, mods3, 0, ptok, ya_p, yb_p, proj_p, w_glu_bf, w_up_a_bf, w_up_b_bf, w_out_bf)
        xp = _ffn(xp, mods3, 0, ptok, norm_g[l, 2], final_g, ffn_in_bf, ffn_out_bf, l, 1, 6, last)
        new_k.append(k_p)
        new_v.append(v_p)
        new_s.append(_slabs_to_state(fin_p))

        xs = _ffn(xs, mods3, 1, sl, norm_g[l, 0], final_g, ffn_in_bf, ffn_out_bf, l, 0, 0, False)
        proj_s = _proj(xs, mods3, 1, sl, norm_g[l, 1], w_in_bf)
        ya_s = _attn_na(proj_s, cache_k, cache_v, l, bias, sb, sl)
        yb_s, _ = _s5(proj_s, u_col, _state_to_slabs(state_ssm[:, l]), e, m, f, at, s5_d[l], sb, sl, 1, S5_SEGMENTS)
        xs = _merge(xs, mods3, 1, sl, ya_s, yb_s, proj_s, w_glu_bf, w_up_a_bf, w_up_b_bf, w_out_bf)
        xs = _ffn(xs, mods3, 1, sl, norm_g[l, 2], final_g, ffn_in_bf, ffn_out_bf, l, 1, 6, last)

    y_prompt = xp.reshape(pb, pl_, d)
    y_sample = xs.reshape(sb, sl, d)
    return (y_prompt, y_sample, jnp.stack(new_k, axis=1), jnp.stack(new_v, axis=1), jnp.stack(new_s, axis=1))
```

```python
import functools
import math

import jax
import jax.numpy as jnp
from jax import lax
from jax.experimental import pallas as pl
from jax.experimental.pallas import tpu as pltpu

F32 = jnp.float32
BF16 = jnp.bfloat16

GRID_W = 64
NA_HEADS = 16
NA_HEAD_DIM = 64
NA_WIDTH = NA_HEADS * NA_HEAD_DIM
WIN_R = 8
WIN_C = 16
S5_GROUP_CH = 16
S5_WIDTH = 1024
S5_GROUPS = S5_WIDTH // S5_GROUP_CH
S5_STATE = 64
N_MOD = 9
EPS = 1e-6

LANES = 128
HEADS_PER_SLAB = LANES // NA_HEAD_DIM
N_HEAD_SLABS = NA_WIDTH // LANES
GROUPS_PER_SLAB = LANES // S5_GROUP_CH
N_GROUP_SLABS = S5_WIDTH // LANES
SLAB_STATE = GROUPS_PER_SLAB * S5_STATE
SLAB_COLS = 4 * SLAB_STATE
HALF_LANES = LANES // 2
HALF_STATE = SLAB_STATE // 2
S5_CHUNK = 8
S5_HALF_IN = S5_CHUNK * HALF_LANES
S5_HALF_STATES = 4 * HALF_STATE
S5_SEGMENTS = 8
NA_QROWS = 4
NA_KROWS = NA_QROWS + WIN_R
NEG_BIAS = -1e30
LOG2E = 1.4426950408889634
VMEM_LIMIT = 56 * 1024 * 1024


def _cparams(sem, limit=VMEM_LIMIT):
    return pltpu.CompilerParams(dimension_semantics=sem, vmem_limit_bytes=limit)


def _silu(x):
    return x * jax.nn.sigmoid(x)


NORM_ROWS = 16
NORM_UNROLL = 8


def _norm_mod_store(x_ref, h_ref, g, shift, scale):
    gs = g * (1.0 + scale)

    def body(i, carry):
        r0 = pl.multiple_of(i * NORM_ROWS, NORM_ROWS)
        x = x_ref[pl.ds(r0, NORM_ROWS), :]
        r = lax.rsqrt(jnp.mean(x * x, axis=-1, keepdims=True) + EPS)
        h_ref[pl.ds(r0, NORM_ROWS), :] = ((x * r) * gs + shift).astype(BF16)
        return carry

    lax.fori_loop(0, x_ref.shape[0] // NORM_ROWS, body, 0, unroll=NORM_UNROLL)


def _split_bf16(x):
    hi = x.astype(BF16)
    return hi, (x - hi.astype(F32)).astype(BF16)


def _dot3(a, b):
    a_hi, a_lo = _split_bf16(a)
    b_hi, b_lo = _split_bf16(b)
    return (jnp.dot(a_hi, b_hi, preferred_element_type=F32)
            + (jnp.dot(a_lo, b_hi, preferred_element_type=F32) + jnp.dot(a_hi, b_lo, preferred_element_type=F32)))


def _mods_kernel(c_ref, w_ref, b_ref, o_ref):
    o_ref[...] = _dot3(_silu(c_ref[...]), w_ref[...]) + b_ref[...]


def _mods(cond, w_ada, b_ada):
    rows, d = cond.shape
    n = w_ada.shape[1]
    tn = math.gcd(1024, n)
    return pl.pallas_call(
        _mods_kernel,
        grid=(n // tn,),
        in_specs=[pl.BlockSpec((rows, d), lambda j: (0, 0)),
                  pl.BlockSpec((d, tn), lambda j: (0, j)),
                  pl.BlockSpec((1, tn), lambda j: (0, j))],
        out_specs=pl.BlockSpec((rows, tn), lambda j: (0, j)),
        out_shape=jax.ShapeDtypeStruct((rows, n), F32),
        compiler_params=_cparams(("arbitrary",)),
        name="mods",
    )(cond, w_ada, b_ada.reshape(1, n))


def _ffn_kernel(x_hbm, mods_ref, ng_ref, fg_ref, wg_ref, wu_ref, wo_ref, o_ref, xs_ref, h_ref, rs_ref, sem,
                *, mod_base, final_norm):
    i, j = pl.program_id(0), pl.program_id(1)
    tm = xs_ref.shape[0]

    def x_copy(tile):
        return pltpu.make_async_copy(x_hbm.at[pl.ds(tile * tm, tm), :], xs_ref, sem)

    @pl.when((i == 0) & (j == 0))
    def _():
        x_copy(0).start()

    @pl.when(j == 0)
    def _():
        x_copy(i).wait()
        m = mods_ref[0]
        gs = ng_ref[...] * (1.0 + m[mod_base + 1:mod_base + 2])
        shift = m[mod_base:mod_base + 1]

        def body(r, carry):
            r0 = pl.multiple_of(r * NORM_ROWS, NORM_ROWS)
            x = xs_ref[pl.ds(r0, NORM_ROWS), :]
            o_ref[pl.ds(r0, NORM_ROWS), :] = x
            rs = lax.rsqrt(jnp.mean(x * x, axis=-1, keepdims=True) + EPS)
            h_ref[pl.ds(r0, NORM_ROWS), :] = ((x * rs) * gs + shift).astype(BF16)
            return carry

        lax.fori_loop(0, tm // NORM_ROWS, body, 0, unroll=NORM_UNROLL)

    @pl.when((j == 1) & (i + 1 < pl.num_programs(0)))
    def _():
        x_copy(i + 1).start()

    h = h_ref[...]
    g = jnp.dot(h, wg_ref[...], preferred_element_type=F32)
    u = jnp.dot(h, wu_ref[...], preferred_element_type=F32)
    a = (_silu(g) * u).astype(BF16)
    half_gate = 0.5 * mods_ref[0][mod_base + 2:mod_base + 3]
    o_ref[...] += half_gate * jnp.dot(a, wo_ref[...], preferred_element_type=F32)

    if final_norm:
        @pl.when(j == pl.num_programs(1) - 1)
        def _():
            def body(r, carry):
                r0 = pl.multiple_of(r * NORM_ROWS, NORM_ROWS)
                y = o_ref[pl.ds(r0, NORM_ROWS), :]
                rs = lax.rsqrt(jnp.mean(y * y, axis=-1, keepdims=True) + EPS)
                rs_ref[pl.ds(r0, NORM_ROWS), :] = jnp.broadcast_to(rs, (NORM_ROWS, LANES))
                return carry

            lax.fori_loop(0, tm // NORM_ROWS, body, 0, unroll=NORM_UNROLL)
            for c in range(o_ref.shape[1] // LANES):
                cols = slice(c * LANES, (c + 1) * LANES)
                o_ref[:, cols] = (o_ref[:, cols] * rs_ref[...]) * fg_ref[:, cols]


FFN_TM = 1024
FFN_TF = 512


def _ffn(x, mods3, row0, cond_tokens, norm_g, final_g, w_in_bf, w_out_bf, layer, which, mod_base, final_norm):
    n_tok, d = x.shape
    d_ff = w_out_bf.shape[2]
    tm = min(FFN_TM, n_tok)
    tf = min(FFN_TF, d_ff)
    assert cond_tokens % tm == 0
    tiles_per_row = cond_tokens // tm
    assert d_ff // tf >= 2
    kern = functools.partial(_ffn_kernel, mod_base=mod_base, final_norm=final_norm)
    return pl.pallas_call(
        kern,
        grid=(n_tok // tm, d_ff // tf),
        in_specs=[pl.BlockSpec(memory_space=pl.ANY),
                  pl.BlockSpec((1, N_MOD, d), lambda i, j: (row0 + i // tiles_per_row, 0, 0)),
                  pl.BlockSpec((1, d), lambda i, j: (0, 0)),
                  pl.BlockSpec((1, d), lambda i, j: (0, 0)),
                  pl.BlockSpec((None, None, d, tf), lambda i, j: (layer, which, 0, j)),
                  pl.BlockSpec((None, None, d, tf), lambda i, j: (layer, which, 0, d_ff // tf + j)),
                  pl.BlockSpec((None, None, tf, d), lambda i, j: (layer, which, j, 0))],
        out_specs=pl.BlockSpec((tm, d), lambda i, j: (i, 0)),
        out_shape=jax.ShapeDtypeStruct((n_tok, d), F32),
        scratch_shapes=[pltpu.VMEM((tm, d), F32), pltpu.VMEM((tm, d), BF16), pltpu.VMEM((tm, LANES), F32),
                        pltpu.SemaphoreType.DMA(())],
        compiler_params=_cparams(("arbitrary", "arbitrary")),
        name="ffn",
    )(x, mods3, norm_g.reshape(1, d), final_g.reshape(1, d), w_in_bf, w_in_bf, w_out_bf)


def _proj_kernel(x_hbm, mods_ref, ng_ref, w_ref, o_ref, xs_ref, h_ref, sem):
    i, j = pl.program_id(0), pl.program_id(1)
    tm = xs_ref.shape[0]

    def x_copy(tile):
        return pltpu.make_async_copy(x_hbm.at[pl.ds(tile * tm, tm), :], xs_ref, sem)

    @pl.when((i == 0) & (j == 0))
    def _():
        x_copy(0).start()

    @pl.when(j == 0)
    def _():
        x_copy(i).wait()
        m = mods_ref[0]
        _norm_mod_store(xs_ref, h_ref, ng_ref[...], m[3:4], m[4:5])

    @pl.when((j == 1) & (i + 1 < pl.num_programs(0)))
    def _():
        x_copy(i + 1).start()

    o_ref[...] = jnp.dot(h_ref[...], w_ref[...], preferred_element_type=F32)


def _proj(x, mods3, row0, cond_tokens, norm_g, w_bf):
    n_tok, d = x.shape
    n = w_bf.shape[1]
    tm = min(1024, n_tok)
    tn = math.gcd(2048, d)
    assert cond_tokens % tm == 0 and n // tn >= 2
    tiles_per_row = cond_tokens // tm
    return pl.pallas_call(
        _proj_kernel,
        grid=(n_tok // tm, n // tn),
        in_specs=[pl.BlockSpec(memory_space=pl.ANY),
                  pl.BlockSpec((1, N_MOD, d), lambda i, j: (row0 + i // tiles_per_row, 0, 0)),
                  pl.BlockSpec((1, d), lambda i, j: (0, 0)),
                  pl.BlockSpec((d, tn), lambda i, j: (0, j))],
        out_specs=pl.BlockSpec((tm, tn), lambda i, j: (i, j)),
        out_shape=jax.ShapeDtypeStruct((n_tok, n), F32),
        scratch_shapes=[pltpu.VMEM((tm, d), F32), pltpu.VMEM((tm, d), BF16), pltpu.SemaphoreType.DMA(())],
        compiler_params=_cparams(("arbitrary", "arbitrary")),
        name="proj",
    )(x, mods3, norm_g.reshape(1, d), w_bf)


def _head_mask(shape, hh):
    lane = lax.broadcasted_iota(jnp.int32, shape, 1)
    return (lane >= hh * NA_HEAD_DIM) & (lane < (hh + 1) * NA_HEAD_DIM)


def _nt_dot(a, b):
    return lax.dot_general(a, b, (((1,), (1,)), ((), ())), preferred_element_type=F32)


Q_SCALE = NA_HEAD_DIM ** -0.5 * LOG2E
CTX_SLABS = 8
NA_SLABS = 2
NA_BLOCKS = 2


def _ones_other_head(v, hh):
    return jnp.where(_head_mask(v.shape, hh), v, 1.0)


def _finish_heads(o0, o1):
    first = _head_mask(o0.shape, 0)
    num = jnp.where(first, o0, o1)
    den = pltpu.roll(jnp.where(first, o1, o0), NA_HEAD_DIM, 1)
    return num / den


def _attn_ctx_kernel(q_ref, k_ref, v_ref, o_ref, ko_ref, vo_ref):
    for sl in range(q_ref.shape[1] // LANES):
        cols = slice(sl * LANES, (sl + 1) * LANES)
        q = q_ref[:, cols] * Q_SCALE
        kf = k_ref[:, cols]
        vf = v_ref[:, cols]
        k = kf.astype(BF16)
        outs = []
        for hh in range(HEADS_PER_SLAB):
            head = slice(hh * NA_HEAD_DIM, (hh + 1) * NA_HEAD_DIM)
            ko_ref[0, sl * HEADS_PER_SLAB + hh] = kf[:, head]
            vo_ref[0, sl * HEADS_PER_SLAB + hh] = vf[:, head]
            qm = jnp.where(_head_mask(q.shape, hh), q, 0.0).astype(BF16)
            s = _nt_dot(qm, k)
            p = jnp.exp2(s - jnp.max(s, axis=-1, keepdims=True))
            outs.append(jnp.dot(p.astype(BF16), _ones_other_head(vf, hh).astype(BF16),
                                preferred_element_type=F32))
        o_ref[:, cols] = _finish_heads(*outs)


def _attn_ctx(proj, batch, seq):
    w = CTX_SLABS * LANES
    nsb = NA_WIDTH // w
    hps = CTX_SLABS * HEADS_PER_SLAB
    kv_spec = pl.BlockSpec((1, hps, seq, NA_HEAD_DIM), lambda b, s: (b, s, 0, 0))
    kv_shape = jax.ShapeDtypeStruct((batch, NA_HEADS, seq, NA_HEAD_DIM), F32)
    return pl.pallas_call(
        _attn_ctx_kernel,
        grid=(batch, nsb),
        in_specs=[pl.BlockSpec((seq, w), lambda b, s: (b, s)),
                  pl.BlockSpec((seq, w), lambda b, s: (b, nsb + s)),
                  pl.BlockSpec((seq, w), lambda b, s: (b, 2 * nsb + s))],
        out_specs=[pl.BlockSpec((seq, w), lambda b, s: (b, s)), kv_spec, kv_spec],
        out_shape=[jax.ShapeDtypeStruct((batch * seq, NA_WIDTH), F32), kv_shape, kv_shape],
        compiler_params=_cparams(("parallel", "parallel")),
        name="attn_ctx",
    )(proj, proj, proj)


def _attn_na_kernel(q_ref, k_ref, v_ref, ck_ref, cv_ref, bias_a_ref, bias_b_ref, o_ref,
                    kb_ref, vb_ref, ckb_ref, cvb_ref, *, rows):
    step = pl.program_id(2)
    nsl = q_ref.shape[1] // LANES

    @pl.when(step == 0)
    def _():
        kb_ref[...] = k_ref[...].astype(BF16)
        vb_ref[...] = v_ref[...].astype(BF16)
        for hd in range(ck_ref.shape[0]):
            lanes = slice(hd * NA_HEAD_DIM, (hd + 1) * NA_HEAD_DIM)
            ckb_ref[:, lanes] = ck_ref[hd].astype(BF16)
            cvb_ref[:, lanes] = cv_ref[hd].astype(BF16)

    heads = [(sl, hh) for sl in range(nsl) for hh in range(HEADS_PER_SLAB)]
    cols = [slice(sl * LANES, (sl + 1) * LANES) for sl in range(nsl)]
    tq = NA_QROWS * GRID_W
    for sub, bias_ref in enumerate((bias_a_ref, bias_b_ref)):
        rb = step * NA_BLOCKS + sub
        qrows = slice(sub * tq, (sub + 1) * tq)
        start = jnp.clip(rb * NA_QROWS - WIN_R // 2, 0, rows - NA_KROWS)
        off = pl.multiple_of(start * GRID_W, NA_QROWS * GRID_W)
        scores = []
        for sl, hh in heads:
            q = q_ref[qrows, cols[sl]] * Q_SCALE
            qm = jnp.where(_head_mask(q.shape, hh), q, 0.0).astype(BF16)
            s_loc = (_nt_dot(qm, kb_ref[pl.ds(off, NA_KROWS * GRID_W), cols[sl]])
                     + bias_ref[0, sl * HEADS_PER_SLAB + hh])
            s_ctx = _nt_dot(qm, ckb_ref[:, cols[sl]])
            scores.append((s_loc, s_ctx))
        probs = []
        for s_loc, s_ctx in scores:
            m = jnp.maximum(jnp.max(s_loc, axis=-1, keepdims=True), jnp.max(s_ctx, axis=-1, keepdims=True))
            p_loc = jnp.exp2(s_loc - m)
            p_ctx = jnp.exp2(s_ctx - m)
            l = jnp.sum(p_loc, axis=-1, keepdims=True) + jnp.sum(p_ctx, axis=-1, keepdims=True)
            probs.append((p_loc.astype(BF16), p_ctx.astype(BF16), l))
        outs = []
        for (sl, hh), (p_loc, p_ctx, l) in zip(heads, probs):
            o = (jnp.dot(p_loc, vb_ref[pl.ds(off, NA_KROWS * GRID_W), cols[sl]], preferred_element_type=F32)
                 + jnp.dot(p_ctx, cvb_ref[:, cols[sl]], preferred_element_type=F32))
            outs.append(o / l)
        for sl in range(nsl):
            first = _head_mask(outs[0].shape, 0)
            o_ref[qrows, cols[sl]] = jnp.where(first, outs[sl * HEADS_PER_SLAB], outs[sl * HEADS_PER_SLAB + 1])


def _na_bias_kernel(rpb_ref, o_ref, *, rows):
    shape = (GRID_W, LANES)
    qc = lax.broadcasted_iota(jnp.int32, shape, 0)
    lane = lax.broadcasted_iota(jnp.int32, shape, 1)
    kc = lane % GRID_W
    cs = jnp.clip(qc - WIN_C // 2, 0, GRID_W - WIN_C)
    valid_c = (kc >= cs) & (kc < cs + WIN_C)
    first_row = lane < GRID_W
    r = rpb_ref[0] * LOG2E
    for p, r0 in enumerate((0, NA_QROWS, rows - NA_QROWS)):
        start = min(max(r0 - WIN_R // 2, 0), rows - NA_KROWS)
        for qr in range(NA_QROWS):
            q_row = r0 + qr
            ws = min(max(q_row - WIN_R // 2, 0), rows - WIN_R)
            for k2 in range(NA_KROWS // 2):
                halves = []
                for kk in range(2):
                    k_row = start + 2 * k2 + kk
                    if ws <= k_row < ws + WIN_R:
                        dr = k_row - q_row + WIN_R - 1
                        row = jnp.broadcast_to(r[dr:dr + 1, :], shape)
                        shift = (LANES - (WIN_C - 1) + GRID_W * kk) % LANES
                        halves.append(pltpu.roll(row, shift, 1, stride=1, stride_axis=0))
                    else:
                        halves.append(jnp.full(shape, NEG_BIAS, F32))
                tile = jnp.where(valid_c, jnp.where(first_row, halves[0], halves[1]), NEG_BIAS)
                o_ref[p, 0, qr * GRID_W:(qr + 1) * GRID_W, k2 * LANES:(k2 + 1) * LANES] = tile


def _na_bias_tables(rpb, rows):
    h, nr, nc = rpb.shape
    rpb_pad = jnp.zeros((h, 16, LANES), F32).at[:, :nr, :nc].set(rpb)
    tq, tk = NA_QROWS * GRID_W, NA_KROWS * GRID_W
    return pl.pallas_call(
        functools.partial(_na_bias_kernel, rows=rows),
        grid=(h,),
        in_specs=[pl.BlockSpec((1, 16, LANES), lambda i: (i, 0, 0))],
        out_specs=pl.BlockSpec((3, 1, tq, tk), lambda i: (0, i, 0, 0)),
        out_shape=jax.ShapeDtypeStruct((3, h, tq, tk), F32),
        compiler_params=_cparams(("parallel",)),
        name="na_bias",
    )(rpb_pad)


def _attn_na(proj, cache_k, cache_v, layer, bias, batch, seq):
    past = cache_k.shape[3]
    rows = seq // GRID_W
    assert rows >= NA_KROWS and rows % (NA_QROWS * NA_BLOCKS) == 0
    nst = rows // (NA_QROWS * NA_BLOCKS)
    tq = NA_BLOCKS * NA_QROWS * GRID_W

    def bias_a_map(b, s, r):
        return ((r > 0).astype(jnp.int32), s, 0, 0)

    def bias_b_map(b, s, r):
        return (1 + (r == nst - 1).astype(jnp.int32), s, 0, 0)

    w = NA_SLABS * LANES
    nsb = NA_WIDTH // w
    hps = NA_SLABS * HEADS_PER_SLAB
    ctx_spec = pl.BlockSpec((None, None, hps, past, NA_HEAD_DIM), lambda b, s, r: (b, layer, s, 0, 0))
    bias_block = (1, hps, NA_QROWS * GRID_W, NA_KROWS * GRID_W)
    return pl.pallas_call(
        functools.partial(_attn_na_kernel, rows=rows),
        grid=(batch, nsb, nst),
        in_specs=[pl.BlockSpec((tq, w), lambda b, s, r: (b * nst + r, s)),
                  pl.BlockSpec((seq, w), lambda b, s, r: (b, nsb + s)),
                  pl.BlockSpec((seq, w), lambda b, s, r: (b, 2 * nsb + s)),
                  ctx_spec, ctx_spec,
                  pl.BlockSpec(bias_block, bias_a_map), pl.BlockSpec(bias_block, bias_b_map)],
        out_specs=pl.BlockSpec((tq, w), lambda b, s, r: (b * nst + r, s)),
        out_shape=jax.ShapeDtypeStruct((batch * seq, NA_WIDTH), F32),
        scratch_shapes=[pltpu.VMEM((seq, w), BF16), pltpu.VMEM((seq, w), BF16),
                        pltpu.VMEM((past, w), BF16), pltpu.VMEM((past, w), BF16)],
        compiler_params=_cparams(("parallel", "parallel", "arbitrary")),
        name="attn_na",
    )(proj, proj, proj, cache_k, cache_v, bias, bias)


def _pair_halves(a, b, q, low):
    if q == 0:
        return jnp.where(low, a, pltpu.roll(b, HALF_LANES, 1))
    return jnp.where(low, pltpu.roll(a, HALF_LANES, 1), b)


def _s5_prep_kernel(lr_ref, li_ref, ld_ref, btr_ref, bti_ref, cr_ref, ci_ref,
                    e_ref, m_ref, f_ref, at_ref, *, seg_tokens):
    T = S5_CHUNK
    ns = SLAB_STATE
    rb = lax.broadcasted_iota(jnp.int32, (LANES, ns), 0) // S5_GROUP_CH
    cb = lax.broadcasted_iota(jnp.int32, (LANES, ns), 1) // S5_STATE
    mask_b = rb == cb

    def powers(ldr, ldi, n):
        mag = jnp.exp(ldr * float(n))
        return mag * jnp.cos(ldi * float(n)), mag * jnp.sin(ldi * float(n))

    hl, hs = HALF_LANES, HALF_STATE
    low64 = lax.broadcasted_iota(jnp.int32, (hl, LANES), 1) < hl
    kt = [[None] * T, [None] * T]
    for d in range(2):
        lam_re, lam_im = lr_ref[0, d], li_ref[0, d]
        dt = jnp.exp(ld_ref[0, d])
        ldr, ldi = lam_re * dt, lam_im * dt
        a_re, a_im = powers(ldr, ldi, 1)
        mag2 = lam_re * lam_re + lam_im * lam_im
        f_re = ((a_re - 1.0) * lam_re + a_im * lam_im) / mag2
        f_im = (a_im * lam_re - (a_re - 1.0) * lam_im) / mag2
        btr = jnp.where(mask_b, btr_ref[0, d], 0.0)
        bti = jnp.where(mask_b, bti_ref[0, d], 0.0)
        cr = jnp.where(mask_b, cr_ref[0, d], 0.0)
        ci = jnp.where(mask_b, ci_ref[0, d], 0.0)
        ctr, cti = cr.T, ci.T
        e_res, e_ims = [], []
        for n in range(T):
            an_re, an_im = powers(ldr, ldi, n)
            w_re = f_re * an_re - f_im * an_im
            w_im = f_re * an_im + f_im * an_re
            e_re = btr * w_re - bti * w_im
            e_im = btr * w_im + bti * w_re
            t = T - 1 - n if d == 0 else n
            for q in range(2):
                rows, cols = slice(q * hl, (q + 1) * hl), slice(q * hs, (q + 1) * hs)
                e_ref[0, q, t * hl:(t + 1) * hl, 2 * d * hs:(2 * d + 1) * hs] = e_re[rows, cols].astype(BF16)
                e_ref[0, q, t * hl:(t + 1) * hl, (2 * d + 1) * hs:(2 * d + 2) * hs] = e_im[rows, cols].astype(BF16)
            e_res.append(e_re)
            e_ims.append(e_im)
        k_all = _dot3(jnp.concatenate(e_res, axis=0), ctr) - _dot3(jnp.concatenate(e_ims, axis=0), cti)
        for n in range(T):
            kt[d][n] = k_all[n * LANES:(n + 1) * LANES]
        coef = []
        for tp in range(T):
            n = tp + 1 if d == 0 else T - tp
            an_re, an_im = powers(ldr, ldi, n)
            coef.append((cr * an_re - ci * an_im, -(cr * an_im + ci * an_re)))
        for q in range(2):
            rows, cols = slice(q * hl, (q + 1) * hl), slice(q * hs, (q + 1) * hs)
            for pr in range(T // 2):
                for ri in range(2):
                    tile = jnp.concatenate([coef[2 * pr][ri][rows, cols], coef[2 * pr + 1][ri][rows, cols]], axis=0)
                    f_ref[0, q, (2 * d + ri) * hs:(2 * d + ri + 1) * hs, pr * LANES:(pr + 1) * LANES] = (
                        tile.T.astype(BF16))
        at_re, at_im = powers(ldr, ldi, T)
        at_ref[0, 2 * d:2 * d + 1, :] = at_re
        at_ref[0, 2 * d + 1:2 * d + 2, :] = at_im
        as_re, as_im = powers(ldr, ldi, seg_tokens)
        at_ref[0, 4 + 2 * d:5 + 2 * d, :] = as_re
        at_ref[0, 5 + 2 * d:6 + 2 * d, :] = as_im
    def lag_block(t, tp):
        if tp > t:
            return kt[0][tp - t]
        if tp < t:
            return kt[1][t - tp]
        return kt[0][0] + kt[1][0]

    for q in range(2):
        rows = slice(q * hl, (q + 1) * hl)
        for t in range(T):
            for pr in range(T // 2):
                tile = _pair_halves(lag_block(t, 2 * pr)[rows], lag_block(t, 2 * pr + 1)[rows], q, low64)
                m_ref[0, q, t * hl:(t + 1) * hl, pr * LANES:(pr + 1) * LANES] = tile.astype(BF16)


def _s5_prep(lam_re, lam_im, log_dt, b_re, b_im, c_re, c_im, seg_tokens):
    T, S, G8, ns = S5_CHUNK, N_GROUP_SLABS, GROUPS_PER_SLAB, SLAB_STATE

    def state_vec(v):
        return v.reshape(2, S, ns).transpose(1, 0, 2)

    lr, li = state_vec(lam_re), state_vec(lam_im)
    ld = state_vec(jnp.broadcast_to(log_dt[:, :, None], lam_re.shape))

    def bt(v):
        v = v.reshape(2, S, G8, S5_STATE, S5_GROUP_CH).transpose(1, 0, 2, 4, 3).reshape(S, 2, LANES, S5_STATE)
        return jnp.tile(v, (1, 1, 1, G8))

    def crow(v):
        v = v.reshape(2, S, G8, S5_GROUP_CH, S5_STATE).transpose(1, 0, 2, 3, 4).reshape(S, 2, LANES, S5_STATE)
        return jnp.tile(v, (1, 1, 1, G8))

    row = lambda v: v[:, :, None, :]
    spec_row = pl.BlockSpec((1, 2, 1, ns), lambda j: (j, 0, 0, 0))
    spec_bt = pl.BlockSpec((1, 2, LANES, ns), lambda j: (j, 0, 0, 0))
    return pl.pallas_call(
        functools.partial(_s5_prep_kernel, seg_tokens=seg_tokens),
        grid=(S,),
        in_specs=[spec_row, spec_row, spec_row, spec_bt, spec_bt, spec_bt, spec_bt],
        out_specs=[pl.BlockSpec((1, 2, S5_HALF_IN, S5_HALF_STATES), lambda j: (j, 0, 0, 0)),
                   pl.BlockSpec((1, 2, S5_HALF_IN, S5_HALF_IN), lambda j: (j, 0, 0, 0)),
                   pl.BlockSpec((1, 2, S5_HALF_STATES, S5_HALF_IN), lambda j: (j, 0, 0, 0)),
                   pl.BlockSpec((1, 8, ns), lambda j: (j, 0, 0))],
        out_shape=[jax.ShapeDtypeStruct((S, 2, S5_HALF_IN, S5_HALF_STATES), BF16),
                   jax.ShapeDtypeStruct((S, 2, S5_HALF_IN, S5_HALF_IN), BF16),
                   jax.ShapeDtypeStruct((S, 2, S5_HALF_STATES, S5_HALF_IN), BF16),
                   jax.ShapeDtypeStruct((S, 8, ns), F32)],
        compiler_params=_cparams(("parallel",)),
        name="s5_prep",
    )(row(lr), row(li), row(ld), bt(b_re), bt(b_im), crow(c_re), crow(c_im))


def _s5_kernel(u_ref, h0_ref, e_ref, m_ref, f_ref, at_ref, d_ref, y_ref, fin_ref,
               us_ref, ubf_ref, st_ref, ys_ref, *, bt, nseg, seq):
    T = S5_CHUNK
    ns = SLAB_STATE
    npar = bt * nseg
    seg = seq // nseg
    nc = seg // T
    for p in range(npar):
        for t in range(T):
            us_ref[t, pl.ds(p, nc, stride=npar), :] = u_ref[pl.ds(p * seg + t, nc, stride=T), :]
    hl, hs = HALF_LANES, HALF_STATE
    low = lax.broadcasted_iota(jnp.int32, (us_ref.shape[1], LANES), 1) < hl
    for q in range(2):
        for pr in range(T // 2):
            pair = _pair_halves(us_ref[2 * pr], us_ref[2 * pr + 1], q, low)
            ubf_ref[q, :, pr * LANES:(pr + 1) * LANES] = pair.astype(BF16)
        local = jnp.dot(ubf_ref[q], e_ref[0, q], preferred_element_type=F32)
        for k in range(4):
            st_ref[:, k * ns + q * hs:k * ns + (q + 1) * hs] = local[:, k * hs:(k + 1) * hs]

    at = at_ref[0]
    a_fr, a_fi, a_br, a_bi = at[0:1], at[1:2], at[2:3], at[3:4]

    def scan(init, keep_entering):
        def step(i, carry):
            c_fr, c_fi, c_br, c_bi = carry
            rf = pl.multiple_of(i * npar, npar)
            rbk = pl.multiple_of((nc - 1 - i) * npar, npar)
            l_fr = st_ref[pl.ds(rf, npar), 0:ns]
            l_fi = st_ref[pl.ds(rf, npar), ns:2 * ns]
            l_br = st_ref[pl.ds(rbk, npar), 2 * ns:3 * ns]
            l_bi = st_ref[pl.ds(rbk, npar), 3 * ns:4 * ns]
            if keep_entering:
                st_ref[pl.ds(rf, npar), 0:ns] = c_fr
                st_ref[pl.ds(rf, npar), ns:2 * ns] = c_fi
                st_ref[pl.ds(rbk, npar), 2 * ns:3 * ns] = c_br
                st_ref[pl.ds(rbk, npar), 3 * ns:4 * ns] = c_bi
            return (a_fr * c_fr - a_fi * c_fi + l_fr, a_fr * c_fi + a_fi * c_fr + l_fi,
                    a_br * c_br - a_bi * c_bi + l_br, a_br * c_bi + a_bi * c_br + l_bi)

        return lax.fori_loop(0, nc, step, init)

    h0 = h0_ref[0, 0]
    h0 = [h0[:, k * ns:(k + 1) * ns] for k in range(4)]
    if nseg == 1:
        enter = tuple(h0)
    else:
        zero = jnp.zeros((npar, ns), F32)
        z_fr, z_fi, z_br, z_bi = scan((zero, zero, zero, zero), False)
        s_fr, s_fi, s_br, s_bi = at[4:5], at[5:6], at[6:7], at[7:8]
        rows_f, rows_b = [None] * npar, [None] * npar
        for b in range(bt):
            c_re, c_im = h0[0][b:b + 1], h0[1][b:b + 1]
            for s in range(nseg):
                p = b * nseg + s
                rows_f[p] = (c_re, c_im)
                c_re, c_im = (s_fr * c_re - s_fi * c_im + z_fr[p:p + 1], s_fr * c_im + s_fi * c_re + z_fi[p:p + 1])
            c_re, c_im = h0[2][b:b + 1], h0[3][b:b + 1]
            for s in reversed(range(nseg)):
                p = b * nseg + s
                rows_b[p] = (c_re, c_im)
                c_re, c_im = (s_br * c_re - s_bi * c_im + z_br[p:p + 1], s_br * c_im + s_bi * c_re + z_bi[p:p + 1])
        enter = (jnp.concatenate([r[0] for r in rows_f], axis=0), jnp.concatenate([r[1] for r in rows_f], axis=0),
                 jnp.concatenate([r[0] for r in rows_b], axis=0), jnp.concatenate([r[1] for r in rows_b], axis=0))
    fin = scan(enter, True)
    for b in range(bt):
        last_f, last_b = b * nseg + nseg - 1, b * nseg
        fin_ref[0, 0, b:b + 1, 0:ns] = fin[0][last_f:last_f + 1]
        fin_ref[0, 0, b:b + 1, ns:2 * ns] = fin[1][last_f:last_f + 1]
        fin_ref[0, 0, b:b + 1, 2 * ns:3 * ns] = fin[2][last_b:last_b + 1]
        fin_ref[0, 0, b:b + 1, 3 * ns:4 * ns] = fin[3][last_b:last_b + 1]

    yq = []
    for q in range(2):
        entering = jnp.concatenate([st_ref[:, k * ns + q * hs:k * ns + (q + 1) * hs] for k in range(4)], axis=1)
        yq.append(jnp.dot(ubf_ref[q], m_ref[0, q], preferred_element_type=F32)
                  + jnp.dot(entering.astype(BF16), f_ref[0, q], preferred_element_type=F32))
    for pr in range(T // 2):
        y0 = yq[0][:, pr * LANES:(pr + 1) * LANES]
        y1 = yq[1][:, pr * LANES:(pr + 1) * LANES]
        ys_ref[2 * pr] = jnp.where(low, y0, pltpu.roll(y1, hl, 1)) + d_ref[0] * us_ref[2 * pr]
        ys_ref[2 * pr + 1] = jnp.where(low, pltpu.roll(y0, hl, 1), y1) + d_ref[0] * us_ref[2 * pr + 1]
    for p in range(npar):
        for t in range(T):
            y_ref[pl.ds(p * seg + t, nc, stride=T), :] = ys_ref[t, pl.ds(p, nc, stride=npar), :]


def _s5(proj, u_col, h0, e, m, f, at, d_skip, batch, seq, bt, nseg):
    T, S = S5_CHUNK, N_GROUP_SLABS
    rows = bt * seq // T
    const = lambda j, g: (j, 0, 0)
    const4 = lambda j, g: (j, 0, 0, 0)
    y, fin = pl.pallas_call(
        functools.partial(_s5_kernel, bt=bt, nseg=nseg, seq=seq),
        grid=(S, batch // bt),
        in_specs=[pl.BlockSpec((bt * seq, LANES), lambda j, g: (g, u_col + j)),
                  pl.BlockSpec((1, 1, bt, SLAB_COLS), lambda j, g: (j, g, 0, 0)),
                  pl.BlockSpec((1, 2, S5_HALF_IN, S5_HALF_STATES), const4),
                  pl.BlockSpec((1, 2, S5_HALF_IN, S5_HALF_IN), const4),
                  pl.BlockSpec((1, 2, S5_HALF_STATES, S5_HALF_IN), const4),
                  pl.BlockSpec((1, 8, SLAB_STATE), const),
                  pl.BlockSpec((1, 1, LANES), const)],
        out_specs=[pl.BlockSpec((bt * seq, LANES), lambda j, g: (g, j)),
                   pl.BlockSpec((1, 1, bt, SLAB_COLS), lambda j, g: (j, g, 0, 0))],
        out_shape=[jax.ShapeDtypeStruct((batch * seq, S5_WIDTH), F32),
                   jax.ShapeDtypeStruct((S, batch // bt, bt, SLAB_COLS), F32)],
        scratch_shapes=[pltpu.VMEM((T, rows, LANES), F32), pltpu.VMEM((2, rows, S5_HALF_IN), BF16),
                        pltpu.VMEM((rows, SLAB_COLS), F32), pltpu.VMEM((T, rows, LANES), F32)],
        compiler_params=_cparams(("parallel", "arbitrary")),
        name="s5",
    )(proj, h0.reshape(S, batch // bt, bt, SLAB_COLS), e, m, f, at, d_skip.reshape(S, 1, LANES))
    return y, fin.reshape(S, batch, SLAB_COLS)


def _state_to_slabs(state):
    b = state.shape[0]
    s = state.reshape(b, 2, 2, N_GROUP_SLABS, SLAB_STATE).transpose(3, 0, 1, 2, 4)
    return s.reshape(N_GROUP_SLABS, b, SLAB_COLS)


def _slabs_to_state(slabs):
    b = slabs.shape[1]
    s = slabs.reshape(N_GROUP_SLABS, b, 2, 2, SLAB_STATE).transpose(1, 2, 3, 0, 4)
    return s.reshape(b, 2, 2, S5_GROUPS, S5_STATE)


def _merge_kernel(x_ref, mods_ref, ya_ref, yb_ref, ga_ref, gb_ref, wglu_ref, wua_ref, wub_ref, wo_ref, o_ref):
    yb = jax.nn.gelu(yb_ref[...])
    glu = jnp.dot(yb.astype(BF16), wglu_ref[...], preferred_element_type=F32)
    yb = yb * jax.nn.sigmoid(glu)
    a = jnp.dot(ya_ref[...].astype(BF16), wua_ref[...], preferred_element_type=F32)
    b = jnp.dot(yb.astype(BF16), wub_ref[...], preferred_element_type=F32)
    merged = jax.nn.sigmoid(ga_ref[...]) * a + jax.nn.sigmoid(gb_ref[...]) * b
    o = jnp.dot(merged.astype(BF16), wo_ref[...], preferred_element_type=F32)
    o_ref[...] = x_ref[...] + mods_ref[0][5:6] * o


def _merge(x, mods3, row0, cond_tokens, ya, yb, proj, w_glu, w_up_a, w_up_b, w_out):
    n_tok, d = x.shape
    tm = min(256, n_tok)
    assert cond_tokens % tm == 0
    tiles_per_row = cond_tokens // tm
    gate_col = (3 * NA_WIDTH + S5_WIDTH) // d
    assert gate_col * d == 3 * NA_WIDTH + S5_WIDTH
    resident = functools.partial(pl.BlockSpec, pipeline_mode=pl.Buffered(1))
    return pl.pallas_call(
        _merge_kernel,
        grid=(n_tok // tm,),
        in_specs=[pl.BlockSpec((tm, d), lambda i: (i, 0)),
                  pl.BlockSpec((1, N_MOD, d), lambda i: (row0 + i // tiles_per_row, 0, 0)),
                  pl.BlockSpec((tm, NA_WIDTH), lambda i: (i, 0)),
                  pl.BlockSpec((tm, S5_WIDTH), lambda i: (i, 0)),
                  pl.BlockSpec((tm, d), lambda i: (i, gate_col)),
                  pl.BlockSpec((tm, d), lambda i: (i, gate_col + 1)),
                  resident((S5_WIDTH, S5_WIDTH), lambda i: (0, 0)),
                  resident((NA_WIDTH, d), lambda i: (0, 0)),
                  resident((S5_WIDTH, d), lambda i: (0, 0)),
                  resident((d, d), lambda i: (0, 0))],
        out_specs=pl.BlockSpec((tm, d), lambda i: (i, 0)),
        out_shape=jax.ShapeDtypeStruct((n_tok, d), F32),
        compiler_params=_cparams(("parallel",)),
        name="merge",
    )(x, mods3, ya, yb, proj, proj, w_glu, w_up_a, w_up_b, w_out)


def kernel(x_prompt, x_sample, cache_k, cache_v, state_ssm, c, c_ctx, w_ada, b_ada, norm_g, ffn_in, ffn_out, w_in, rpb, s5_lam_re, s5_lam_im, s5_log_dt, s5_b_re, s5_b_im, s5_c_re, s5_c_im, s5_d, w_glu, w_up_a, w_up_b, w_out, final_g):
    depth = w_ada.shape[0]
    pb, pl_, d = x_prompt.shape
    sb, sl, _ = x_sample.shape
    u_col = 3 * NA_WIDTH // LANES
    mod_rows = 8 * ((1 + sb + 7) // 8)

    xp = x_prompt.reshape(pb * pl_, d)
    xs = x_sample.reshape(sb * sl, d)
    cond = jnp.concatenate([c_ctx[None, :], c, jnp.zeros((mod_rows - 1 - sb, d), F32)], axis=0)
    ffn_in_bf, ffn_out_bf = ffn_in.astype(BF16), ffn_out.astype(BF16)
    new_k, new_v, new_s = [], [], []
    for l in range(depth):
        last = l == depth - 1
        mods3 = _mods(cond, w_ada[l], b_ada[l]).reshape(mod_rows, N_MOD, d)
        w_in_bf = w_in[l].astype(BF16)
        w_glu_bf, w_up_a_bf, w_up_b_bf, w_out_bf = (w[l].astype(BF16) for w in (w_glu, w_up_a, w_up_b, w_out))
        e, m, f, at = _s5_prep(s5_lam_re[l], s5_lam_im[l], s5_log_dt[l], s5_b_re[l], s5_b_im[l],
                               s5_c_re[l], s5_c_im[l], sl // S5_SEGMENTS)
        bias = _na_bias_tables(rpb[l], sl // GRID_W)

        ptok = pb * pl_
        xp = _ffn(xp, mods3, 0, ptok, norm_g[l, 0], final_g, ffn_in_bf, ffn_out_bf, l, 0, 0, False)
        proj_p = _proj(xp, mods3, 0, ptok, norm_g[l, 1], w_in_bf)
        ya_p, k_p, v_p = _attn_ctx(proj_p, pb, pl_)
        yb_p, fin_p = _s5(proj_p, u_col, jnp.zeros((N_GROUP_SLABS, pb, SLAB_COLS), F32), e, m, f, at,
                          s5_d[l], pb, pl_, math.gcd(pb, 16), 1)
        xp = _merge(xp, mods3, 0, ptok, ya_p, yb_p, proj_p, w_glu_bf, w_up_a_bf, w_up_b_bf, w_out_bf)
        xp = _ffn(xp, mods3, 0, ptok, norm_g[l, 2], final_g, ffn_in_bf, ffn_out_bf, l, 1, 6, last)
        new_k.append(k_p)
        new_v.append(v_p)
        new_s.append(_slabs_to_state(fin_p))

        xs = _ffn(xs, mods3, 1, sl, norm_g[l, 0], final_g, ffn_in_bf, ffn_out_bf, l, 0, 0, False)
        proj_s = _proj(xs, mods3, 1, sl, norm_g[l, 1], w_in_bf)
        ya_s = _attn_na(proj_s, cache_k, cache_v, l, bias, sb, sl)
        yb_s, _ = _s5(proj_s, u_col, _state_to_slabs(state_ssm[:, l]), e, m, f, at, s5_d[l], sb, sl, 1, S5_SEGMENTS)
        xs = _merge(xs, mods3, 1, sl, ya_s, yb_s, proj_s, w_glu_bf, w_up_a_bf, w_up_b_bf, w_out_bf)
        xs = _ffn(xs, mods3, 1, sl, norm_g[l, 2], final_g, ffn_in_bf, ffn_out_bf, l, 1, 6, last)

    y_prompt = xp.reshape(pb, pl_, d)
    y_sample = xs.reshape(sb, sl, d)
    return (y_prompt, y_sample, jnp.stack(new_k, axis=1), jnp.stack(new_v, axis=1), jnp.stack(new_s, axis=1))
```

```python
import functools
import math

import jax
import jax.numpy as jnp
from jax import lax
from jax.experimental import pallas as pl
from jax.experimental.pallas import tpu as pltpu

F32 = jnp.float32
BF16 = jnp.bfloat16

GRID_W = 64
NA_HEADS = 16
NA_HEAD_DIM = 64
NA_WIDTH = NA_HEADS * NA_HEAD_DIM
WIN_R = 8
WIN_C = 16
S5_GROUP_CH = 16
S5_WIDTH = 1024
S5_GROUPS = S5_WIDTH // S5_GROUP_CH
S5_STATE = 64
N_MOD = 9
EPS = 1e-6

LANES = 128
HEADS_PER_SLAB = LANES // NA_HEAD_DIM
N_HEAD_SLABS = NA_WIDTH // LANES
GROUPS_PER_SLAB = LANES // S5_GROUP_CH
N_GROUP_SLABS = S5_WIDTH // LANES
SLAB_STATE = GROUPS_PER_SLAB * S5_STATE
SLAB_COLS = 4 * SLAB_STATE
HALF_LANES = LANES // 2
HALF_STATE = SLAB_STATE // 2
S5_CHUNK = 8
S5_HALF_IN = S5_CHUNK * HALF_LANES
S5_HALF_STATES = 4 * HALF_STATE
S5_SEGMENTS = 8
NA_QROWS = 4
NA_KROWS = NA_QROWS + WIN_R
NEG_BIAS = -1e30
LOG2E = 1.4426950408889634
VMEM_LIMIT = 56 * 1024 * 1024


def _cparams(sem, limit=VMEM_LIMIT):
    return pltpu.CompilerParams(dimension_semantics=sem, vmem_limit_bytes=limit)


def _silu(x):
    return x * jax.nn.sigmoid(x)


NORM_ROWS = 16
NORM_UNROLL = 8


def _norm_mod_store(x_ref, h_ref, g, shift, scale):
    gs = g * (1.0 + scale)

    def body(i, carry):
        r0 = pl.multiple_of(i * NORM_ROWS, NORM_ROWS)
        x = x_ref[pl.ds(r0, NORM_ROWS), :]
        r = lax.rsqrt(jnp.mean(x * x, axis=-1, keepdims=True) + EPS)
        h_ref[pl.ds(r0, NORM_ROWS), :] = ((x * r) * gs + shift).astype(BF16)
        return carry

    lax.fori_loop(0, x_ref.shape[0] // NORM_ROWS, body, 0, unroll=NORM_UNROLL)


def _split_bf16(x):
    hi = x.astype(BF16)
    return hi, (x - hi.astype(F32)).astype(BF16)


def _dot3(a, b):
    a_hi, a_lo = _split_bf16(a)
    b_hi, b_lo = _split_bf16(b)
    return (jnp.dot(a_hi, b_hi, preferred_element_type=F32)
            + (jnp.dot(a_lo, b_hi, preferred_element_type=F32) + jnp.dot(a_hi, b_lo, preferred_element_type=F32)))


def _mods_kernel(c_ref, w_ref, b_ref, o_ref):
    o_ref[...] = _dot3(_silu(c_ref[...]), w_ref[...]) + b_ref[...]


def _mods(cond, w_ada, b_ada):
    rows, d = cond.shape
    n = w_ada.shape[1]
    tn = math.gcd(1024, n)
    return pl.pallas_call(
        _mods_kernel,
        grid=(n // tn,),
        in_specs=[pl.BlockSpec((rows, d), lambda j: (0, 0)),
                  pl.BlockSpec((d, tn), lambda j: (0, j)),
                  pl.BlockSpec((1, tn), lambda j: (0, j))],
        out_specs=pl.BlockSpec((rows, tn), lambda j: (0, j)),
        out_shape=jax.ShapeDtypeStruct((rows, n), F32),
        compiler_params=_cparams(("arbitrary",)),
        name="mods",
    )(cond, w_ada, b_ada.reshape(1, n))


def _ffn_kernel(x_hbm, mods_ref, ng_ref, fg_ref, wg_ref, wu_ref, wo_ref, o_ref, xs_ref, h_ref, rs_ref, sem,
                *, mod_base, final_norm):
    i, j = pl.program_id(0), pl.program_id(1)
    tm = xs_ref.shape[0]

    def x_copy(tile):
        return pltpu.make_async_copy(x_hbm.at[pl.ds(tile * tm, tm), :], xs_ref, sem)

    @pl.when((i == 0) & (j == 0))
    def _():
        x_copy(0).start()

    @pl.when(j == 0)
    def _():
        x_copy(i).wait()
        m = mods_ref[0]
        gs = ng_ref[...] * (1.0 + m[mod_base + 1:mod_base + 2])
        shift = m[mod_base:mod_base + 1]

        def body(r, carry):
            r0 = pl.multiple_of(r * NORM_ROWS, NORM_ROWS)
            x = xs_ref[pl.ds(r0, NORM_ROWS), :]
            o_ref[pl.ds(r0, NORM_ROWS), :] = x
            rs = lax.rsqrt(jnp.mean(x * x, axis=-1, keepdims=True) + EPS)
            h_ref[pl.ds(r0, NORM_ROWS), :] = ((x * rs) * gs + shift).astype(BF16)
            return carry

        lax.fori_loop(0, tm // NORM_ROWS, body, 0, unroll=NORM_UNROLL)

    @pl.when((j == 1) & (i + 1 < pl.num_programs(0)))
    def _():
        x_copy(i + 1).start()

    h = h_ref[...]
    g = jnp.dot(h, wg_ref[...], preferred_element_type=F32)
    u = jnp.dot(h, wu_ref[...], preferred_element_type=F32)
    a = (_silu(g) * u).astype(BF16)
    half_gate = 0.5 * mods_ref[0][mod_base + 2:mod_base + 3]
    o_ref[...] += half_gate * jnp.dot(a, wo_ref[...], preferred_element_type=F32)

    if final_norm:
        @pl.when(j == pl.num_programs(1) - 1)
        def _():
            def body(r, carry):
                r0 = pl.multiple_of(r * NORM_ROWS, NORM_ROWS)
                y = o_ref[pl.ds(r0, NORM_ROWS), :]
                rs = lax.rsqrt(jnp.mean(y * y, axis=-1, keepdims=True) + EPS)
                rs_ref[pl.ds(r0, NORM_ROWS), :] = jnp.broadcast_to(rs, (NORM_ROWS, LANES))
                return carry

            lax.fori_loop(0, tm // NORM_ROWS, body, 0, unroll=NORM_UNROLL)
            for c in range(o_ref.shape[1] // LANES):
                cols = slice(c * LANES, (c + 1) * LANES)
                o_ref[:, cols] = (o_ref[:, cols] * rs_ref[...]) * fg_ref[:, cols]


FFN_TM = 1024
FFN_TF = 512


def _ffn(x, mods3, row0, cond_tokens, norm_g, final_g, w_in_bf, w_out_bf, layer, which, mod_base, final_norm):
    n_tok, d = x.shape
    d_ff = w_out_bf.shape[2]
    tm = min(FFN_TM, n_tok)
    tf = min(FFN_TF, d_ff)
    assert cond_tokens % tm == 0
    tiles_per_row = cond_tokens // tm
    assert d_ff // tf >= 2
    kern = functools.partial(_ffn_kernel, mod_base=mod_base, final_norm=final_norm)
    return pl.pallas_call(
        kern,
        grid=(n_tok // tm, d_ff // tf),
        in_specs=[pl.BlockSpec(memory_space=pl.ANY),
                  pl.BlockSpec((1, N_MOD, d), lambda i, j: (row0 + i // tiles_per_row, 0, 0)),
                  pl.BlockSpec((1, d), lambda i, j: (0, 0)),
                  pl.BlockSpec((1, d), lambda i, j: (0, 0)),
                  pl.BlockSpec((None, None, d, tf), lambda i, j: (layer, which, 0, j)),
                  pl.BlockSpec((None, None, d, tf), lambda i, j: (layer, which, 0, d_ff // tf + j)),
                  pl.BlockSpec((None, None, tf, d), lambda i, j: (layer, which, j, 0))],
        out_specs=pl.BlockSpec((tm, d), lambda i, j: (i, 0)),
        out_shape=jax.ShapeDtypeStruct((n_tok, d), F32),
        scratch_shapes=[pltpu.VMEM((tm, d), F32), pltpu.VMEM((tm, d), BF16), pltpu.VMEM((tm, LANES), F32),
                        pltpu.SemaphoreType.DMA(())],
        compiler_params=_cparams(("arbitrary", "arbitrary")),
        name="ffn",
    )(x, mods3, norm_g.reshape(1, d), final_g.reshape(1, d), w_in_bf, w_in_bf, w_out_bf)


def _proj_kernel(x_hbm, mods_ref, ng_ref, w_ref, o_ref, xs_ref, h_ref, sem):
    i, j = pl.program_id(0), pl.program_id(1)
    tm = xs_ref.shape[0]

    def x_copy(tile):
        return pltpu.make_async_copy(x_hbm.at[pl.ds(tile * tm, tm), :], xs_ref, sem)

    @pl.when((i == 0) & (j == 0))
    def _():
        x_copy(0).start()

    @pl.when(j == 0)
    def _():
        x_copy(i).wait()
        m = mods_ref[0]
        _norm_mod_store(xs_ref, h_ref, ng_ref[...], m[3:4], m[4:5])

    @pl.when((j == 1) & (i + 1 < pl.num_programs(0)))
    def _():
        x_copy(i + 1).start()

    o_ref[...] = jnp.dot(h_ref[...], w_ref[...], preferred_element_type=F32)


def _proj(x, mods3, row0, cond_tokens, norm_g, w_bf):
    n_tok, d = x.shape
    n = w_bf.shape[1]
    tm = min(1024, n_tok)
    tn = math.gcd(2048, d)
    assert cond_tokens % tm == 0 and n // tn >= 2
    tiles_per_row = cond_tokens // tm
    return pl.pallas_call(
        _proj_kernel,
        grid=(n_tok // tm, n // tn),
        in_specs=[pl.BlockSpec(memory_space=pl.ANY),
                  pl.BlockSpec((1, N_MOD, d), lambda i, j: (row0 + i // tiles_per_row, 0, 0)),
                  pl.BlockSpec((1, d), lambda i, j: (0, 0)),
                  pl.BlockSpec((d, tn), lambda i, j: (0, j))],
        out_specs=pl.BlockSpec((tm, tn), lambda i, j: (i, j)),
        out_shape=jax.ShapeDtypeStruct((n_tok, n), F32),
        scratch_shapes=[pltpu.VMEM((tm, d), F32), pltpu.VMEM((tm, d), BF16), pltpu.SemaphoreType.DMA(())],
        compiler_params=_cparams(("arbitrary", "arbitrary")),
        name="proj",
    )(x, mods3, norm_g.reshape(1, d), w_bf)


def _head_mask(shape, hh):
    lane = lax.broadcasted_iota(jnp.int32, shape, 1)
    return (lane >= hh * NA_HEAD_DIM) & (lane < (hh + 1) * NA_HEAD_DIM)


def _nt_dot(a, b):
    return lax.dot_general(a, b, (((1,), (1,)), ((), ())), preferred_element_type=F32)


Q_SCALE = NA_HEAD_DIM ** -0.5 * LOG2E
CTX_SLABS = 8
NA_SLABS = 2
NA_BLOCKS = 4
NA_VMEM_LIMIT = 60 * 1024 * 1024


def _ones_other_head(v, hh):
    return jnp.where(_head_mask(v.shape, hh), v, 1.0)


def _finish_heads(o0, o1):
    first = _head_mask(o0.shape, 0)
    num = jnp.where(first, o0, o1)
    den = pltpu.roll(jnp.where(first, o1, o0), NA_HEAD_DIM, 1)
    return num / den


def _attn_ctx_kernel(q_ref, k_ref, v_ref, o_ref, ko_ref, vo_ref):
    for sl in range(q_ref.shape[1] // LANES):
        cols = slice(sl * LANES, (sl + 1) * LANES)
        q = q_ref[:, cols] * Q_SCALE
        kf = k_ref[:, cols]
        vf = v_ref[:, cols]
        k = kf.astype(BF16)
        outs = []
        for hh in range(HEADS_PER_SLAB):
            head = slice(hh * NA_HEAD_DIM, (hh + 1) * NA_HEAD_DIM)
            ko_ref[0, sl * HEADS_PER_SLAB + hh] = kf[:, head]
            vo_ref[0, sl * HEADS_PER_SLAB + hh] = vf[:, head]
            qm = jnp.where(_head_mask(q.shape, hh), q, 0.0).astype(BF16)
            s = _nt_dot(qm, k)
            p = jnp.exp2(s - jnp.max(s, axis=-1, keepdims=True))
            outs.append(jnp.dot(p.astype(BF16), _ones_other_head(vf, hh).astype(BF16),
                                preferred_element_type=F32))
        o_ref[:, cols] = _finish_heads(*outs)


def _attn_ctx(proj, batch, seq):
    w = CTX_SLABS * LANES
    nsb = NA_WIDTH // w
    hps = CTX_SLABS * HEADS_PER_SLAB
    kv_spec = pl.BlockSpec((1, hps, seq, NA_HEAD_DIM), lambda b, s: (b, s, 0, 0))
    kv_shape = jax.ShapeDtypeStruct((batch, NA_HEADS, seq, NA_HEAD_DIM), F32)
    return pl.pallas_call(
        _attn_ctx_kernel,
        grid=(batch, nsb),
        in_specs=[pl.BlockSpec((seq, w), lambda b, s: (b, s)),
                  pl.BlockSpec((seq, w), lambda b, s: (b, nsb + s)),
                  pl.BlockSpec((seq, w), lambda b, s: (b, 2 * nsb + s))],
        out_specs=[pl.BlockSpec((seq, w), lambda b, s: (b, s)), kv_spec, kv_spec],
        out_shape=[jax.ShapeDtypeStruct((batch * seq, NA_WIDTH), F32), kv_shape, kv_shape],
        compiler_params=_cparams(("parallel", "parallel")),
        name="attn_ctx",
    )(proj, proj, proj)


def _attn_na_kernel(q_ref, k_ref, v_ref, ck_ref, cv_ref, bias_a_ref, bias_mid_ref, bias_b_ref, o_ref,
                    kb_ref, vb_ref, ckb_ref, cvb_ref, *, rows):
    step = pl.program_id(2)
    nsl = q_ref.shape[1] // LANES

    @pl.when(step == 0)
    def _():
        kb_ref[...] = k_ref[...].astype(BF16)
        vb_ref[...] = v_ref[...].astype(BF16)
        for hd in range(ck_ref.shape[0]):
            lanes = slice(hd * NA_HEAD_DIM, (hd + 1) * NA_HEAD_DIM)
            ckb_ref[:, lanes] = ck_ref[hd].astype(BF16)
            cvb_ref[:, lanes] = cv_ref[hd].astype(BF16)

    heads = [(sl, hh) for sl in range(nsl) for hh in range(HEADS_PER_SLAB)]
    cols = [slice(sl * LANES, (sl + 1) * LANES) for sl in range(nsl)]
    tq = NA_QROWS * GRID_W
    for sub, bias_ref in enumerate([bias_a_ref] + [bias_mid_ref] * (NA_BLOCKS - 2) + [bias_b_ref]):
        rb = step * NA_BLOCKS + sub
        qrows = slice(sub * tq, (sub + 1) * tq)
        start = jnp.clip(rb * NA_QROWS - WIN_R // 2, 0, rows - NA_KROWS)
        off = pl.multiple_of(start * GRID_W, NA_QROWS * GRID_W)
        scores = []
        for sl, hh in heads:
            q = q_ref[qrows, cols[sl]] * Q_SCALE
            qm = jnp.where(_head_mask(q.shape, hh), q, 0.0).astype(BF16)
            s_loc = (_nt_dot(qm, kb_ref[pl.ds(off, NA_KROWS * GRID_W), cols[sl]])
                     + bias_ref[0, sl * HEADS_PER_SLAB + hh])
            s_ctx = _nt_dot(qm, ckb_ref[:, cols[sl]])
            scores.append((s_loc, s_ctx))
        probs = []
        for s_loc, s_ctx in scores:
            m = jnp.maximum(jnp.max(s_loc, axis=-1, keepdims=True), jnp.max(s_ctx, axis=-1, keepdims=True))
            p_loc = jnp.exp2(s_loc - m)
            p_ctx = jnp.exp2(s_ctx - m)
            l = jnp.sum(p_loc, axis=-1, keepdims=True) + jnp.sum(p_ctx, axis=-1, keepdims=True)
            probs.append((p_loc.astype(BF16), p_ctx.astype(BF16), l))
        outs = []
        for (sl, hh), (p_loc, p_ctx, l) in zip(heads, probs):
            o = (jnp.dot(p_loc, vb_ref[pl.ds(off, NA_KROWS * GRID_W), cols[sl]], preferred_element_type=F32)
                 + jnp.dot(p_ctx, cvb_ref[:, cols[sl]], preferred_element_type=F32))
            outs.append(o / l)
        for sl in range(nsl):
            first = _head_mask(outs[0].shape, 0)
            o_ref[qrows, cols[sl]] = jnp.where(first, outs[sl * HEADS_PER_SLAB], outs[sl * HEADS_PER_SLAB + 1])


def _na_bias_kernel(rpb_ref, o_ref, *, rows):
    shape = (GRID_W, LANES)
    qc = lax.broadcasted_iota(jnp.int32, shape, 0)
    lane = lax.broadcasted_iota(jnp.int32, shape, 1)
    kc = lane % GRID_W
    cs = jnp.clip(qc - WIN_C // 2, 0, GRID_W - WIN_C)
    valid_c = (kc >= cs) & (kc < cs + WIN_C)
    first_row = lane < GRID_W
    r = rpb_ref[0] * LOG2E
    for p, r0 in enumerate((0, NA_QROWS, rows - NA_QROWS)):
        start = min(max(r0 - WIN_R // 2, 0), rows - NA_KROWS)
        for qr in range(NA_QROWS):
            q_row = r0 + qr
            ws = min(max(q_row - WIN_R // 2, 0), rows - WIN_R)
            for k2 in range(NA_KROWS // 2):
                halves = []
                for kk in range(2):
                    k_row = start + 2 * k2 + kk
                    if ws <= k_row < ws + WIN_R:
                        dr = k_row - q_row + WIN_R - 1
                        row = jnp.broadcast_to(r[dr:dr + 1, :], shape)
                        shift = (LANES - (WIN_C - 1) + GRID_W * kk) % LANES
                        halves.append(pltpu.roll(row, shift, 1, stride=1, stride_axis=0))
                    else:
                        halves.append(jnp.full(shape, NEG_BIAS, F32))
                tile = jnp.where(valid_c, jnp.where(first_row, halves[0], halves[1]), NEG_BIAS)
                o_ref[p, 0, qr * GRID_W:(qr + 1) * GRID_W, k2 * LANES:(k2 + 1) * LANES] = tile


def _na_bias_tables(rpb, rows):
    h, nr, nc = rpb.shape
    rpb_pad = jnp.zeros((h, 16, LANES), F32).at[:, :nr, :nc].set(rpb)
    tq, tk = NA_QROWS * GRID_W, NA_KROWS * GRID_W
    return pl.pallas_call(
        functools.partial(_na_bias_kernel, rows=rows),
        grid=(h,),
        in_specs=[pl.BlockSpec((1, 16, LANES), lambda i: (i, 0, 0))],
        out_specs=pl.BlockSpec((3, 1, tq, tk), lambda i: (0, i, 0, 0)),
        out_shape=jax.ShapeDtypeStruct((3, h, tq, tk), F32),
        compiler_params=_cparams(("parallel",)),
        name="na_bias",
    )(rpb_pad)


def _attn_na(proj, cache_k, cache_v, layer, bias, batch, seq):
    past = cache_k.shape[3]
    rows = seq // GRID_W
    assert rows >= NA_KROWS and rows % (NA_QROWS * NA_BLOCKS) == 0
    nst = rows // (NA_QROWS * NA_BLOCKS)
    tq = NA_BLOCKS * NA_QROWS * GRID_W

    def bias_a_map(b, s, r):
        return ((r > 0).astype(jnp.int32), s, 0, 0)

    def bias_mid_map(b, s, r):
        return (1, s, 0, 0)

    def bias_b_map(b, s, r):
        return (1 + (r == nst - 1).astype(jnp.int32), s, 0, 0)

    w = NA_SLABS * LANES
    nsb = NA_WIDTH // w
    hps = NA_SLABS * HEADS_PER_SLAB
    ctx_spec = pl.BlockSpec((None, None, hps, past, NA_HEAD_DIM), lambda b, s, r: (b, layer, s, 0, 0))
    bias_block = (1, hps, NA_QROWS * GRID_W, NA_KROWS * GRID_W)
    return pl.pallas_call(
        functools.partial(_attn_na_kernel, rows=rows),
        grid=(batch, nsb, nst),
        in_specs=[pl.BlockSpec((tq, w), lambda b, s, r: (b * nst + r, s)),
                  pl.BlockSpec((seq, w), lambda b, s, r: (b, nsb + s)),
                  pl.BlockSpec((seq, w), lambda b, s, r: (b, 2 * nsb + s)),
                  ctx_spec, ctx_spec,
                  pl.BlockSpec(bias_block, bias_a_map), pl.BlockSpec(bias_block, bias_mid_map),
                  pl.BlockSpec(bias_block, bias_b_map)],
        out_specs=pl.BlockSpec((tq, w), lambda b, s, r: (b * nst + r, s)),
        out_shape=jax.ShapeDtypeStruct((batch * seq, NA_WIDTH), F32),
        scratch_shapes=[pltpu.VMEM((seq, w), BF16), pltpu.VMEM((seq, w), BF16),
                        pltpu.VMEM((past, w), BF16), pltpu.VMEM((past, w), BF16)],
        compiler_params=_cparams(("parallel", "parallel", "arbitrary"), NA_VMEM_LIMIT),
        name="attn_na",
    )(proj, proj, proj, cache_k, cache_v, bias, bias, bias)


def _pair_halves(a, b, q, low):
    if q == 0:
        return jnp.where(low, a, pltpu.roll(b, HALF_LANES, 1))
    return jnp.where(low, pltpu.roll(a, HALF_LANES, 1), b)


def _s5_prep_kernel(lr_ref, li_ref, ld_ref, btr_ref, bti_ref, cr_ref, ci_ref,
                    e_ref, m_ref, f_ref, at_ref, *, seg_tokens):
    T = S5_CHUNK
    ns = SLAB_STATE
    rb = lax.broadcasted_iota(jnp.int32, (LANES, ns), 0) // S5_GROUP_CH
    cb = lax.broadcasted_iota(jnp.int32, (LANES, ns), 1) // S5_STATE
    mask_b = rb == cb

    def powers(ldr, ldi, n):
        mag = jnp.exp(ldr * float(n))
        return mag * jnp.cos(ldi * float(n)), mag * jnp.sin(ldi * float(n))

    hl, hs = HALF_LANES, HALF_STATE
    low64 = lax.broadcasted_iota(jnp.int32, (hl, LANES), 1) < hl
    kt = [[None] * T, [None] * T]
    for d in range(2):
        lam_re, lam_im = lr_ref[0, d], li_ref[0, d]
        dt = jnp.exp(ld_ref[0, d])
        ldr, ldi = lam_re * dt, lam_im * dt
        a_re, a_im = powers(ldr, ldi, 1)
        mag2 = lam_re * lam_re + lam_im * lam_im
        f_re = ((a_re - 1.0) * lam_re + a_im * lam_im) / mag2
        f_im = (a_im * lam_re - (a_re - 1.0) * lam_im) / mag2
        btr = jnp.where(mask_b, btr_ref[0, d], 0.0)
        bti = jnp.where(mask_b, bti_ref[0, d], 0.0)
        cr = jnp.where(mask_b, cr_ref[0, d], 0.0)
        ci = jnp.where(mask_b, ci_ref[0, d], 0.0)
        ctr, cti = cr.T, ci.T
        e_res, e_ims = [], []
        for n in range(T):
            an_re, an_im = powers(ldr, ldi, n)
            w_re = f_re * an_re - f_im * an_im
            w_im = f_re * an_im + f_im * an_re
            e_re = btr * w_re - bti * w_im
            e_im = btr * w_im + bti * w_re
            t = T - 1 - n if d == 0 else n
            for q in range(2):
                rows, cols = slice(q * hl, (q + 1) * hl), slice(q * hs, (q + 1) * hs)
                e_ref[0, q, t * hl:(t + 1) * hl, 2 * d * hs:(2 * d + 1) * hs] = e_re[rows, cols].astype(BF16)
                e_ref[0, q, t * hl:(t + 1) * hl, (2 * d + 1) * hs:(2 * d + 2) * hs] = e_im[rows, cols].astype(BF16)
            e_res.append(e_re)
            e_ims.append(e_im)
        k_all = _dot3(jnp.concatenate(e_res, axis=0), ctr) - _dot3(jnp.concatenate(e_ims, axis=0), cti)
        for n in range(T):
            kt[d][n] = k_all[n * LANES:(n + 1) * LANES]
        coef = []
        for tp in range(T):
            n = tp + 1 if d == 0 else T - tp
            an_re, an_im = powers(ldr, ldi, n)
            coef.append((cr * an_re - ci * an_im, -(cr * an_im + ci * an_re)))
        for q in range(2):
            rows, cols = slice(q * hl, (q + 1) * hl), slice(q * hs, (q + 1) * hs)
            for pr in range(T // 2):
                for ri in range(2):
                    tile = jnp.concatenate([coef[2 * pr][ri][rows, cols], coef[2 * pr + 1][ri][rows, cols]], axis=0)
                    f_ref[0, q, (2 * d + ri) * hs:(2 * d + ri + 1) * hs, pr * LANES:(pr + 1) * LANES] = (
                        tile.T.astype(BF16))
        at_re, at_im = powers(ldr, ldi, T)
        at_ref[0, 2 * d:2 * d + 1, :] = at_re
        at_ref[0, 2 * d + 1:2 * d + 2, :] = at_im
        as_re, as_im = powers(ldr, ldi, seg_tokens)
        at_ref[0, 4 + 2 * d:5 + 2 * d, :] = as_re
        at_ref[0, 5 + 2 * d:6 + 2 * d, :] = as_im
    def lag_block(t, tp):
        if tp > t:
            return kt[0][tp - t]
        if tp < t:
            return kt[1][t - tp]
        return kt[0][0] + kt[1][0]

    for q in range(2):
        rows = slice(q * hl, (q + 1) * hl)
        for t in range(T):
            for pr in range(T // 2):
                tile = _pair_halves(lag_block(t, 2 * pr)[rows], lag_block(t, 2 * pr + 1)[rows], q, low64)
                m_ref[0, q, t * hl:(t + 1) * hl, pr * LANES:(pr + 1) * LANES] = tile.astype(BF16)


def _s5_prep(lam_re, lam_im, log_dt, b_re, b_im, c_re, c_im, seg_tokens):
    T, S, G8, ns = S5_CHUNK, N_GROUP_SLABS, GROUPS_PER_SLAB, SLAB_STATE

    def state_vec(v):
        return v.reshape(2, S, ns).transpose(1, 0, 2)

    lr, li = state_vec(lam_re), state_vec(lam_im)
    ld = state_vec(jnp.broadcast_to(log_dt[:, :, None], lam_re.shape))

    def bt(v):
        v = v.reshape(2, S, G8, S5_STATE, S5_GROUP_CH).transpose(1, 0, 2, 4, 3).reshape(S, 2, LANES, S5_STATE)
        return jnp.tile(v, (1, 1, 1, G8))

    def crow(v):
        v = v.reshape(2, S, G8, S5_GROUP_CH, S5_STATE).transpose(1, 0, 2, 3, 4).reshape(S, 2, LANES, S5_STATE)
        return jnp.tile(v, (1, 1, 1, G8))

    row = lambda v: v[:, :, None, :]
    spec_row = pl.BlockSpec((1, 2, 1, ns), lambda j: (j, 0, 0, 0))
    spec_bt = pl.BlockSpec((1, 2, LANES, ns), lambda j: (j, 0, 0, 0))
    return pl.pallas_call(
        functools.partial(_s5_prep_kernel, seg_tokens=seg_tokens),
        grid=(S,),
        in_specs=[spec_row, spec_row, spec_row, spec_bt, spec_bt, spec_bt, spec_bt],
        out_specs=[pl.BlockSpec((1, 2, S5_HALF_IN, S5_HALF_STATES), lambda j: (j, 0, 0, 0)),
                   pl.BlockSpec((1, 2, S5_HALF_IN, S5_HALF_IN), lambda j: (j, 0, 0, 0)),
                   pl.BlockSpec((1, 2, S5_HALF_STATES, S5_HALF_IN), lambda j: (j, 0, 0, 0)),
                   pl.BlockSpec((1, 8, ns), lambda j: (j, 0, 0))],
        out_shape=[jax.ShapeDtypeStruct((S, 2, S5_HALF_IN, S5_HALF_STATES), BF16),
                   jax.ShapeDtypeStruct((S, 2, S5_HALF_IN, S5_HALF_IN), BF16),
                   jax.ShapeDtypeStruct((S, 2, S5_HALF_STATES, S5_HALF_IN), BF16),
                   jax.ShapeDtypeStruct((S, 8, ns), F32)],
        compiler_params=_cparams(("parallel",)),
        name="s5_prep",
    )(row(lr), row(li), row(ld), bt(b_re), bt(b_im), crow(c_re), crow(c_im))


def _s5_kernel(u_ref, h0_ref, e_ref, m_ref, f_ref, at_ref, d_ref, y_ref, fin_ref,
               us_ref, ubf_ref, st_ref, ys_ref, *, bt, nseg, seq):
    T = S5_CHUNK
    ns = SLAB_STATE
    npar = bt * nseg
    seg = seq // nseg
    nc = seg // T
    for p in range(npar):
        for t in range(T):
            us_ref[t, pl.ds(p, nc, stride=npar), :] = u_ref[pl.ds(p * seg + t, nc, stride=T), :]
    hl, hs = HALF_LANES, HALF_STATE
    low = lax.broadcasted_iota(jnp.int32, (us_ref.shape[1], LANES), 1) < hl
    for q in range(2):
        for pr in range(T // 2):
            pair = _pair_halves(us_ref[2 * pr], us_ref[2 * pr + 1], q, low)
            ubf_ref[q, :, pr * LANES:(pr + 1) * LANES] = pair.astype(BF16)
        local = jnp.dot(ubf_ref[q], e_ref[0, q], preferred_element_type=F32)
        for k in range(4):
            st_ref[:, k * ns + q * hs:k * ns + (q + 1) * hs] = local[:, k * hs:(k + 1) * hs]

    at = at_ref[0]
    a_fr, a_fi, a_br, a_bi = at[0:1], at[1:2], at[2:3], at[3:4]

    def scan(init, keep_entering):
        def step(i, carry):
            c_fr, c_fi, c_br, c_bi = carry
            rf = pl.multiple_of(i * npar, npar)
            rbk = pl.multiple_of((nc - 1 - i) * npar, npar)
            l_fr = st_ref[pl.ds(rf, npar), 0:ns]
            l_fi = st_ref[pl.ds(rf, npar), ns:2 * ns]
            l_br = st_ref[pl.ds(rbk, npar), 2 * ns:3 * ns]
            l_bi = st_ref[pl.ds(rbk, npar), 3 * ns:4 * ns]
            if keep_entering:
                st_ref[pl.ds(rf, npar), 0:ns] = c_fr
                st_ref[pl.ds(rf, npar), ns:2 * ns] = c_fi
                st_ref[pl.ds(rbk, npar), 2 * ns:3 * ns] = c_br
                st_ref[pl.ds(rbk, npar), 3 * ns:4 * ns] = c_bi
            return (a_fr * c_fr - a_fi * c_fi + l_fr, a_fr * c_fi + a_fi * c_fr + l_fi,
                    a_br * c_br - a_bi * c_bi + l_br, a_br * c_bi + a_bi * c_br + l_bi)

        return lax.fori_loop(0, nc, step, init)

    h0 = h0_ref[0, 0]
    h0 = [h0[:, k * ns:(k + 1) * ns] for k in range(4)]
    if nseg == 1:
        enter = tuple(h0)
    else:
        zero = jnp.zeros((npar, ns), F32)
        z_fr, z_fi, z_br, z_bi = scan((zero, zero, zero, zero), False)
        s_fr, s_fi, s_br, s_bi = at[4:5], at[5:6], at[6:7], at[7:8]
        rows_f, rows_b = [None] * npar, [None] * npar
        for b in range(bt):
            c_re, c_im = h0[0][b:b + 1], h0[1][b:b + 1]
            for s in range(nseg):
                p = b * nseg + s
                rows_f[p] = (c_re, c_im)
                c_re, c_im = (s_fr * c_re - s_fi * c_im + z_fr[p:p + 1], s_fr * c_im + s_fi * c_re + z_fi[p:p + 1])
            c_re, c_im = h0[2][b:b + 1], h0[3][b:b + 1]
            for s in reversed(range(nseg)):
                p = b * nseg + s
                rows_b[p] = (c_re, c_im)
                c_re, c_im = (s_br * c_re - s_bi * c_im + z_br[p:p + 1], s_br * c_im + s_bi * c_re + z_bi[p:p + 1])
        enter = (jnp.concatenate([r[0] for r in rows_f], axis=0), jnp.concatenate([r[1] for r in rows_f], axis=0),
                 jnp.concatenate([r[0] for r in rows_b], axis=0), jnp.concatenate([r[1] for r in rows_b], axis=0))
    fin = scan(enter, True)
    for b in range(bt):
        last_f, last_b = b * nseg + nseg - 1, b * nseg
        fin_ref[0, 0, b:b + 1, 0:ns] = fin[0][last_f:last_f + 1]
        fin_ref[0, 0, b:b + 1, ns:2 * ns] = fin[1][last_f:last_f + 1]
        fin_ref[0, 0, b:b + 1, 2 * ns:3 * ns] = fin[2][last_b:last_b + 1]
        fin_ref[0, 0, b:b + 1, 3 * ns:4 * ns] = fin[3][last_b:last_b + 1]

    yq = []
    for q in range(2):
        entering = jnp.concatenate([st_ref[:, k * ns + q * hs:k * ns + (q + 1) * hs] for k in range(4)], axis=1)
        yq.append(jnp.dot(ubf_ref[q], m_ref[0, q], preferred_element_type=F32)
                  + jnp.dot(entering.astype(BF16), f_ref[0, q], preferred_element_type=F32))
    for pr in range(T // 2):
        y0 = yq[0][:, pr * LANES:(pr + 1) * LANES]
        y1 = yq[1][:, pr * LANES:(pr + 1) * LANES]
        ys_ref[2 * pr] = jnp.where(low, y0, pltpu.roll(y1, hl, 1)) + d_ref[0] * us_ref[2 * pr]
        ys_ref[2 * pr + 1] = jnp.where(low, pltpu.roll(y0, hl, 1), y1) + d_ref[0] * us_ref[2 * pr + 1]
    for p in range(npar):
        for t in range(T):
            y_ref[pl.ds(p * seg + t, nc, stride=T), :] = ys_ref[t, pl.ds(p, nc, stride=npar), :]


def _s5(proj, u_col, h0, e, m, f, at, d_skip, batch, seq, bt, nseg):
    T, S = S5_CHUNK, N_GROUP_SLABS
    rows = bt * seq // T
    const = lambda j, g: (j, 0, 0)
    const4 = lambda j, g: (j, 0, 0, 0)
    y, fin = pl.pallas_call(
        functools.partial(_s5_kernel, bt=bt, nseg=nseg, seq=seq),
        grid=(S, batch // bt),
        in_specs=[pl.BlockSpec((bt * seq, LANES), lambda j, g: (g, u_col + j)),
                  pl.BlockSpec((1, 1, bt, SLAB_COLS), lambda j, g: (j, g, 0, 0)),
                  pl.BlockSpec((1, 2, S5_HALF_IN, S5_HALF_STATES), const4),
                  pl.BlockSpec((1, 2, S5_HALF_IN, S5_HALF_IN), const4),
                  pl.BlockSpec((1, 2, S5_HALF_STATES, S5_HALF_IN), const4),
                  pl.BlockSpec((1, 8, SLAB_STATE), const),
                  pl.BlockSpec((1, 1, LANES), const)],
        out_specs=[pl.BlockSpec((bt * seq, LANES), lambda j, g: (g, j)),
                   pl.BlockSpec((1, 1, bt, SLAB_COLS), lambda j, g: (j, g, 0, 0))],
        out_shape=[jax.ShapeDtypeStruct((batch * seq, S5_WIDTH), F32),
                   jax.ShapeDtypeStruct((S, batch // bt, bt, SLAB_COLS), F32)],
        scratch_shapes=[pltpu.VMEM((T, rows, LANES), F32), pltpu.VMEM((2, rows, S5_HALF_IN), BF16),
                        pltpu.VMEM((rows, SLAB_COLS), F32), pltpu.VMEM((T, rows, LANES), F32)],
        compiler_params=_cparams(("parallel", "arbitrary")),
        name="s5",
    )(proj, h0.reshape(S, batch // bt, bt, SLAB_COLS), e, m, f, at, d_skip.reshape(S, 1, LANES))
    return y, fin.reshape(S, batch, SLAB_COLS)


def _state_to_slabs(state):
    b = state.shape[0]
    s = state.reshape(b, 2, 2, N_GROUP_SLABS, SLAB_STATE).transpose(3, 0, 1, 2, 4)
    return s.reshape(N_GROUP_SLABS, b, SLAB_COLS)


def _slabs_to_state(slabs):
    b = slabs.shape[1]
    s = slabs.reshape(N_GROUP_SLABS, b, 2, 2, SLAB_STATE).transpose(1, 2, 3, 0, 4)
    return s.reshape(b, 2, 2, S5_GROUPS, S5_STATE)


def _merge_kernel(x_ref, mods_ref, ya_ref, yb_ref, ga_ref, gb_ref, wglu_ref, wua_ref, wub_ref, wo_ref, o_ref):
    yb = jax.nn.gelu(yb_ref[...])
    glu = jnp.dot(yb.astype(BF16), wglu_ref[...], preferred_element_type=F32)
    yb = yb * jax.nn.sigmoid(glu)
    a = jnp.dot(ya_ref[...].astype(BF16), wua_ref[...], preferred_element_type=F32)
    b = jnp.dot(yb.astype(BF16), wub_ref[...], preferred_element_type=F32)
    merged = jax.nn.sigmoid(ga_ref[...]) * a + jax.nn.sigmoid(gb_ref[...]) * b
    o = jnp.dot(merged.astype(BF16), wo_ref[...], preferred_element_type=F32)
    o_ref[...] = x_ref[...] + mods_ref[0][5:6] * o


def _merge(x, mods3, row0, cond_tokens, ya, yb, proj, w_glu, w_up_a, w_up_b, w_out):
    n_tok, d = x.shape
    tm = min(256, n_tok)
    assert cond_tokens % tm == 0
    tiles_per_row = cond_tokens // tm
    gate_col = (3 * NA_WIDTH + S5_WIDTH) // d
    assert gate_col * d == 3 * NA_WIDTH + S5_WIDTH
    resident = functools.partial(pl.BlockSpec, pipeline_mode=pl.Buffered(1))
    return pl.pallas_call(
        _merge_kernel,
        grid=(n_tok // tm,),
        in_specs=[pl.BlockSpec((tm, d), lambda i: (i, 0)),
                  pl.BlockSpec((1, N_MOD, d), lambda i: (row0 + i // tiles_per_row, 0, 0)),
                  pl.BlockSpec((tm, NA_WIDTH), lambda i: (i, 0)),
                  pl.BlockSpec((tm, S5_WIDTH), lambda i: (i, 0)),
                  pl.BlockSpec((tm, d), lambda i: (i, gate_col)),
                  pl.BlockSpec((tm, d), lambda i: (i, gate_col + 1)),
                  resident((S5_WIDTH, S5_WIDTH), lambda i: (0, 0)),
                  resident((NA_WIDTH, d), lambda i: (0, 0)),
                  resident((S5_WIDTH, d), lambda i: (0, 0)),
                  resident((d, d), lambda i: (0, 0))],
        out_specs=pl.BlockSpec((tm, d), lambda i: (i, 0)),
        out_shape=jax.ShapeDtypeStruct((n_tok, d), F32),
        compiler_params=_cparams(("parallel",)),
        name="merge",
    )(x, mods3, ya, yb, proj, proj, w_glu, w_up_a, w_up_b, w_out)


def kernel(x_prompt, x_sample, cache_k, cache_v, state_ssm, c, c_ctx, w_ada, b_ada, norm_g, ffn_in, ffn_out, w_in, rpb, s5_lam_re, s5_lam_im, s5_log_dt, s5_b_re, s5_b_im, s5_c_re, s5_c_im, s5_d, w_glu, w_up_a, w_up_b, w_out, final_g):
    depth = w_ada.shape[0]
    pb, pl_, d = x_prompt.shape
    sb, sl, _ = x_sample.shape
    u_col = 3 * NA_WIDTH // LANES
    mod_rows = 8 * ((1 + sb + 7) // 8)

    xp = x_prompt.reshape(pb * pl_, d)
    xs = x_sample.reshape(sb * sl, d)
    cond = jnp.concatenate([c_ctx[None, :], c, jnp.zeros((mod_rows - 1 - sb, d), F32)], axis=0)
    ffn_in_bf, ffn_out_bf = ffn_in.astype(BF16), ffn_out.astype(BF16)
    new_k, new_v, new_s = [], [], []
    for l in range(depth):
        last = l == depth - 1
        mods3 = _mods(cond, w_ada[l], b_ada[l]).reshape(mod_rows, N_MOD, d)
        w_in_bf = w_in[l].astype(BF16)
        w_glu_bf, w_up_a_bf, w_up_b_bf, w_out_bf = (w[l].astype(BF16) for w in (w_glu, w_up_a, w_up_b, w_out))
        e, m, f, at = _s5_prep(s5_lam_re[l], s5_lam_im[l], s5_log_dt[l], s5_b_re[l], s5_b_im[l],
                               s5_c_re[l], s5_c_im[l], sl // S5_SEGMENTS)
        bias = _na_bias_tables(rpb[l], sl // GRID_W)

        ptok = pb * pl_
        xp = _ffn(xp, mods3, 0, ptok, norm_g[l, 0], final_g, ffn_in_bf, ffn_out_bf, l, 0, 0, False)
        proj_p = _proj(xp, mods3, 0, ptok, norm_g[l, 1], w_in_bf)
        ya_p, k_p, v_p = _attn_ctx(proj_p, pb, pl_)
        yb_p, fin_p = _s5(proj_p, u_col, jnp.zeros((N_GROUP_SLABS, pb, SLAB_COLS), F32), e, m, f, at,
                          s5_d[l], pb, pl_, math.gcd(pb, 16), 1)
        xp = _merge(xp, mods3, 0, ptok, ya_p, yb_p, proj_p, w_glu_bf, w_up_a_bf, w_up_b_bf, w_out_bf)
        xp = _ffn(xp, mods3, 0, ptok, norm_g[l, 2], final_g, ffn_in_bf, ffn_out_bf, l, 1, 6, last)
        new_k.append(k_p)
        new_v.append(v_p)
        new_s.append(_slabs_to_state(fin_p))

        xs = _ffn(xs, mods3, 1, sl, norm_g[l, 0], final_g, ffn_in_bf, ffn_out_bf, l, 0, 0, False)
        proj_s = _proj(xs, mods3, 1, sl, norm_g[l, 1], w_in_bf)
        ya_s = _attn_na(proj_s, cache_k, cache_v, l, bias, sb, sl)
        yb_s, _ = _s5(proj_s, u_col, _state_to_slabs(state_ssm[:, l]), e, m, f, at, s5_d[l], sb, sl, 1, S5_SEGMENTS)
        xs = _merge(xs, mods3, 1, sl, ya_s, yb_s, proj_s, w_glu_bf, w_up_a_bf, w_up_b_bf, w_out_bf)
        xs = _ffn(xs, mods3, 1, sl, norm_g[l, 2], final_g, ffn_in_bf, ffn_out_bf, l, 1, 6, last)

    y_prompt = xp.reshape(pb, pl_, d)
    y_sample = xs.reshape(sb, sl, d)
    return (y_prompt, y_sample, jnp.stack(new_k, axis=1), jnp.stack(new_v, axis=1), jnp.stack(new_s, axis=1))
```

```python
import functools
import math

import jax
import jax.numpy as jnp
from jax import lax
from jax.experimental import pallas as pl
from jax.experimental.pallas import tpu as pltpu

F32 = jnp.float32
BF16 = jnp.bfloat16

GRID_W = 64
NA_HEADS = 16
NA_HEAD_DIM = 64
NA_WIDTH = NA_HEADS * NA_HEAD_DIM
WIN_R = 8
WIN_C = 16
S5_GROUP_CH = 16
S5_WIDTH = 1024
S5_GROUPS = S5_WIDTH // S5_GROUP_CH
S5_STATE = 64
N_MOD = 9
EPS = 1e-6

LANES = 128
HEADS_PER_SLAB = LANES // NA_HEAD_DIM
N_HEAD_SLABS = NA_WIDTH // LANES
GROUPS_PER_SLAB = LANES // S5_GROUP_CH
N_GROUP_SLABS = S5_WIDTH // LANES
SLAB_STATE = GROUPS_PER_SLAB * S5_STATE
SLAB_COLS = 4 * SLAB_STATE
HALF_LANES = LANES // 2
HALF_STATE = SLAB_STATE // 2
S5_CHUNK = 8
S5_HALF_IN = S5_CHUNK * HALF_LANES
S5_HALF_STATES = 4 * HALF_STATE
S5_SEGMENTS = 8
NA_QROWS = 4
NA_KROWS = NA_QROWS + WIN_R
NEG_BIAS = -1e30
LOG2E = 1.4426950408889634
VMEM_LIMIT = 56 * 1024 * 1024


def _cparams(sem, limit=VMEM_LIMIT):
    return pltpu.CompilerParams(dimension_semantics=sem, vmem_limit_bytes=limit)


def _silu(x):
    return x * jax.nn.sigmoid(x)


NORM_ROWS = 16
NORM_UNROLL = 8


def _norm_mod_store(x_ref, h_ref, g, shift, scale):
    gs = g * (1.0 + scale)

    def body(i, carry):
        r0 = pl.multiple_of(i * NORM_ROWS, NORM_ROWS)
        x = x_ref[pl.ds(r0, NORM_ROWS), :]
        r = lax.rsqrt(jnp.mean(x * x, axis=-1, keepdims=True) + EPS)
        h_ref[pl.ds(r0, NORM_ROWS), :] = ((x * r) * gs + shift).astype(BF16)
        return carry

    lax.fori_loop(0, x_ref.shape[0] // NORM_ROWS, body, 0, unroll=NORM_UNROLL)


def _split_bf16(x):
    hi = x.astype(BF16)
    return hi, (x - hi.astype(F32)).astype(BF16)


def _dot3(a, b):
    a_hi, a_lo = _split_bf16(a)
    b_hi, b_lo = _split_bf16(b)
    return (jnp.dot(a_hi, b_hi, preferred_element_type=F32)
            + (jnp.dot(a_lo, b_hi, preferred_element_type=F32) + jnp.dot(a_hi, b_lo, preferred_element_type=F32)))


def _mods_kernel(c_ref, w_ref, b_ref, o_ref):
    o_ref[...] = _dot3(_silu(c_ref[...]), w_ref[...]) + b_ref[...]


def _mods(cond, w_ada, b_ada):
    rows, d = cond.shape
    n = w_ada.shape[1]
    tn = math.gcd(1024, n)
    return pl.pallas_call(
        _mods_kernel,
        grid=(n // tn,),
        in_specs=[pl.BlockSpec((rows, d), lambda j: (0, 0)),
                  pl.BlockSpec((d, tn), lambda j: (0, j)),
                  pl.BlockSpec((1, tn), lambda j: (0, j))],
        out_specs=pl.BlockSpec((rows, tn), lambda j: (0, j)),
        out_shape=jax.ShapeDtypeStruct((rows, n), F32),
        compiler_params=_cparams(("arbitrary",)),
        name="mods",
    )(cond, w_ada, b_ada.reshape(1, n))


def _ffn_kernel(x_hbm, mods_ref, ng_ref, fg_ref, wg_ref, wu_ref, wo_ref, o_ref, xs_ref, h_ref, rs_ref, sem,
                *, mod_base, final_norm):
    i, j = pl.program_id(0), pl.program_id(1)
    tm = xs_ref.shape[0]

    def x_copy(tile):
        return pltpu.make_async_copy(x_hbm.at[pl.ds(tile * tm, tm), :], xs_ref, sem)

    @pl.when((i == 0) & (j == 0))
    def _():
        x_copy(0).start()

    @pl.when(j == 0)
    def _():
        x_copy(i).wait()
        m = mods_ref[0]
        gs = ng_ref[...] * (1.0 + m[mod_base + 1:mod_base + 2])
        shift = m[mod_base:mod_base + 1]

        def body(r, carry):
            r0 = pl.multiple_of(r * NORM_ROWS, NORM_ROWS)
            x = xs_ref[pl.ds(r0, NORM_ROWS), :]
            o_ref[pl.ds(r0, NORM_ROWS), :] = x
            rs = lax.rsqrt(jnp.mean(x * x, axis=-1, keepdims=True) + EPS)
            h_ref[pl.ds(r0, NORM_ROWS), :] = ((x * rs) * gs + shift).astype(BF16)
            return carry

        lax.fori_loop(0, tm // NORM_ROWS, body, 0, unroll=NORM_UNROLL)

    @pl.when((j == 1) & (i + 1 < pl.num_programs(0)))
    def _():
        x_copy(i + 1).start()

    h = h_ref[...]
    g = jnp.dot(h, wg_ref[...], preferred_element_type=F32)
    u = jnp.dot(h, wu_ref[...], preferred_element_type=F32)
    a = (_silu(g) * u).astype(BF16)
    half_gate = 0.5 * mods_ref[0][mod_base + 2:mod_base + 3]
    o_ref[...] += half_gate * jnp.dot(a, wo_ref[...], preferred_element_type=F32)

    if final_norm:
        @pl.when(j == pl.num_programs(1) - 1)
        def _():
            def body(r, carry):
                r0 = pl.multiple_of(r * NORM_ROWS, NORM_ROWS)
                y = o_ref[pl.ds(r0, NORM_ROWS), :]
                rs = lax.rsqrt(jnp.mean(y * y, axis=-1, keepdims=True) + EPS)
                rs_ref[pl.ds(r0, NORM_ROWS), :] = jnp.broadcast_to(rs, (NORM_ROWS, LANES))
                return carry

            lax.fori_loop(0, tm // NORM_ROWS, body, 0, unroll=NORM_UNROLL)
            for c in range(o_ref.shape[1] // LANES):
                cols = slice(c * LANES, (c + 1) * LANES)
                o_ref[:, cols] = (o_ref[:, cols] * rs_ref[...]) * fg_ref[:, cols]


FFN_TM = 1024
FFN_TF = 512


def _ffn(x, mods3, row0, cond_tokens, norm_g, final_g, w_in_bf, w_out_bf, layer, which, mod_base, final_norm):
    n_tok, d = x.shape
    d_ff = w_out_bf.shape[2]
    tm = min(FFN_TM, n_tok)
    tf = min(FFN_TF, d_ff)
    assert cond_tokens % tm == 0
    tiles_per_row = cond_tokens // tm
    assert d_ff // tf >= 2
    kern = functools.partial(_ffn_kernel, mod_base=mod_base, final_norm=final_norm)
    return pl.pallas_call(
        kern,
        grid=(n_tok // tm, d_ff // tf),
        in_specs=[pl.BlockSpec(memory_space=pl.ANY),
                  pl.BlockSpec((1, N_MOD, d), lambda i, j: (row0 + i // tiles_per_row, 0, 0)),
                  pl.BlockSpec((1, d), lambda i, j: (0, 0)),
                  pl.BlockSpec((1, d), lambda i, j: (0, 0)),
                  pl.BlockSpec((None, None, d, tf), lambda i, j: (layer, which, 0, j)),
                  pl.BlockSpec((None, None, d, tf), lambda i, j: (layer, which, 0, d_ff // tf + j)),
                  pl.BlockSpec((None, None, tf, d), lambda i, j: (layer, which, j, 0))],
        out_specs=pl.BlockSpec((tm, d), lambda i, j: (i, 0)),
        out_shape=jax.ShapeDtypeStruct((n_tok, d), F32),
        scratch_shapes=[pltpu.VMEM((tm, d), F32), pltpu.VMEM((tm, d), BF16), pltpu.VMEM((tm, LANES), F32),
                        pltpu.SemaphoreType.DMA(())],
        compiler_params=_cparams(("arbitrary", "arbitrary")),
        name="ffn",
    )(x, mods3, norm_g.reshape(1, d), final_g.reshape(1, d), w_in_bf, w_in_bf, w_out_bf)


def _proj_kernel(x_hbm, mods_ref, ng_ref, w_ref, o_ref, xs_ref, h_ref, sem):
    i, j = pl.program_id(0), pl.program_id(1)
    tm = xs_ref.shape[0]

    def x_copy(tile):
        return pltpu.make_async_copy(x_hbm.at[pl.ds(tile * tm, tm), :], xs_ref, sem)

    @pl.when((i == 0) & (j == 0))
    def _():
        x_copy(0).start()

    @pl.when(j == 0)
    def _():
        x_copy(i).wait()
        m = mods_ref[0]
        _norm_mod_store(xs_ref, h_ref, ng_ref[...], m[3:4], m[4:5])

    @pl.when((j == 1) & (i + 1 < pl.num_programs(0)))
    def _():
        x_copy(i + 1).start()

    o_ref[...] = jnp.dot(h_ref[...], w_ref[...], preferred_element_type=F32)


def _proj(x, mods3, row0, cond_tokens, norm_g, w_bf):
    n_tok, d = x.shape
    n = w_bf.shape[1]
    tm = min(1024, n_tok)
    tn = math.gcd(2048, d)
    assert cond_tokens % tm == 0 and n // tn >= 2
    tiles_per_row = cond_tokens // tm
    return pl.pallas_call(
        _proj_kernel,
        grid=(n_tok // tm, n // tn),
        in_specs=[pl.BlockSpec(memory_space=pl.ANY),
                  pl.BlockSpec((1, N_MOD, d), lambda i, j: (row0 + i // tiles_per_row, 0, 0)),
                  pl.BlockSpec((1, d), lambda i, j: (0, 0)),
                  pl.BlockSpec((d, tn), lambda i, j: (0, j))],
        out_specs=pl.BlockSpec((tm, tn), lambda i, j: (i, j)),
        out_shape=jax.ShapeDtypeStruct((n_tok, n), F32),
        scratch_shapes=[pltpu.VMEM((tm, d), F32), pltpu.VMEM((tm, d), BF16), pltpu.SemaphoreType.DMA(())],
        compiler_params=_cparams(("arbitrary", "arbitrary")),
        name="proj",
    )(x, mods3, norm_g.reshape(1, d), w_bf)


def _head_mask(shape, hh):
    lane = lax.broadcasted_iota(jnp.int32, shape, 1)
    return (lane >= hh * NA_HEAD_DIM) & (lane < (hh + 1) * NA_HEAD_DIM)


def _nt_dot(a, b):
    return lax.dot_general(a, b, (((1,), (1,)), ((), ())), preferred_element_type=F32)


Q_SCALE = NA_HEAD_DIM ** -0.5 * LOG2E
CTX_SLABS = 8
NA_SLABS = 2
NA_BLOCKS = 4
NA_VMEM_LIMIT = 60 * 1024 * 1024


def _ones_other_head(v, hh):
    return jnp.where(_head_mask(v.shape, hh), v, 1.0)


def _finish_heads(o0, o1):
    first = _head_mask(o0.shape, 0)
    num = jnp.where(first, o0, o1)
    den = pltpu.roll(jnp.where(first, o1, o0), NA_HEAD_DIM, 1)
    return num / den


def _attn_ctx_kernel(q_ref, k_ref, v_ref, o_ref, ko_ref, vo_ref):
    for sl in range(q_ref.shape[1] // LANES):
        cols = slice(sl * LANES, (sl + 1) * LANES)
        q = q_ref[:, cols] * Q_SCALE
        kf = k_ref[:, cols]
        vf = v_ref[:, cols]
        k = kf.astype(BF16)
        outs = []
        for hh in range(HEADS_PER_SLAB):
            head = slice(hh * NA_HEAD_DIM, (hh + 1) * NA_HEAD_DIM)
            ko_ref[0, sl * HEADS_PER_SLAB + hh] = kf[:, head]
            vo_ref[0, sl * HEADS_PER_SLAB + hh] = vf[:, head]
            qm = jnp.where(_head_mask(q.shape, hh), q, 0.0).astype(BF16)
            s = _nt_dot(qm, k)
            p = jnp.exp2(s - jnp.max(s, axis=-1, keepdims=True))
            outs.append(jnp.dot(p.astype(BF16), _ones_other_head(vf, hh).astype(BF16),
                                preferred_element_type=F32))
        o_ref[:, cols] = _finish_heads(*outs)


def _attn_ctx(proj, batch, seq):
    w = CTX_SLABS * LANES
    nsb = NA_WIDTH // w
    hps = CTX_SLABS * HEADS_PER_SLAB
    kv_spec = pl.BlockSpec((1, hps, seq, NA_HEAD_DIM), lambda b, s: (b, s, 0, 0))
    kv_shape = jax.ShapeDtypeStruct((batch, NA_HEADS, seq, NA_HEAD_DIM), F32)
    return pl.pallas_call(
        _attn_ctx_kernel,
        grid=(batch, nsb),
        in_specs=[pl.BlockSpec((seq, w), lambda b, s: (b, s)),
                  pl.BlockSpec((seq, w), lambda b, s: (b, nsb + s)),
                  pl.BlockSpec((seq, w), lambda b, s: (b, 2 * nsb + s))],
        out_specs=[pl.BlockSpec((seq, w), lambda b, s: (b, s)), kv_spec, kv_spec],
        out_shape=[jax.ShapeDtypeStruct((batch * seq, NA_WIDTH), F32), kv_shape, kv_shape],
        compiler_params=_cparams(("parallel", "parallel")),
        name="attn_ctx",
    )(proj, proj, proj)


def _attn_na_kernel(q_ref, k_ref, v_ref, ck_ref, cv_ref, bias_a_ref, bias_mid_ref, bias_b_ref, o_ref,
                    kb_ref, vb_ref, ckb_ref, cvb_ref, *, rows):
    step = pl.program_id(2)
    nsl = q_ref.shape[1] // LANES

    @pl.when(step == 0)
    def _():
        kb_ref[...] = k_ref[...].astype(BF16)
        vb_ref[...] = v_ref[...].astype(BF16)
        for hd in range(ck_ref.shape[0]):
            lanes = slice(hd * NA_HEAD_DIM, (hd + 1) * NA_HEAD_DIM)
            ckb_ref[:, lanes] = ck_ref[hd].astype(BF16)
            cvb_ref[:, lanes] = cv_ref[hd].astype(BF16)

    heads = [(sl, hh) for sl in range(nsl) for hh in range(HEADS_PER_SLAB)]
    cols = [slice(sl * LANES, (sl + 1) * LANES) for sl in range(nsl)]
    tq = NA_QROWS * GRID_W
    for sub, bias_ref in enumerate([bias_a_ref] + [bias_mid_ref] * (NA_BLOCKS - 2) + [bias_b_ref]):
        rb = step * NA_BLOCKS + sub
        qrows = slice(sub * tq, (sub + 1) * tq)
        start = jnp.clip(rb * NA_QROWS - WIN_R // 2, 0, rows - NA_KROWS)
        off = pl.multiple_of(start * GRID_W, NA_QROWS * GRID_W)
        scores = []
        for sl, hh in heads:
            q = q_ref[qrows, cols[sl]] * Q_SCALE
            qm = jnp.where(_head_mask(q.shape, hh), q, 0.0).astype(BF16)
            s_loc = (_nt_dot(qm, kb_ref[pl.ds(off, NA_KROWS * GRID_W), cols[sl]])
                     + bias_ref[0, sl * HEADS_PER_SLAB + hh])
            s_ctx = _nt_dot(qm, ckb_ref[:, cols[sl]])
            scores.append((s_loc, s_ctx))
        probs = []
        for s_loc, s_ctx in scores:
            m = jnp.maximum(jnp.max(s_loc, axis=-1, keepdims=True), jnp.max(s_ctx, axis=-1, keepdims=True))
            p_loc = jnp.exp2(s_loc - m)
            p_ctx = jnp.exp2(s_ctx - m)
            l = jnp.sum(p_loc, axis=-1, keepdims=True) + jnp.sum(p_ctx, axis=-1, keepdims=True)
            probs.append((p_loc.astype(BF16), p_ctx.astype(BF16), l))
        outs = []
        for (sl, hh), (p_loc, p_ctx, l) in zip(heads, probs):
            o = (jnp.dot(p_loc, vb_ref[pl.ds(off, NA_KROWS * GRID_W), cols[sl]], preferred_element_type=F32)
                 + jnp.dot(p_ctx, cvb_ref[:, cols[sl]], preferred_element_type=F32))
            outs.append(o / l)
        for sl in range(nsl):
            first = _head_mask(outs[0].shape, 0)
            o_ref[qrows, cols[sl]] = jnp.where(first, outs[sl * HEADS_PER_SLAB], outs[sl * HEADS_PER_SLAB + 1])


def _na_bias_kernel(rpb_ref, o_ref, *, rows):
    shape = (GRID_W, LANES)
    qc = lax.broadcasted_iota(jnp.int32, shape, 0)
    lane = lax.broadcasted_iota(jnp.int32, shape, 1)
    kc = lane % GRID_W
    cs = jnp.clip(qc - WIN_C // 2, 0, GRID_W - WIN_C)
    valid_c = (kc >= cs) & (kc < cs + WIN_C)
    first_row = lane < GRID_W
    r = rpb_ref[0] * LOG2E
    for p, r0 in enumerate((0, NA_QROWS, rows - NA_QROWS)):
        start = min(max(r0 - WIN_R // 2, 0), rows - NA_KROWS)
        for qr in range(NA_QROWS):
            q_row = r0 + qr
            ws = min(max(q_row - WIN_R // 2, 0), rows - WIN_R)
            for k2 in range(NA_KROWS // 2):
                halves = []
                for kk in range(2):
                    k_row = start + 2 * k2 + kk
                    if ws <= k_row < ws + WIN_R:
                        dr = k_row - q_row + WIN_R - 1
                        row = jnp.broadcast_to(r[dr:dr + 1, :], shape)
                        shift = (LANES - (WIN_C - 1) + GRID_W * kk) % LANES
                        halves.append(pltpu.roll(row, shift, 1, stride=1, stride_axis=0))
                    else:
                        halves.append(jnp.full(shape, NEG_BIAS, F32))
                tile = jnp.where(valid_c, jnp.where(first_row, halves[0], halves[1]), NEG_BIAS)
                o_ref[p, 0, qr * GRID_W:(qr + 1) * GRID_W, k2 * LANES:(k2 + 1) * LANES] = tile


def _na_bias_tables(rpb, rows):
    h, nr, nc = rpb.shape
    rpb_pad = jnp.zeros((h, 16, LANES), F32).at[:, :nr, :nc].set(rpb)
    tq, tk = NA_QROWS * GRID_W, NA_KROWS * GRID_W
    return pl.pallas_call(
        functools.partial(_na_bias_kernel, rows=rows),
        grid=(h,),
        in_specs=[pl.BlockSpec((1, 16, LANES), lambda i: (i, 0, 0))],
        out_specs=pl.BlockSpec((3, 1, tq, tk), lambda i: (0, i, 0, 0)),
        out_shape=jax.ShapeDtypeStruct((3, h, tq, tk), F32),
        compiler_params=_cparams(("parallel",)),
        name="na_bias",
    )(rpb_pad)


def _attn_na(proj, cache_k, cache_v, layer, bias, batch, seq):
    past = cache_k.shape[3]
    rows = seq // GRID_W
    assert rows >= NA_KROWS and rows % (NA_QROWS * NA_BLOCKS) == 0
    nst = rows // (NA_QROWS * NA_BLOCKS)
    tq = NA_BLOCKS * NA_QROWS * GRID_W

    def bias_a_map(b, s, r):
        return ((r > 0).astype(jnp.int32), s, 0, 0)

    def bias_mid_map(b, s, r):
        return (1, s, 0, 0)

    def bias_b_map(b, s, r):
        return (1 + (r == nst - 1).astype(jnp.int32), s, 0, 0)

    w = NA_SLABS * LANES
    nsb = NA_WIDTH // w
    hps = NA_SLABS * HEADS_PER_SLAB
    ctx_spec = pl.BlockSpec((None, None, hps, past, NA_HEAD_DIM), lambda b, s, r: (b, layer, s, 0, 0))
    bias_block = (1, hps, NA_QROWS * GRID_W, NA_KROWS * GRID_W)
    return pl.pallas_call(
        functools.partial(_attn_na_kernel, rows=rows),
        grid=(batch, nsb, nst),
        in_specs=[pl.BlockSpec((tq, w), lambda b, s, r: (b * nst + r, s)),
                  pl.BlockSpec((seq, w), lambda b, s, r: (b, nsb + s)),
                  pl.BlockSpec((seq, w), lambda b, s, r: (b, 2 * nsb + s)),
                  ctx_spec, ctx_spec,
                  pl.BlockSpec(bias_block, bias_a_map), pl.BlockSpec(bias_block, bias_mid_map),
                  pl.BlockSpec(bias_block, bias_b_map)],
        out_specs=pl.BlockSpec((tq, w), lambda b, s, r: (b * nst + r, s)),
        out_shape=jax.ShapeDtypeStruct((batch * seq, NA_WIDTH), F32),
        scratch_shapes=[pltpu.VMEM((seq, w), BF16), pltpu.VMEM((seq, w), BF16),
                        pltpu.VMEM((past, w), BF16), pltpu.VMEM((past, w), BF16)],
        compiler_params=_cparams(("parallel", "parallel", "arbitrary"), NA_VMEM_LIMIT),
        name="attn_na",
    )(proj, proj, proj, cache_k, cache_v, bias, bias, bias)


def _pair_halves(a, b, q, low):
    if q == 0:
        return jnp.where(low, a, pltpu.roll(b, HALF_LANES, 1))
    return jnp.where(low, pltpu.roll(a, HALF_LANES, 1), b)


def _s5_prep_kernel(lr_ref, li_ref, ld_ref, btr_ref, bti_ref, cr_ref, ci_ref,
                    e_ref, m_ref, f_ref, at_ref, *, seg_tokens):
    T = S5_CHUNK
    ns = SLAB_STATE
    rb = lax.broadcasted_iota(jnp.int32, (LANES, ns), 0) // S5_GROUP_CH
    cb = lax.broadcasted_iota(jnp.int32, (LANES, ns), 1) // S5_STATE
    mask_b = rb == cb

    def powers(ldr, ldi, n):
        mag = jnp.exp(ldr * float(n))
        return mag * jnp.cos(ldi * float(n)), mag * jnp.sin(ldi * float(n))

    hl, hs = HALF_LANES, HALF_STATE
    low64 = lax.broadcasted_iota(jnp.int32, (hl, LANES), 1) < hl
    kt = [[None] * T, [None] * T]
    for d in range(2):
        lam_re, lam_im = lr_ref[0, d], li_ref[0, d]
        dt = jnp.exp(ld_ref[0, d])
        ldr, ldi = lam_re * dt, lam_im * dt
        a_re, a_im = powers(ldr, ldi, 1)
        mag2 = lam_re * lam_re + lam_im * lam_im
        f_re = ((a_re - 1.0) * lam_re + a_im * lam_im) / mag2
        f_im = (a_im * lam_re - (a_re - 1.0) * lam_im) / mag2
        btr = jnp.where(mask_b, btr_ref[0, d], 0.0)
        bti = jnp.where(mask_b, bti_ref[0, d], 0.0)
        cr = jnp.where(mask_b, cr_ref[0, d], 0.0)
        ci = jnp.where(mask_b, ci_ref[0, d], 0.0)
        ctr, cti = cr.T, ci.T
        e_res, e_ims = [], []
        for n in range(T):
            an_re, an_im = powers(ldr, ldi, n)
            w_re = f_re * an_re - f_im * an_im
            w_im = f_re * an_im + f_im * an_re
            e_re = btr * w_re - bti * w_im
            e_im = btr * w_im + bti * w_re
            t = T - 1 - n if d == 0 else n
            for q in range(2):
                rows, cols = slice(q * hl, (q + 1) * hl), slice(q * hs, (q + 1) * hs)
                e_ref[0, q, t * hl:(t + 1) * hl, 2 * d * hs:(2 * d + 1) * hs] = e_re[rows, cols].astype(BF16)
                e_ref[0, q, t * hl:(t + 1) * hl, (2 * d + 1) * hs:(2 * d + 2) * hs] = e_im[rows, cols].astype(BF16)
            e_res.append(e_re)
            e_ims.append(e_im)
        k_all = _dot3(jnp.concatenate(e_res, axis=0), ctr) - _dot3(jnp.concatenate(e_ims, axis=0), cti)
        for n in range(T):
            kt[d][n] = k_all[n * LANES:(n + 1) * LANES]
        coef = []
        for tp in range(T):
            n = tp + 1 if d == 0 else T - tp
            an_re, an_im = powers(ldr, ldi, n)
            coef.append((cr * an_re - ci * an_im, -(cr * an_im + ci * an_re)))
        for q in range(2):
            rows, cols = slice(q * hl, (q + 1) * hl), slice(q * hs, (q + 1) * hs)
            for pr in range(T // 2):
                for ri in range(2):
                    tile = jnp.concatenate([coef[2 * pr][ri][rows, cols], coef[2 * pr + 1][ri][rows, cols]], axis=0)
                    f_ref[0, q, (2 * d + ri) * hs:(2 * d + ri + 1) * hs, pr * LANES:(pr + 1) * LANES] = (
                        tile.T.astype(BF16))
        at_re, at_im = powers(ldr, ldi, T)
        at_ref[0, 2 * d:2 * d + 1, :] = at_re
        at_ref[0, 2 * d + 1:2 * d + 2, :] = at_im
        as_re, as_im = powers(ldr, ldi, seg_tokens)
        at_ref[0, 4 + 2 * d:5 + 2 * d, :] = as_re
        at_ref[0, 5 + 2 * d:6 + 2 * d, :] = as_im
    def lag_block(t, tp):
        if tp > t:
            return kt[0][tp - t]
        if tp < t:
            return kt[1][t - tp]
        return kt[0][0] + kt[1][0]

    for q in range(2):
        rows = slice(q * hl, (q + 1) * hl)
        for t in range(T):
            for pr in range(T // 2):
                tile = _pair_halves(lag_block(t, 2 * pr)[rows], lag_block(t, 2 * pr + 1)[rows], q, low64)
                m_ref[0, q, t * hl:(t + 1) * hl, pr * LANES:(pr + 1) * LANES] = tile.astype(BF16)


def _s5_prep(lam_re, lam_im, log_dt, b_re, b_im, c_re, c_im, seg_tokens):
    T, S, G8, ns = S5_CHUNK, N_GROUP_SLABS, GROUPS_PER_SLAB, SLAB_STATE

    def state_vec(v):
        return v.reshape(2, S, ns).transpose(1, 0, 2)

    lr, li = state_vec(lam_re), state_vec(lam_im)
    ld = state_vec(jnp.broadcast_to(log_dt[:, :, None], lam_re.shape))

    def bt(v):
        v = v.reshape(2, S, G8, S5_STATE, S5_GROUP_CH).transpose(1, 0, 2, 4, 3).reshape(S, 2, LANES, S5_STATE)
        return jnp.tile(v, (1, 1, 1, G8))

    def crow(v):
        v = v.reshape(2, S, G8, S5_GROUP_CH, S5_STATE).transpose(1, 0, 2, 3, 4).reshape(S, 2, LANES, S5_STATE)
        return jnp.tile(v, (1, 1, 1, G8))

    row = lambda v: v[:, :, None, :]
    spec_row = pl.BlockSpec((1, 2, 1, ns), lambda j: (j, 0, 0, 0))
    spec_bt = pl.BlockSpec((1, 2, LANES, ns), lambda j: (j, 0, 0, 0))
    return pl.pallas_call(
        functools.partial(_s5_prep_kernel, seg_tokens=seg_tokens),
        grid=(S,),
        in_specs=[spec_row, spec_row, spec_row, spec_bt, spec_bt, spec_bt, spec_bt],
        out_specs=[pl.BlockSpec((1, 2, S5_HALF_IN, S5_HALF_STATES), lambda j: (j, 0, 0, 0)),
                   pl.BlockSpec((1, 2, S5_HALF_IN, S5_HALF_IN), lambda j: (j, 0, 0, 0)),
                   pl.BlockSpec((1, 2, S5_HALF_STATES, S5_HALF_IN), lambda j: (j, 0, 0, 0)),
                   pl.BlockSpec((1, 8, ns), lambda j: (j, 0, 0))],
        out_shape=[jax.ShapeDtypeStruct((S, 2, S5_HALF_IN, S5_HALF_STATES), BF16),
                   jax.ShapeDtypeStruct((S, 2, S5_HALF_IN, S5_HALF_IN), BF16),
                   jax.ShapeDtypeStruct((S, 2, S5_HALF_STATES, S5_HALF_IN), BF16),
                   jax.ShapeDtypeStruct((S, 8, ns), F32)],
        compiler_params=_cparams(("parallel",)),
        name="s5_prep",
    )(row(lr), row(li), row(ld), bt(b_re), bt(b_im), crow(c_re), crow(c_im))


def _s5_kernel(u_ref, h0_ref, e_ref, m_ref, f_ref, at_ref, d_ref, y_ref, fin_ref,
               us_ref, ubf_ref, st_ref, ys_ref, *, bt, nseg, seq):
    T = S5_CHUNK
    ns = SLAB_STATE
    npar = bt * nseg
    seg = seq // nseg
    nc = seg // T
    for p in range(npar):
        for t in range(T):
            us_ref[t, pl.ds(p, nc, stride=npar), :] = u_ref[pl.ds(p * seg + t, nc, stride=T), :]
    hl, hs = HALF_LANES, HALF_STATE
    low = lax.broadcasted_iota(jnp.int32, (us_ref.shape[1], LANES), 1) < hl
    for q in range(2):
        for pr in range(T // 2):
            pair = _pair_halves(us_ref[2 * pr], us_ref[2 * pr + 1], q, low)
            ubf_ref[q, :, pr * LANES:(pr + 1) * LANES] = pair.astype(BF16)
        local = jnp.dot(ubf_ref[q], e_ref[0, q], preferred_element_type=F32)
        for k in range(4):
            st_ref[:, k * ns + q * hs:k * ns + (q + 1) * hs] = local[:, k * hs:(k + 1) * hs]

    at = at_ref[0]
    a_fr, a_fi, a_br, a_bi = at[0:1], at[1:2], at[2:3], at[3:4]

    def scan(init, keep_entering):
        def step(i, carry):
            c_fr, c_fi, c_br, c_bi = carry
            rf = pl.multiple_of(i * npar, npar)
            rbk = pl.multiple_of((nc - 1 - i) * npar, npar)
            l_fr = st_ref[pl.ds(rf, npar), 0:ns]
            l_fi = st_ref[pl.ds(rf, npar), ns:2 * ns]
            l_br = st_ref[pl.ds(rbk, npar), 2 * ns:3 * ns]
            l_bi = st_ref[pl.ds(rbk, npar), 3 * ns:4 * ns]
            if keep_entering:
                st_ref[pl.ds(rf, npar), 0:ns] = c_fr
                st_ref[pl.ds(rf, npar), ns:2 * ns] = c_fi
                st_ref[pl.ds(rbk, npar), 2 * ns:3 * ns] = c_br
                st_ref[pl.ds(rbk, npar), 3 * ns:4 * ns] = c_bi
            return (a_fr * c_fr - a_fi * c_fi + l_fr, a_fr * c_fi + a_fi * c_fr + l_fi,
                    a_br * c_br - a_bi * c_bi + l_br, a_br * c_bi + a_bi * c_br + l_bi)

        return lax.fori_loop(0, nc, step, init, unroll=2 if npar <= 8 else 1)

    h0 = h0_ref[0, 0]
    h0 = [h0[:, k * ns:(k + 1) * ns] for k in range(4)]
    if nseg == 1:
        enter = tuple(h0)
    else:
        zero = jnp.zeros((npar, ns), F32)
        z_fr, z_fi, z_br, z_bi = scan((zero, zero, zero, zero), False)
        s_fr, s_fi, s_br, s_bi = at[4:5], at[5:6], at[6:7], at[7:8]
        rows_f, rows_b = [None] * npar, [None] * npar
        for b in range(bt):
            c_re, c_im = h0[0][b:b + 1], h0[1][b:b + 1]
            for s in range(nseg):
                p = b * nseg + s
                rows_f[p] = (c_re, c_im)
                c_re, c_im = (s_fr * c_re - s_fi * c_im + z_fr[p:p + 1], s_fr * c_im + s_fi * c_re + z_fi[p:p + 1])
            c_re, c_im = h0[2][b:b + 1], h0[3][b:b + 1]
            for s in reversed(range(nseg)):
                p = b * nseg + s
                rows_b[p] = (c_re, c_im)
                c_re, c_im = (s_br * c_re - s_bi * c_im + z_br[p:p + 1], s_br * c_im + s_bi * c_re + z_bi[p:p + 1])
        enter = (jnp.concatenate([r[0] for r in rows_f], axis=0), jnp.concatenate([r[1] for r in rows_f], axis=0),
                 jnp.concatenate([r[0] for r in rows_b], axis=0), jnp.concatenate([r[1] for r in rows_b], axis=0))
    fin = scan(enter, True)
    for b in range(bt):
        last_f, last_b = b * nseg + nseg - 1, b * nseg
        fin_ref[0, 0, b:b + 1, 0:ns] = fin[0][last_f:last_f + 1]
        fin_ref[0, 0, b:b + 1, ns:2 * ns] = fin[1][last_f:last_f + 1]
        fin_ref[0, 0, b:b + 1, 2 * ns:3 * ns] = fin[2][last_b:last_b + 1]
        fin_ref[0, 0, b:b + 1, 3 * ns:4 * ns] = fin[3][last_b:last_b + 1]

    yq = []
    for q in range(2):
        entering = jnp.concatenate([st_ref[:, k * ns + q * hs:k * ns + (q + 1) * hs] for k in range(4)], axis=1)
        yq.append(jnp.dot(ubf_ref[q], m_ref[0, q], preferred_element_type=F32)
                  + jnp.dot(entering.astype(BF16), f_ref[0, q], preferred_element_type=F32))
    for pr in range(T // 2):
        y0 = yq[0][:, pr * LANES:(pr + 1) * LANES]
        y1 = yq[1][:, pr * LANES:(pr + 1) * LANES]
        ys_ref[2 * pr] = jnp.where(low, y0, pltpu.roll(y1, hl, 1)) + d_ref[0] * us_ref[2 * pr]
        ys_ref[2 * pr + 1] = jnp.where(low, pltpu.roll(y0, hl, 1), y1) + d_ref[0] * us_ref[2 * pr + 1]
    for p in range(npar):
        for t in range(T):
            y_ref[pl.ds(p * seg + t, nc, stride=T), :] = ys_ref[t, pl.ds(p, nc, stride=npar), :]


def _s5(proj, u_col, h0, e, m, f, at, d_skip, batch, seq, bt, nseg):
    T, S = S5_CHUNK, N_GROUP_SLABS
    rows = bt * seq // T
    const = lambda j, g: (j, 0, 0)
    const4 = lambda j, g: (j, 0, 0, 0)
    y, fin = pl.pallas_call(
        functools.partial(_s5_kernel, bt=bt, nseg=nseg, seq=seq),
        grid=(S, batch // bt),
        in_specs=[pl.BlockSpec((bt * seq, LANES), lambda j, g: (g, u_col + j)),
                  pl.BlockSpec((1, 1, bt, SLAB_COLS), lambda j, g: (j, g, 0, 0)),
                  pl.BlockSpec((1, 2, S5_HALF_IN, S5_HALF_STATES), const4),
                  pl.BlockSpec((1, 2, S5_HALF_IN, S5_HALF_IN), const4),
                  pl.BlockSpec((1, 2, S5_HALF_STATES, S5_HALF_IN), const4),
                  pl.BlockSpec((1, 8, SLAB_STATE), const),
                  pl.BlockSpec((1, 1, LANES), const)],
        out_specs=[pl.BlockSpec((bt * seq, LANES), lambda j, g: (g, j)),
                   pl.BlockSpec((1, 1, bt, SLAB_COLS), lambda j, g: (j, g, 0, 0))],
        out_shape=[jax.ShapeDtypeStruct((batch * seq, S5_WIDTH), F32),
                   jax.ShapeDtypeStruct((S, batch // bt, bt, SLAB_COLS), F32)],
        scratch_shapes=[pltpu.VMEM((T, rows, LANES), F32), pltpu.VMEM((2, rows, S5_HALF_IN), BF16),
                        pltpu.VMEM((rows, SLAB_COLS), F32), pltpu.VMEM((T, rows, LANES), F32)],
        compiler_params=_cparams(("parallel", "arbitrary")),
        name="s5",
    )(proj, h0.reshape(S, batch // bt, bt, SLAB_COLS), e, m, f, at, d_skip.reshape(S, 1, LANES))
    return y, fin.reshape(S, batch, SLAB_COLS)


def _state_to_slabs(state):
    b = state.shape[0]
    s = state.reshape(b, 2, 2, N_GROUP_SLABS, SLAB_STATE).transpose(3, 0, 1, 2, 4)
    return s.reshape(N_GROUP_SLABS, b, SLAB_COLS)


def _slabs_to_state(slabs):
    b = slabs.shape[1]
    s = slabs.reshape(N_GROUP_SLABS, b, 2, 2, SLAB_STATE).transpose(1, 2, 3, 0, 4)
    return s.reshape(b, 2, 2, S5_GROUPS, S5_STATE)


def _merge_kernel(x_ref, mods_ref, ya_ref, yb_ref, ga_ref, gb_ref, wglu_ref, wua_ref, wub_ref, wo_ref, o_ref):
    yb = jax.nn.gelu(yb_ref[...])
    glu = jnp.dot(yb.astype(BF16), wglu_ref[...], preferred_element_type=F32)
    yb = yb * jax.nn.sigmoid(glu)
    a = jnp.dot(ya_ref[...].astype(BF16), wua_ref[...], preferred_element_type=F32)
    b = jnp.dot(yb.astype(BF16), wub_ref[...], preferred_element_type=F32)
    merged = jax.nn.sigmoid(ga_ref[...]) * a + jax.nn.sigmoid(gb_ref[...]) * b
    o = jnp.dot(merged.astype(BF16), wo_ref[...], preferred_element_type=F32)
    o_ref[...] = x_ref[...] + mods_ref[0][5:6] * o


def _merge(x, mods3, row0, cond_tokens, ya, yb, proj, w_glu, w_up_a, w_up_b, w_out):
    n_tok, d = x.shape
    tm = min(256, n_tok)
    assert cond_tokens % tm == 0
    tiles_per_row = cond_tokens // tm
    gate_col = (3 * NA_WIDTH + S5_WIDTH) // d
    assert gate_col * d == 3 * NA_WIDTH + S5_WIDTH
    resident = functools.partial(pl.BlockSpec, pipeline_mode=pl.Buffered(1))
    return pl.pallas_call(
        _merge_kernel,
        grid=(n_tok // tm,),
        in_specs=[pl.BlockSpec((tm, d), lambda i: (i, 0)),
                  pl.BlockSpec((1, N_MOD, d), lambda i: (row0 + i // tiles_per_row, 0, 0)),
                  pl.BlockSpec((tm, NA_WIDTH), lambda i: (i, 0)),
                  pl.BlockSpec((tm, S5_WIDTH), lambda i: (i, 0)),
                  pl.BlockSpec((tm, d), lambda i: (i, gate_col)),
                  pl.BlockSpec((tm, d), lambda i: (i, gate_col + 1)),
                  resident((S5_WIDTH, S5_WIDTH), lambda i: (0, 0)),
                  resident((NA_WIDTH, d), lambda i: (0, 0)),
                  resident((S5_WIDTH, d), lambda i: (0, 0)),
                  resident((d, d), lambda i: (0, 0))],
        out_specs=pl.BlockSpec((tm, d), lambda i: (i, 0)),
        out_shape=jax.ShapeDtypeStruct((n_tok, d), F32),
        compiler_params=_cparams(("parallel",)),
        name="merge",
    )(x, mods3, ya, yb, proj, proj, w_glu, w_up_a, w_up_b, w_out)


def kernel(x_prompt, x_sample, cache_k, cache_v, state_ssm, c, c_ctx, w_ada, b_ada, norm_g, ffn_in, ffn_out, w_in, rpb, s5_lam_re, s5_lam_im, s5_log_dt, s5_b_re, s5_b_im, s5_c_re, s5_c_im, s5_d, w_glu, w_up_a, w_up_b, w_out, final_g):
    depth = w_ada.shape[0]
    pb, pl_, d = x_prompt.shape
    sb, sl, _ = x_sample.shape
    u_col = 3 * NA_WIDTH // LANES
    mod_rows = 8 * ((1 + sb + 7) // 8)

    xp = x_prompt.reshape(pb * pl_, d)
    xs = x_sample.reshape(sb * sl, d)
    cond = jnp.concatenate([c_ctx[None, :], c, jnp.zeros((mod_rows - 1 - sb, d), F32)], axis=0)
    ffn_in_bf, ffn_out_bf = ffn_in.astype(BF16), ffn_out.astype(BF16)
    new_k, new_v, new_s = [], [], []
    for l in range(depth):
        last = l == depth - 1
        mods3 = _mods(cond, w_ada[l], b_ada[l]).reshape(mod_rows, N_MOD, d)
        w_in_bf = w_in[l].astype(BF16)
        w_glu_bf, w_up_a_bf, w_up_b_bf, w_out_bf = (w[l].astype(BF16) for w in (w_glu, w_up_a, w_up_b, w_out))
        e, m, f, at = _s5_prep(s5_lam_re[l], s5_lam_im[l], s5_log_dt[l], s5_b_re[l], s5_b_im[l],
                               s5_c_re[l], s5_c_im[l], sl // S5_SEGMENTS)
        bias = _na_bias_tables(rpb[l], sl // GRID_W)

        ptok = pb * pl_
        xp = _ffn(xp, mods3, 0, ptok, norm_g[l, 0], final_g, ffn_in_bf, ffn_out_bf, l, 0, 0, False)
        proj_p = _proj(xp, mods3, 0, ptok, norm_g[l, 1], w_in_bf)
        ya_p, k_p, v_p = _attn_ctx(proj_p, pb, pl_)
        yb_p, fin_p = _s5(proj_p, u_col, jnp.zeros((N_GROUP_SLABS, pb, SLAB_COLS), F32), e, m, f, at,
                          s5_d[l], pb, pl_, math.gcd(pb, 16), 1)
        xp = _merge(xp, mods3, 0, ptok, ya_p, yb_p, proj_p, w_glu_bf, w_up_a_bf, w_up_b_bf, w_out_bf)
        xp = _ffn(xp, mods3, 0, ptok, norm_g[l, 2], final_g, ffn_in_bf, ffn_out_bf, l, 1, 6, last)
        new_k.append(k_p)
        new_v.append(v_p)
        new_s.append(_slabs_to_state(fin_p))

        xs = _ffn(xs, mods3, 1, sl, norm_g[l, 0], final_g, ffn_in_bf, ffn_out_bf, l, 0, 0, False)
        proj_s = _proj(xs, mods3, 1, sl, norm_g[l, 1], w_in_bf)
        ya_s = _attn_na(proj_s, cache_k, cache_v, l, bias, sb, sl)
        yb_s, _ = _s5(proj_s, u_col, _state_to_slabs(state_ssm[:, l]), e, m, f, at, s5_d[l], sb, sl, 1, S5_SEGMENTS)
        xs = _merge(xs, mods3, 1, sl, ya_s, yb_s, proj_s, w_glu_bf, w_up_a_bf, w_up_b_bf, w_out_bf)
        xs = _ffn(xs, mods3, 1, sl, norm_g[l, 2], final_g, ffn_in_bf, ffn_out_bf, l, 1, 6, last)

    y_prompt = xp.reshape(pb, pl_, d)
    y_sample = xs.reshape(sb, sl, d)
    return (y_prompt, y_sample, jnp.stack(new_k, axis=1), jnp.stack(new_v, axis=1), jnp.stack(new_s, axis=1))
```
